```python
import functools
import jax, jax.numpy as jnp
from jax import lax
import numpy as np

D_MODEL = 1024
BATCH = 2
SEQ = 8192
DEPTH = 1
DEC_BATCH = 128
DEC_SEQ = 8
PAST_LEN = 2048
PAGE_SIZE = 128

HEAD_DIM = 64
N_HEADS = D_MODEL // HEAD_DIM
H_ATT = N_HEADS // 2
H_GDN = N_HEADS - H_ATT
D_ATT = H_ATT * HEAD_DIM
D_GDN = H_GDN * HEAD_DIM
MOBA_BLOCK = 256
MOBA_TOPK = 3
Q_BLOCK = 128
GDN_CHUNK = 64
CONV_W = 4
D_FF = -(-8 * D_MODEL // (3 * 256)) * 256
RMS_EPS = 1e-6
L2_EPS = 1e-6
ATT_SCALE = HEAD_DIM ** -0.5
SPLITS = [D_ATT, 2 * D_ATT, 3 * D_ATT, 3 * D_ATT + 3 * D_GDN, 3 * D_ATT + 4 * D_GDN,
          3 * D_ATT + 4 * D_GDN + H_GDN]
D_IN = 3 * D_ATT + 4 * D_GDN + 2 * H_GDN

kernel_name = "hymba_moba_gdn_decode_step"


def rms_norm(x, g):
    xf = x.astype(jnp.float32)
    y = xf * lax.rsqrt(jnp.mean(xf * xf, axis=-1, keepdims=True) + RMS_EPS)
    return (y * g.astype(jnp.float32)).astype(x.dtype)


def l2_norm(x):
    return x * lax.rsqrt(jnp.sum(x * x, axis=-1, keepdims=True) + L2_EPS)


def alibi_slopes():
    return jnp.asarray(2.0 ** (-8.0 * np.arange(1, H_ATT + 1) / H_ATT), dtype=jnp.float32)


def moba_prompt(q, k, v):
    B, H, S, Dh = q.shape
    n_blk = max(-(-S // MOBA_BLOCK), MOBA_TOPK)
    pad = n_blk * MOBA_BLOCK - S
    kpad = jnp.pad(k, ((0, 0), (0, 0), (0, pad), (0, 0)))
    vpad = jnp.pad(v, ((0, 0), (0, 0), (0, pad), (0, 0)))
    kb = kpad.reshape(B, H, n_blk, MOBA_BLOCK, Dh)
    vb = vpad.reshape(B, H, n_blk, MOBA_BLOCK, Dh)
    kmean = kb.astype(jnp.float32).mean(axis=3)
    slopes = alibi_slopes()
    bi = jnp.arange(B)[:, None, None, None]
    hi = jnp.arange(H)[None, :, None, None]

    def one_block(j):
        t0 = j * Q_BLOCK
        own = t0 // MOBA_BLOCK
        tpos = t0 + jnp.arange(Q_BLOCK)
        qj = lax.dynamic_slice_in_dim(q, t0, Q_BLOCK, axis=2).astype(jnp.float32)
        gate = jnp.einsum('bhqd,bhnd->bhqn', qj, kmean)
        gate = jnp.where(jnp.arange(n_blk) < own, gate, -jnp.inf)
        _, sel = lax.top_k(gate, MOBA_TOPK)
        k_sel = kb[bi, hi, sel]
        v_sel = vb[bi, hi, sel]
        s_sel = sel[..., None] * MOBA_BLOCK + jnp.arange(MOBA_BLOCK)
        dist = (tpos[:, None, None] - s_sel).astype(jnp.float32)
        l_sel = jnp.einsum('bhqd,bhqkmd->bhqkm', qj, k_sel.astype(jnp.float32)) * ATT_SCALE \
            - slopes[:, None, None, None] * dist
        l_sel = jnp.where((jnp.arange(MOBA_TOPK) < own)[:, None], l_sel, -jnp.inf)
        k_own = lax.dynamic_slice_in_dim(kpad, own * MOBA_BLOCK, MOBA_BLOCK, axis=2)
        v_own = lax.dynamic_slice_in_dim(vpad, own * MOBA_BLOCK, MOBA_BLOCK, axis=2)
        s_own = own * MOBA_BLOCK + jnp.arange(MOBA_BLOCK)
        d_own = (tpos[:, None] - s_own[None, :]).astype(jnp.float32)
        l_own = jnp.einsum('bhqd,bhkd->bhqk', qj, k_own.astype(jnp.float32)) * ATT_SCALE \
            - slopes[:, None, None] * d_own
        l_own = jnp.where(s_own[None, :] <= tpos[:, None], l_own, -jnp.inf)
        n_sel = MOBA_TOPK * MOBA_BLOCK
        p = jax.nn.softmax(jnp.concatenate([l_sel.reshape(B, H, Q_BLOCK, n_sel), l_own], -1), axis=-1)
        p_sel = p[..., :n_sel].reshape(B, H, Q_BLOCK, MOBA_TOPK, MOBA_BLOCK)
        return jnp.einsum('bhqkm,bhqkmd->bhqd', p_sel, v_sel.astype(jnp.float32)) \
            + jnp.einsum('bhqk,bhkd->bhqd', p[..., n_sel:], v_own.astype(jnp.float32))

    o = lax.map(one_block, jnp.arange(S // Q_BLOCK))
    return o.transpose(1, 2, 0, 3, 4).reshape(B, H, S, Dh)


def moba_sample(q, k, v, cache_k, cache_v, page_table):
    T = q.shape[2]
    own = PAST_LEN // MOBA_BLOCK
    r = PAST_LEN - own * MOBA_BLOCK
    n_sel = min(MOBA_TOPK, own)
    slopes = alibi_slopes()
    tpos = PAST_LEN + jnp.arange(T)
    s_own = PAST_LEN - r + jnp.arange(r + T)
    d_own = (tpos[:, None] - s_own[None, :]).astype(jnp.float32)
    own_bias = jnp.where(s_own[None, :] <= tpos[:, None], -slopes[:, None, None] * d_own, -jnp.inf)
    hi = jnp.arange(H_ATT)[:, None, None]

    def one_seq(args):
        qs, kn, vn, pages = args
        kp = cache_k[pages].transpose(1, 0, 2, 3).reshape(H_ATT, PAST_LEN, HEAD_DIM)
        vp = cache_v[pages].transpose(1, 0, 2, 3).reshape(H_ATT, PAST_LEN, HEAD_DIM)
        qf = qs.astype(jnp.float32)
        k_own = jnp.concatenate([kp[:, PAST_LEN - r:].astype(jnp.float32), kn.astype(jnp.float32)], axis=1)
        v_own = jnp.concatenate([vp[:, PAST_LEN - r:].astype(jnp.float32), vn.astype(jnp.float32)], axis=1)
        l_own = jnp.einsum('hqd,hkd->hqk', qf, k_own) * ATT_SCALE + own_bias
        if n_sel == 0:
            return jnp.einsum('hqk,hkd->hqd', jax.nn.softmax(l_own, axis=-1), v_own)
        kb = kp[:, :own * MOBA_BLOCK].reshape(H_ATT, own, MOBA_BLOCK, HEAD_DIM)
        vb = vp[:, :own * MOBA_BLOCK].reshape(H_ATT, own, MOBA_BLOCK, HEAD_DIM)
        kmean = kb.astype(jnp.float32).mean(axis=2)
        gate = jnp.einsum('hqd,hnd->hqn', qf, kmean)
        _, sel = lax.top_k(gate, n_sel)
        k_sel = kb[hi, sel].astype(jnp.float32)
        v_sel = vb[hi, sel].astype(jnp.float32)
        s_sel = sel[..., None] * MOBA_BLOCK + jnp.arange(MOBA_BLOCK)
        dist = (tpos[:, None, None] - s_sel).astype(jnp.float32)
        l_sel = jnp.einsum('hqd,hqkmd->hqkm', qf, k_sel) * ATT_SCALE - slopes[:, None, None, None] * dist
        n_rows = n_sel * MOBA_BLOCK
        p = jax.nn.softmax(jnp.concatenate([l_sel.reshape(H_ATT, T, n_rows), l_own], -1), axis=-1)
        p_sel = p[..., :n_rows].reshape(H_ATT, T, n_sel, MOBA_BLOCK)
        return jnp.einsum('hqkm,hqkmd->hqd', p_sel, v_sel) + jnp.einsum('hqk,hkd->hqd', p[..., n_rows:], v_own)

    return lax.map(one_seq, (q, k, v, page_table))


def short_conv(u, buf, w):
    L = u.shape[1]
    up = jnp.concatenate([buf.astype(u.dtype), u], axis=1)
    y = sum(up[:, i:i + L].astype(jnp.float32) * w[i].astype(jnp.float32) for i in range(CONV_W))
    return jax.nn.silu(y), up[:, L:]


def gated_delta_chunked(q, k, v, g, beta, m0):
    B, L, H, Dh = q.shape
    C = GDN_CHUNK
    n = -(-L // C)
    pad = n * C - L

    def blocks(t):
        t = jnp.pad(t, ((0, 0), (0, pad)) + ((0, 0),) * (t.ndim - 2))
        t = jnp.moveaxis(t, 2, 1)
        return t.reshape((B, H, n, C) + t.shape[3:])

    q, k, v, g, beta = blocks(q), blocks(k), blocks(v), blocks(g), blocks(beta)
    G = jnp.cumsum(g, axis=-1)
    tri_incl = jnp.tril(jnp.ones((C, C), dtype=bool))
    tri_strict = jnp.tril(jnp.ones((C, C), dtype=bool), -1)
    decay = jnp.exp(jnp.where(tri_incl, G[..., :, None] - G[..., None, :], -jnp.inf))
    kk = jnp.einsum('bhnid,bhnjd->bhnij', k, k)
    a = beta[..., :, None] * kk * jnp.where(tri_strict, decay, 0.0)
    rhs = jnp.concatenate([beta[..., None] * v, (beta * jnp.exp(G))[..., None] * k], axis=-1)
    sol = lax.linalg.triangular_solve(jnp.eye(C, dtype=jnp.float32) + a, rhs,
                                      left_side=True, lower=True, unit_diagonal=True)
    u, wk = sol[..., :Dh], sol[..., Dh:]
    aqk = jnp.einsum('bhnid,bhnjd->bhnij', q, k) * decay
    qg = q * jnp.exp(G)[..., None]
    kd = k * jnp.exp(G[..., -1:] - G)[..., None]
    gc = jnp.exp(G[..., -1])
    xs = tuple(jnp.moveaxis(t, 2, 0) for t in (u, wk, aqk, qg, kd, gc))

    def step(m, xc):
        u_c, wk_c, aqk_c, qg_c, kd_c, gc_c = xc
        delta = u_c - jnp.einsum('bhcd,bhde->bhce', wk_c, m)
        o_c = jnp.einsum('bhcd,bhde->bhce', qg_c, m) + jnp.einsum('bhij,bhje->bhie', aqk_c, delta)
        m = gc_c[..., None, None] * m + jnp.einsum('bhcd,bhce->bhde', kd_c, delta)
        return m, o_c

    m_fin, o = lax.scan(step, m0, xs)
    o = jnp.moveaxis(o, 0, 2).reshape(B, H, n * C, Dh)[:, :, :L]
    return jnp.moveaxis(o, 1, 2), m_fin


def decoder_layer(x, attend, conv_buf, ssm0, norm_mix, w_in, q_norm_g, k_norm_g, attn_out_g,
                  conv_w, a_log, dt_bias, gdn_out_g, w_out, norm_ffn, w_gate, w_up, w_down):
    B, L, _ = x.shape
    z = rms_norm(x, norm_mix) @ w_in
    aq, ak, av, gqkv, ggate, gbeta, galpha = jnp.split(z, SPLITS, axis=-1)

    def heads(t):
        return t.reshape(B, L, H_ATT, HEAD_DIM).transpose(0, 2, 1, 3)

    q = rms_norm(heads(aq), q_norm_g)
    k = rms_norm(heads(ak), k_norm_g)
    v = heads(av)
    o_att = attend(q, k, v)
    o_att = rms_norm(o_att.transpose(0, 2, 1, 3).reshape(B, L, D_ATT).astype(x.dtype), attn_out_g)

    qkv, conv_new = short_conv(gqkv, conv_buf, conv_w)
    qkv = qkv.reshape(B, L, 3, H_GDN, HEAD_DIM)
    gq = l2_norm(qkv[:, :, 0]) * (HEAD_DIM ** -0.5)
    gk = l2_norm(qkv[:, :, 1])
    gv = qkv[:, :, 2]
    beta = jax.nn.sigmoid(gbeta.astype(jnp.float32))
    g = -jnp.exp(a_log.astype(jnp.float32)) * jax.nn.softplus(galpha.astype(jnp.float32) + dt_bias.astype(jnp.float32))
    o_g, ssm_new = gated_delta_chunked(gq, gk, gv, g, beta, ssm0.astype(jnp.float32))
    o_g = rms_norm(o_g, gdn_out_g) * jax.nn.silu(ggate.astype(jnp.float32).reshape(B, L, H_GDN, HEAD_DIM))
    o_g = o_g.reshape(B, L, D_GDN).astype(x.dtype)

    h = x + jnp.concatenate([o_att, o_g], axis=-1) @ w_out
    u = rms_norm(h, norm_ffn)
    y = h + (jax.nn.silu(u @ w_gate) * (u @ w_up)) @ w_down
    return y, k.astype(x.dtype), v.astype(x.dtype), conv_new, ssm_new.astype(ssm0.dtype)


def setup_inputs(seed: int = 0) -> dict:
    key = jax.random.key(seed)
    ks = jax.random.split(key, 24)
    f32 = jnp.float32
    n_pages = PAST_LEN // PAGE_SIZE
    n_used = DEC_BATCH * n_pages
    n_pool = n_used + -(-n_used // 4)

    def nrm(k, shape, s=1.0):
        return s * jax.random.normal(k, shape, f32)

    dt = jnp.exp(jax.random.uniform(ks[12], (DEPTH, H_GDN), f32, np.log(1e-3), np.log(1e-1)))
    return {
        "x_prompt": nrm(ks[0], (BATCH, SEQ, D_MODEL)),
        "x_sample": nrm(ks[1], (DEC_BATCH, DEC_SEQ, D_MODEL)),
        "cache_k": nrm(ks[2], (DEPTH, n_pool, H_ATT, PAGE_SIZE, HEAD_DIM)),
        "cache_v": nrm(ks[3], (DEPTH, n_pool, H_ATT, PAGE_SIZE, HEAD_DIM)),
        "state_ssm": nrm(ks[4], (DEPTH, DEC_BATCH, H_GDN, HEAD_DIM, HEAD_DIM), 0.1),
        "state_conv": nrm(ks[5], (DEPTH, DEC_BATCH, CONV_W - 1, 3 * D_GDN)),
        "page_table": jax.random.permutation(ks[6], n_pool)[:n_used].reshape(DEC_BATCH, n_pages).astype(jnp.int32),
        "norm_mix": 1.0 + nrm(ks[7], (DEPTH, D_MODEL), 0.01),
        "w_in": nrm(ks[8], (DEPTH, D_MODEL, D_IN), D_MODEL ** -0.5),
        "q_norm_g": 1.0 + nrm(ks[9], (DEPTH, HEAD_DIM), 0.01),
        "k_norm_g": 1.0 + nrm(ks[10], (DEPTH, HEAD_DIM), 0.01),
        "attn_out_g": 1.0 + nrm(ks[11], (DEPTH, D_ATT), 0.01),
        "conv_w": nrm(ks[13], (DEPTH, CONV_W, 3 * D_GDN), CONV_W ** -0.5),
        "a_log": jnp.log(jax.random.uniform(ks[14], (DEPTH, H_GDN), f32, 1.0, 16.0)),
        "dt_bias": dt + jnp.log(-jnp.expm1(-dt)),
        "gdn_out_g": 1.0 + nrm(ks[15], (DEPTH, HEAD_DIM), 0.01),
        "w_out": nrm(ks[16], (DEPTH, D_MODEL, D_MODEL), D_MODEL ** -0.5),
        "norm_ffn": 1.0 + nrm(ks[17], (DEPTH, D_MODEL), 0.01),
        "w_gate": nrm(ks[18], (DEPTH, D_MODEL, D_FF), D_MODEL ** -0.5),
        "w_up": nrm(ks[19], (DEPTH, D_MODEL, D_FF), D_MODEL ** -0.5),
        "w_down": nrm(ks[20], (DEPTH, D_FF, D_MODEL), D_FF ** -0.5),
    }


def reference(x_prompt, x_sample, cache_k, cache_v, state_ssm, state_conv, page_table,
              norm_mix, w_in, q_norm_g, k_norm_g, attn_out_g, conv_w, a_log, dt_bias,
              gdn_out_g, w_out, norm_ffn, w_gate, w_up, w_down):
    yp, ys = x_prompt, x_sample
    kp_l, vp_l, ks_l, vs_l, sp_l, ss_l, cp_l, cs_l = [], [], [], [], [], [], [], []
    for layer in range(DEPTH):
        lw = (norm_mix[layer], w_in[layer], q_norm_g[layer], k_norm_g[layer], attn_out_g[layer],
              conv_w[layer], a_log[layer], dt_bias[layer], gdn_out_g[layer], w_out[layer],
              norm_ffn[layer], w_gate[layer], w_up[layer], w_down[layer])
        conv0 = jnp.zeros((BATCH, CONV_W - 1, 3 * D_GDN), x_prompt.dtype)
        ssm0 = jnp.zeros((BATCH, H_GDN, HEAD_DIM, HEAD_DIM), state_ssm.dtype)
        yp, kp, vp, cp, sp = decoder_layer(yp, moba_prompt, conv0, ssm0, *lw)
        attend_s = functools.partial(moba_sample, cache_k=cache_k[layer], cache_v=cache_v[layer],
                                     page_table=page_table)
        ys, kS, vS, cS, sS = decoder_layer(ys, attend_s, state_conv[layer], state_ssm[layer], *lw)
        kp_l.append(kp); vp_l.append(vp); ks_l.append(kS); vs_l.append(vS)
        sp_l.append(sp); ss_l.append(sS); cp_l.append(cp); cs_l.append(cS)
    return (yp, ys, jnp.stack(kp_l), jnp.stack(vp_l), jnp.stack(ks_l), jnp.stack(vs_l),
            jnp.stack(sp_l), jnp.stack(ss_l), jnp.stack(cp_l), jnp.stack(cs_l))
```

```python
import functools

import numpy as np
import jax
import jax.numpy as jnp
from jax import lax
from jax.experimental import pallas as pl
from jax.experimental.pallas import tpu as pltpu

f32 = jnp.float32
bf16 = jnp.bfloat16

D_MODEL = 1024
HEAD_DIM = 64
H_ATT = 8
H_GDN = 8
D_ATT = H_ATT * HEAD_DIM
D_GDN = H_GDN * HEAD_DIM
MOBA_BLOCK = 256
MOBA_TOPK = 3
GDN_CHUNK = 64
CONV_W = 4
PAGE_SIZE = 128
RMS_EPS = 1e-6
L2_EPS = 1e-6
ATT_SCALE = HEAD_DIM ** -0.5
NEG = -1e30
LANES = 128
VMEM_LIMIT = 56 * 1024 * 1024

HI = lax.Precision.HIGHEST
NN = (((1,), (0,)), ((), ()))
NT = (((1,), (1,)), ((), ()))
TN = (((0,), (0,)), ((), ()))

ALIBI_SLOPES = [2.0 ** (-8.0 * (i + 1) / H_ATT) for i in range(H_ATT)]


def _dot(a, b, dims=NN, precision=None):
    return lax.dot_general(a, b, dims, precision=precision, preferred_element_type=f32)


def _bdot(a, b, dims=NN):
    return lax.dot_general(a.astype(bf16), b.astype(bf16), dims, preferred_element_type=f32)


def _rms(x, axis=-1):
    return x * lax.rsqrt(jnp.mean(x * x, axis=axis, keepdims=True) + RMS_EPS)


def _silu(x):
    return x * jax.nn.sigmoid(x)


def _iota(shape, dim):
    return lax.broadcasted_iota(jnp.int32, shape, dim)


def _in_proj_body(x_ref, nm_ref, wqkv_ref, wg_ref, wgate_ref, wba_ref, qg_ref, kg_ref,
                  q_ref, k_ref, v_ref, g_ref, gate_ref, ba_ref, *, head_major):
    x = x_ref[...]
    xb = (_rms(x) * nm_ref[...]).astype(bf16)
    z = jnp.dot(xb, wqkv_ref[...], preferred_element_type=f32)
    qs, ks = [], []
    for h in range(H_ATT):
        lo, hi = h * HEAD_DIM, (h + 1) * HEAD_DIM
        qh = _rms(z[:, lo:hi]) * qg_ref[...]
        kh = _rms(z[:, D_ATT + lo:D_ATT + hi]) * kg_ref[...]
        if head_major:
            q_ref[0, h] = qh
            k_ref[0, h] = kh
            v_ref[0, h] = z[:, 2 * D_ATT + lo:2 * D_ATT + hi]
        else:
            qs.append(qh)
            ks.append(kh)
    if not head_major:
        q_ref[...] = jnp.concatenate(qs, axis=-1)
        k_ref[...] = jnp.concatenate(ks, axis=-1)
        v_ref[...] = z[:, 2 * D_ATT:]
    g_ref[...] = jnp.dot(xb, wg_ref[...], preferred_element_type=f32)
    gate_ref[...] = jnp.dot(xb, wgate_ref[...], preferred_element_type=f32)
    ba_ref[...] = jnp.dot(xb, wba_ref[...], preferred_element_type=f32)


def _in_proj(x2d, nm, wqkv, wg, wgate, wba, qg, kg, *, batch, seq, head_major, tm):
    n = x2d.shape[0]
    assert n == batch * seq and n % tm == 0
    full = lambda shape: pl.BlockSpec(shape, lambda i: (0,) * len(shape))
    rows = lambda width: pl.BlockSpec((tm, width), lambda i: (i, 0))
    if head_major:
        assert seq % tm == 0
        spb = seq // tm
        hm_shape = jax.ShapeDtypeStruct((batch, H_ATT, seq, HEAD_DIM), f32)
        hm_spec = pl.BlockSpec((1, H_ATT, tm, HEAD_DIM), lambda i: (i // spb, 0, i % spb, 0))
    else:
        hm_shape = jax.ShapeDtypeStruct((n, D_ATT), f32)
        hm_spec = rows(D_ATT)
    return pl.pallas_call(
        functools.partial(_in_proj_body, head_major=head_major),
        grid=(n // tm,),
        in_specs=[rows(D_MODEL), full((1, D_MODEL)), full(wqkv.shape), full(wg.shape),
                  full(wgate.shape), full(wba.shape), full((1, HEAD_DIM)), full((1, HEAD_DIM))],
        out_specs=[hm_spec, hm_spec, hm_spec, rows(3 * D_GDN), rows(D_GDN), rows(LANES)],
        out_shape=[hm_shape, hm_shape, hm_shape,
                   jax.ShapeDtypeStruct((n, 3 * D_GDN), f32),
                   jax.ShapeDtypeStruct((n, D_GDN), f32),
                   jax.ShapeDtypeStruct((n, LANES), f32)],
        compiler_params=pltpu.CompilerParams(dimension_semantics=("arbitrary",),
                                             vmem_limit_bytes=VMEM_LIMIT),
        name="in_proj",
    )(x2d, nm, wqkv, wg, wgate, wba, qg, kg)


def _moba_prompt_body(slope_ref, q_ref, k_ref, v_ref, o_ref, ka_ref, vb_ref, kmean_ref, *, seq):
    h = pl.program_id(1)
    i = pl.program_id(2)
    n_blk = seq // MOBA_BLOCK
    slope = slope_ref[pl.ds(h, 1), 0:1]

    @pl.when(i == 0)
    def _():
        k = k_ref[0, 0]
        row = _iota((seq, HEAD_DIM), 0)
        lane = _iota((seq, HEAD_DIM), 1)
        blk = jnp.right_shift(row, 8)
        jloc = jnp.bitwise_and(row, MOBA_BLOCK - 1)
        e = jnp.where(lane == blk, 1.0, 0.0)
        e = jnp.where(lane == n_blk, slope * jloc.astype(f32), e)
        e = jnp.where(lane == n_blk + 1, (slope * MOBA_BLOCK) * blk.astype(f32), e)
        e = jnp.where(lane == n_blk + 2, 1.0, e)
        ka_ref[...] = jnp.concatenate([k.astype(bf16), e.astype(bf16)], axis=-1)
        vb_ref[...] = v_ref[0, 0].astype(bf16)
        for n in range(n_blk):
            kmean_ref[n:n + 1, :] = jnp.sum(
                k_ref[0, 0, n * MOBA_BLOCK:(n + 1) * MOBA_BLOCK, :], axis=0, keepdims=True) * (1.0 / MOBA_BLOCK)

    q = q_ref[0, 0]
    gt = _dot(kmean_ref[...], q, NT, HI)
    nrow = _iota((n_blk, MOBA_BLOCK), 0)
    ivec = jnp.full((n_blk, MOBA_BLOCK), i, jnp.int32)
    rank = jnp.zeros((n_blk, MOBA_BLOCK), f32)
    for m in range(n_blk):
        gm = gt[m:m + 1, :]
        beats = (gm > gt) | ((gm == gt) & (nrow > m))
        rank = rank + jnp.where(beats & (ivec > m), 1.0, 0.0)
    sel = ((nrow < ivec) & (rank < MOBA_TOPK)) | (nrow == ivec)
    pen = jnp.where(sel, 0.0, NEG).T
    clane = _iota((MOBA_BLOCK, n_blk), 1)
    shift = -(slope * MOBA_BLOCK) * i.astype(f32)
    c = jnp.where(clane < 2, 1.0, jnp.where(clane == 2, shift, 0.0))
    qa = jnp.concatenate([q * ATT_SCALE, pen, c], axis=-1).astype(bf16)

    d0 = pl.multiple_of(i * MOBA_BLOCK, MOBA_BLOCK)
    s = _dot(qa, ka_ref[pl.ds(d0, MOBA_BLOCK), :], NT)
    causal = _iota((MOBA_BLOCK, MOBA_BLOCK), 1) <= _iota((MOBA_BLOCK, MOBA_BLOCK), 0)
    s = jnp.where(causal, s, NEG)
    m0 = jnp.max(s, axis=1, keepdims=True)
    p = jnp.exp(s - m0)
    l0 = jnp.sum(p, axis=1, keepdims=True)
    acc0 = _dot(p.astype(bf16), vb_ref[pl.ds(d0, MOBA_BLOCK), :])

    def body(n, carry):
        m_i, l_i, acc = carry
        r0 = pl.multiple_of(n * MOBA_BLOCK, MOBA_BLOCK)
        s = _dot(qa, ka_ref[pl.ds(r0, MOBA_BLOCK), :], NT)
        m_new = jnp.maximum(m_i, jnp.max(s, axis=1, keepdims=True))
        alpha = jnp.exp(m_i - m_new)
        p = jnp.exp(s - m_new)
        l_new = alpha * l_i + jnp.sum(p, axis=1, keepdims=True)
        acc = alpha * acc + _dot(p.astype(bf16), vb_ref[pl.ds(r0, MOBA_BLOCK), :])
        return m_new, l_new, acc

    _, l_f, acc_f = lax.fori_loop(0, i, body, (m0, l0, acc0))
    o_ref[0, 0] = acc_f / l_f


def _moba_prompt(q, k, v):
    batch, heads, seq, _ = q.shape
    assert seq % MOBA_BLOCK == 0 and seq // MOBA_BLOCK + 3 <= HEAD_DIM
    slopes = jnp.asarray(np.repeat(np.asarray(ALIBI_SLOPES, np.float32)[:, None], LANES, axis=1))
    tile = pl.BlockSpec((1, 1, MOBA_BLOCK, HEAD_DIM), lambda b, h, i: (b, h, i, 0))
    whole = pl.BlockSpec((1, 1, seq, HEAD_DIM), lambda b, h, i: (b, h, 0, 0))
    return pl.pallas_call(
        functools.partial(_moba_prompt_body, seq=seq),
        grid=(batch, heads, seq // MOBA_BLOCK),
        in_specs=[pl.BlockSpec((H_ATT, LANES), lambda b, h, i: (0, 0)), tile, whole, whole],
        out_specs=tile,
        out_shape=jax.ShapeDtypeStruct(q.shape, f32),
        scratch_shapes=[pltpu.VMEM((seq, 2 * HEAD_DIM), bf16), pltpu.VMEM((seq, HEAD_DIM), bf16),
                        pltpu.VMEM((seq // MOBA_BLOCK, HEAD_DIM), f32)],
        compiler_params=pltpu.CompilerParams(dimension_semantics=("arbitrary",) * 3,
                                             vmem_limit_bytes=VMEM_LIMIT),
        name="moba_prompt",
    )(slopes, q, k, v)


def _moba_sample_body(pt_ref, q_ref, k_ref, v_ref, *refs, past_len, t_new):
    del pt_ref
    n_pages = past_len // PAGE_SIZE
    kp_refs, vp_refs, o_ref = refs[:n_pages], refs[n_pages:2 * n_pages], refs[2 * n_pages]
    own = past_len // MOBA_BLOCK
    ppb = MOBA_BLOCK // PAGE_SIZE
    tq = _iota((t_new, past_len), 0)
    sk = _iota((t_new, past_len), 1)
    dist = (past_len + tq - sk).astype(f32)
    tq2 = _iota((t_new, t_new), 0)
    sk2 = _iota((t_new, t_new), 1)
    outs = []
    for h in range(H_ATT):
        lo, hi = h * HEAD_DIM, (h + 1) * HEAD_DIM
        slope = ALIBI_SLOPES[h]
        q = q_ref[:, lo:hi]
        qb = (q * ATT_SCALE).astype(bf16)
        kb, gate = [], []
        for n in range(own):
            pages = [kp_refs[n * ppb + j][0, h] for j in range(ppb)]
            kmean = sum(jnp.sum(pg, axis=0, keepdims=True) for pg in pages) * (1.0 / MOBA_BLOCK)
            gate.append(jnp.sum(q * kmean, axis=1, keepdims=True))
            kb.append(jnp.concatenate([pg.astype(bf16) for pg in pages], axis=0))
        s_blocks = []
        for n in range(own):
            rank = jnp.zeros((t_new, 1), f32)
            for m in range(own):
                if m == n:
                    continue
                beats = (gate[m] > gate[n]) | ((gate[m] == gate[n]) & (m < n))
                rank = rank + jnp.where(beats, 1.0, 0.0)
            pen = jnp.where(rank < min(MOBA_TOPK, own), 0.0, NEG)
            s_blocks.append(_dot(qb, kb[n], NT) + pen)
        s_past = jnp.concatenate(s_blocks, axis=1) - slope * dist
        s_own = _dot(qb, k_ref[:, lo:hi].astype(bf16), NT) - slope * (tq2 - sk2).astype(f32)
        s_own = jnp.where(sk2 <= tq2, s_own, NEG)
        m = jnp.maximum(jnp.max(s_past, axis=1, keepdims=True), jnp.max(s_own, axis=1, keepdims=True))
        p_past = jnp.exp(s_past - m)
        p_own = jnp.exp(s_own - m)
        denom = jnp.sum(p_past, axis=1, keepdims=True) + jnp.sum(p_own, axis=1, keepdims=True)
        o = _dot(p_own.astype(bf16), v_ref[:, lo:hi].astype(bf16))
        for n in range(own):
            vb = jnp.concatenate([vp_refs[n * ppb + j][0, h].astype(bf16) for j in range(ppb)], axis=0)
            o = o + _dot(p_past[:, n * MOBA_BLOCK:(n + 1) * MOBA_BLOCK].astype(bf16), vb)
        outs.append(o / denom)
    o_ref[...] = jnp.concatenate(outs, axis=-1)


def _moba_sample(q, k, v, cache_k, cache_v, page_table, *, t_new):
    n_seq, n_pages = page_table.shape
    past_len = n_pages * PAGE_SIZE
    assert past_len % MOBA_BLOCK == 0 and past_len // MOBA_BLOCK >= 1
    rows = pl.BlockSpec((t_new, D_ATT), lambda b, pt: (b, 0))

    def page_spec(p):
        return pl.BlockSpec((1, H_ATT, PAGE_SIZE, HEAD_DIM), lambda b, pt: (pt[b * n_pages + p], 0, 0, 0))

    grid_spec = pltpu.PrefetchScalarGridSpec(
        num_scalar_prefetch=1,
        grid=(n_seq,),
        in_specs=[rows, rows, rows] + [page_spec(p) for p in range(n_pages)] * 2,
        out_specs=rows,
    )
    return pl.pallas_call(
        functools.partial(_moba_sample_body, past_len=past_len, t_new=t_new),
        grid_spec=grid_spec,
        out_shape=jax.ShapeDtypeStruct(q.shape, f32),
        compiler_params=pltpu.CompilerParams(dimension_semantics=("arbitrary",),
                                             vmem_limit_bytes=VMEM_LIMIT),
        name="moba_sample",
    )(page_table.reshape(-1), q, k, v, *([cache_k] * n_pages), *([cache_v] * n_pages))


def _conv_silu(cur, prev8, w):
    t = cur.shape[0]
    row8 = _iota((8, cur.shape[1]), 0)
    y = None
    for i in range(CONV_W):
        d = CONV_W - 1 - i
        if d == 0:
            term = cur
        else:
            sh = pltpu.roll(cur, d, 0)
            top = jnp.where(row8 < d, pltpu.roll(prev8, d, 0), sh[0:8])
            term = top if t == 8 else jnp.concatenate([top, sh[8:]], axis=0)
        term = term * w[i:i + 1, :]
        y = term if y is None else y + term
    return _silu(y)


def _gdn_body(uq_ref, uk_ref, uv_ref, pq_ref, pk_ref, pv_ref, cq_ref, ck_ref, cv_ref,
              wq_ref, wk_ref, wv_ref, ba_ref, gate_ref, alog_ref, dtb_ref, gg_ref, m0_ref,
              o_ref, m_ref, *, bb, tb, c):
    hp = pl.program_id(1)
    t = pl.program_id(2)

    @pl.when(t == 0)
    def _():
        m_ref[...] = m0_ref[...]

    n_chunks = tb // c
    n_double = int(np.log2(c)) - 1
    assert 2 ** (n_double + 1) == c
    ri = _iota((tb, tb), 0)
    ci = _iota((tb, tb), 1)
    cum = jnp.where((ri >= ci) & ((ri // c) == (ci // c)), 1.0, 0.0)
    r = _iota((c, c), 0)
    cc = _iota((c, c), 1)
    eye = r == cc
    tri_incl = r >= cc
    tri_strict = r > cc
    lane = _iota((tb, LANES), 1)
    first = jnp.full((8, LANES), t, jnp.int32) == 0

    for s in range(bb):
        q = _conv_silu(uq_ref[s], jnp.where(first, cq_ref[s], pq_ref[s]), wq_ref[...])
        k = _conv_silu(uk_ref[s], jnp.where(first, ck_ref[s], pk_ref[s]), wk_ref[...])
        v = _conv_silu(uv_ref[s], jnp.where(first, cv_ref[s], pv_ref[s]), wv_ref[...])
        ba = ba_ref[s]
        beta_all = jax.nn.sigmoid(ba)
        xa = ba + dtb_ref[...]
        softplus = jnp.maximum(xa, 0.0) + jnp.log1p(jnp.exp(-jnp.abs(xa)))
        g_all = -jnp.exp(alog_ref[...]) * softplus
        gcum_all = _dot(cum, g_all, NN, HI)
        halves = []
        for j in range(2):
            hh = 2 * hp + j
            lo, hi = j * HEAD_DIM, (j + 1) * HEAD_DIM
            beta = jnp.sum(jnp.where(lane == hh, beta_all, 0.0), axis=1, keepdims=True)
            gcum = jnp.sum(jnp.where(lane == H_GDN + hh, gcum_all, 0.0), axis=1, keepdims=True)
            qh = q[:, lo:hi]
            kh = k[:, lo:hi]
            vh = v[:, lo:hi]
            qh = qh * lax.rsqrt(jnp.sum(qh * qh, axis=-1, keepdims=True) + L2_EPS) * (HEAD_DIM ** -0.5)
            kh = kh * lax.rsqrt(jnp.sum(kh * kh, axis=-1, keepdims=True) + L2_EPS)
            m = m_ref[s, j]
            outs = []
            for ch in range(n_chunks):
                sl = slice(ch * c, (ch + 1) * c)
                gc_col = gcum[sl]
                gc_row = jnp.sum(jnp.where(eye, gc_col, 0.0), axis=0, keepdims=True)
                decay = jnp.exp(jnp.where(tri_incl, gc_col - gc_row, NEG))
                kc, qc, vc, bc = kh[sl], qh[sl], vh[sl], beta[sl]
                a = bc * _dot(kc, kc, NT, HI) * jnp.where(tri_strict, decay, 0.0)
                x = jnp.where(eye, 1.0, 0.0) - a
                pw = a
                for _ in range(n_double):
                    pw = _dot(pw, pw, NN, HI)
                    x = x + _dot(x, pw, NN, HI)
                eg = jnp.exp(gc_col)
                u = _dot(x, bc * vc, NN, HI)
                wk = _dot(x, (bc * eg) * kc, NN, HI)
                aqk = _dot(qc, kc, NT, HI) * decay
                g_last = gc_col[c - 1:c, :]
                kd = kc * jnp.exp(g_last - gc_col)
                delta = u - _dot(wk, m, NN, HI)
                outs.append(_dot(qc * eg, m, NN, HI) + _dot(aqk, delta, NN, HI))
                m = jnp.exp(g_last) * m + _dot(kd, delta, TN, HI)
            m_ref[s, j] = m
            oh = outs[0] if n_chunks == 1 else jnp.concatenate(outs, axis=0)
            halves.append(_rms(oh))
        o = jnp.concatenate(halves, axis=-1) * gg_ref[...]
        o_ref[s] = o * _silu(gate_ref[s])


def _gdn(u, conv_pad, conv_w, ba, gate, alog_pad, dtb_pad, gg2, m0, *, bb, tb):
    batch, seq, _ = u.shape
    c = min(GDN_CHUNK, seq)
    assert batch % bb == 0 and seq % tb == 0 and tb % c == 0 and tb % 8 == 0
    n_hp = H_GDN // 2
    cpb = D_GDN // LANES
    cur = lambda g: pl.BlockSpec((bb, tb, LANES), lambda b, hp, t: (b, t, g * cpb + hp))
    prev = lambda g: pl.BlockSpec((bb, 8, LANES),
                                  lambda b, hp, t: (b, jnp.maximum(t * (tb // 8) - 1, 0), g * cpb + hp))
    cbuf = lambda g: pl.BlockSpec((bb, 8, LANES), lambda b, hp, t: (b, 0, g * cpb + hp))
    cw = lambda g: pl.BlockSpec((CONV_W, LANES), lambda b, hp, t: (0, g * cpb + hp))
    row = pl.BlockSpec((1, LANES), lambda b, hp, t: (0, 0))
    state = pl.BlockSpec((bb, 2, HEAD_DIM, HEAD_DIM), lambda b, hp, t: (b, hp, 0, 0))
    return pl.pallas_call(
        functools.partial(_gdn_body, bb=bb, tb=tb, c=c),
        grid=(batch // bb, n_hp, seq // tb),
        in_specs=[cur(0), cur(1), cur(2), prev(0), prev(1), prev(2), cbuf(0), cbuf(1), cbuf(2),
                  cw(0), cw(1), cw(2),
                  pl.BlockSpec((bb, tb, LANES), lambda b, hp, t: (b, t, 0)),
                  pl.BlockSpec((bb, tb, LANES), lambda b, hp, t: (b, t, hp)),
                  row, row, row, state],
        out_specs=[pl.BlockSpec((bb, tb, LANES), lambda b, hp, t: (b, t, hp)), state],
        out_shape=[jax.ShapeDtypeStruct((batch, seq, D_GDN), f32),
                   jax.ShapeDtypeStruct((batch, H_GDN, HEAD_DIM, HEAD_DIM), f32)],
        compiler_params=pltpu.CompilerParams(dimension_semantics=("arbitrary",) * 3,
                                             vmem_limit_bytes=VMEM_LIMIT),
        name="gdn",
    )(u, u, u, u, u, u, conv_pad, conv_pad, conv_pad, conv_w, conv_w, conv_w,
      ba, gate, alog_pad, dtb_pad, gg2, m0)


FF_CHUNK = 256


def _out_ffn_body(x_ref, oa_ref, og_ref, ag_ref, wo_ref, nf_ref, wg_ref, wu_ref, wd_ref, y_ref):
    oa = jnp.concatenate([oa_ref[0, h] for h in range(H_ATT)], axis=-1)
    oa = _rms(oa) * ag_ref[...]
    mix = jnp.concatenate([oa, og_ref[...]], axis=-1).astype(bf16)
    hid = x_ref[...] + jnp.dot(mix, wo_ref[...], preferred_element_type=f32)
    ub = (_rms(hid) * nf_ref[...]).astype(bf16)
    d_ff = wg_ref.shape[1]
    acc = hid
    for cidx in range(d_ff // FF_CHUNK):
        sl = slice(cidx * FF_CHUNK, (cidx + 1) * FF_CHUNK)
        a = jnp.dot(ub, wg_ref[:, sl], preferred_element_type=f32)
        b = jnp.dot(ub, wu_ref[:, sl], preferred_element_type=f32)
        acc = acc + jnp.dot((_silu(a) * b).astype(bf16), wd_ref[sl, :], preferred_element_type=f32)
    y_ref[...] = acc


def _out_ffn(x2d, o_att, o_gdn, ag, wo, nf, wg, wu, wd, *, tm):
    n = x2d.shape[0]
    batch, _, seq, _ = o_att.shape
    assert n == batch * seq and seq % tm == 0 and wg.shape[1] % FF_CHUNK == 0
    spb = seq // tm
    full = lambda shape: pl.BlockSpec(shape, lambda i: (0,) * len(shape))
    rows = lambda width: pl.BlockSpec((tm, width), lambda i: (i, 0))
    return pl.pallas_call(
        _out_ffn_body,
        grid=(n // tm,),
        in_specs=[rows(D_MODEL),
                  pl.BlockSpec((1, H_ATT, tm, HEAD_DIM), lambda i: (i // spb, 0, i % spb, 0)),
                  rows(D_GDN), full((1, D_ATT)), full(wo.shape), full((1, D_MODEL)),
                  full(wg.shape), full(wu.shape), full(wd.shape)],
        out_specs=rows(D_MODEL),
        out_shape=jax.ShapeDtypeStruct((n, D_MODEL), f32),
        compiler_params=pltpu.CompilerParams(dimension_semantics=("arbitrary",),
                                             vmem_limit_bytes=VMEM_LIMIT),
        name="out_ffn",
    )(x2d, o_att, o_gdn, ag, wo, nf, wg, wu, wd)


def kernel(x_prompt, x_sample, cache_k, cache_v, state_ssm, state_conv, page_table, norm_mix, w_in,
           q_norm_g, k_norm_g, attn_out_g, conv_w, a_log, dt_bias, gdn_out_g, w_out, norm_ffn,
           w_gate, w_up, w_down):
    depth = w_in.shape[0]
    assert depth == 1
    batch, seq, _ = x_prompt.shape
    n_seq, t_new, _ = x_sample.shape

    w = w_in[0]
    c0, c1, c2 = 3 * D_ATT, 3 * D_ATT + 3 * D_GDN, 3 * D_ATT + 4 * D_GDN
    wqkv = w[:, :c0].astype(bf16)
    wg = w[:, c0:c1].astype(bf16)
    wgate = w[:, c1:c2].astype(bf16)
    wba = jnp.pad(w[:, c2:], ((0, 0), (0, LANES - 2 * H_GDN))).astype(bf16)
    nm = norm_mix[0].reshape(1, D_MODEL)
    qg = q_norm_g[0].reshape(1, HEAD_DIM)
    kg = k_norm_g[0].reshape(1, HEAD_DIM)
    ag = attn_out_g[0].reshape(1, D_ATT)
    nf = norm_ffn[0].reshape(1, D_MODEL)
    cw = conv_w[0]
    pad_lo = lambda vec: jnp.pad(vec.astype(f32), (H_GDN, LANES - 2 * H_GDN)).reshape(1, LANES)
    alog_pad = pad_lo(a_log[0])
    dtb_pad = pad_lo(dt_bias[0])
    gg2 = jnp.tile(gdn_out_g[0].astype(f32), 2).reshape(1, LANES)
    wo = w_out[0].astype(bf16)
    wgt = w_gate[0].astype(bf16)
    wup = w_up[0].astype(bf16)
    wdn = w_down[0].astype(bf16)

    xp = x_prompt.reshape(batch * seq, D_MODEL)
    qp, kp, vp, gp, gatep, bap = _in_proj(xp, nm, wqkv, wg, wgate, wba, qg, kg,
                                          batch=batch, seq=seq, head_major=True, tm=256)
    oap = _moba_prompt(qp, kp, vp)
    conv0 = jnp.zeros((batch, 8, 3 * D_GDN), f32)
    ssm0 = jnp.zeros((batch, H_GDN, HEAD_DIM, HEAD_DIM), f32)
    ogp, ssm_p = _gdn(gp.reshape(batch, seq, 3 * D_GDN), conv0, cw, bap.reshape(batch, seq, LANES),
                      gatep.reshape(batch, seq, D_GDN), alog_pad, dtb_pad, gg2, ssm0, bb=1, tb=256)
    yp = _out_ffn(xp, oap, ogp.reshape(batch * seq, D_GDN), ag, wo, nf, wgt, wup, wdn, tm=256)

    ns = n_seq * t_new
    xs = x_sample.reshape(ns, D_MODEL)
    qs, ks, vs, gs, gates, bas = _in_proj(xs, nm, wqkv, wg, wgate, wba, qg, kg,
                                          batch=1, seq=ns, head_major=False, tm=256)
    oas = _moba_sample(qs, ks, vs, cache_k[0], cache_v[0], page_table, t_new=t_new)
    conv_s = jnp.pad(state_conv[0], ((0, 0), (8 - (CONV_W - 1), 0), (0, 0)))
    ogs, ssm_s = _gdn(gs.reshape(n_seq, t_new, 3 * D_GDN), conv_s, cw, bas.reshape(n_seq, t_new, LANES),
                      gates.reshape(n_seq, t_new, D_GDN), alog_pad, dtb_pad, gg2, state_ssm[0],
                      bb=8, tb=t_new)
    to_heads = lambda a: a.reshape(n_seq, t_new, H_ATT, HEAD_DIM).transpose(0, 2, 1, 3)
    oas_hm = oas.reshape(1, ns, H_ATT, HEAD_DIM).transpose(0, 2, 1, 3)
    ys = _out_ffn(xs, oas_hm, ogs.reshape(ns, D_GDN), ag, wo, nf, wgt, wup, wdn, tm=256)

    gp3 = gp.reshape(batch, seq, 3 * D_GDN)
    gs3 = gs.reshape(n_seq, t_new, 3 * D_GDN)
    return (yp.reshape(batch, seq, D_MODEL), ys.reshape(n_seq, t_new, D_MODEL),
            kp[None], vp[None], to_heads(ks)[None], to_heads(vs)[None],
            ssm_p[None], ssm_s[None],
            gp3[:, seq - (CONV_W - 1):][None], gs3[:, t_new - (CONV_W - 1):][None])
```

```python
import functools

import numpy as np
import jax
import jax.numpy as jnp
from jax import lax
from jax.experimental import pallas as pl
from jax.experimental.pallas import tpu as pltpu

f32 = jnp.float32
bf16 = jnp.bfloat16

D_MODEL = 1024
HEAD_DIM = 64
H_ATT = 8
H_GDN = 8
D_ATT = H_ATT * HEAD_DIM
D_GDN = H_GDN * HEAD_DIM
MOBA_BLOCK = 256
MOBA_SHIFT = 8
MOBA_TOPK = 3
GDN_CHUNK = 64
CONV_W = 4
PAGE_SIZE = 128
RMS_EPS = 1e-6
L2_EPS = 1e-6
ATT_SCALE = HEAD_DIM ** -0.5
NEG = -1e30
LANES = 128
VMEM_LIMIT = 56 * 1024 * 1024

HI = lax.Precision.HIGHEST
NN = (((1,), (0,)), ((), ()))
NT = (((1,), (1,)), ((), ()))
TN = (((0,), (0,)), ((), ()))

ALIBI_SLOPES = [2.0 ** (-8.0 * (i + 1) / H_ATT) for i in range(H_ATT)]


def _dot(a, b, dims=NN, precision=None):
    return lax.dot_general(a, b, dims, precision=precision, preferred_element_type=f32)


def _bdot(a, b, dims=NN):
    return lax.dot_general(a.astype(bf16), b.astype(bf16), dims, preferred_element_type=f32)


def _rms(x, axis=-1):
    return x * lax.rsqrt(jnp.mean(x * x, axis=axis, keepdims=True) + RMS_EPS)


def _silu(x):
    return x * jax.nn.sigmoid(x)


def _iota(shape, dim):
    return lax.broadcasted_iota(jnp.int32, shape, dim)


def _in_proj_prompt_body(x_ref, nm_ref, wq_ref, wkvt_ref, wg_ref, wgate_ref, wba_ref, qg_ref, kgc_ref,
                         q_ref, kt_ref, vt_ref, g_ref, gate_ref, ba_ref):
    x = x_ref[...]
    xb = (_rms(x) * nm_ref[...]).astype(bf16)
    zq = jnp.dot(xb, wq_ref[...], preferred_element_type=f32)
    zkv = _dot(wkvt_ref[...], xb, NT)
    for h in range(H_ATT):
        lo, hi = h * HEAD_DIM, (h + 1) * HEAD_DIM
        q_ref[0, h] = _rms(zq[:, lo:hi]) * qg_ref[...]
        kt_ref[0, h] = _rms(zkv[lo:hi, :], axis=0) * kgc_ref[...]
        vt_ref[0, h] = zkv[D_ATT + lo:D_ATT + hi, :]
    g_ref[...] = jnp.dot(xb, wg_ref[...], preferred_element_type=f32)
    gate_ref[...] = jnp.dot(xb, wgate_ref[...], preferred_element_type=f32)
    ba_ref[...] = jnp.dot(xb, wba_ref[...], preferred_element_type=f32)


def _in_proj_prompt(x2d, nm, wq, wkvt, wg, wgate, wba, qg, kgc, *, batch, seq, tm):
    n = x2d.shape[0]
    assert n == batch * seq and seq % tm == 0
    spb = seq // tm
    full = lambda shape: pl.BlockSpec(shape, lambda i: (0,) * len(shape))
    rows = lambda width: pl.BlockSpec((tm, width), lambda i: (i, 0))
    q_spec = pl.BlockSpec((1, H_ATT, tm, HEAD_DIM), lambda i: (i // spb, 0, i % spb, 0))
    t_spec = pl.BlockSpec((1, H_ATT, HEAD_DIM, tm), lambda i: (i // spb, 0, 0, i % spb))
    t_shape = jax.ShapeDtypeStruct((batch, H_ATT, HEAD_DIM, seq), f32)
    return pl.pallas_call(
        _in_proj_prompt_body,
        grid=(n // tm,),
        in_specs=[rows(D_MODEL), full((1, D_MODEL)), full(wq.shape), full(wkvt.shape), full(wg.shape),
                  full(wgate.shape), full(wba.shape), full((1, HEAD_DIM)), full((HEAD_DIM, 1))],
        out_specs=[q_spec, t_spec, t_spec, rows(3 * D_GDN), rows(D_GDN), rows(LANES)],
        out_shape=[jax.ShapeDtypeStruct((batch, H_ATT, seq, HEAD_DIM), f32), t_shape, t_shape,
                   jax.ShapeDtypeStruct((n, 3 * D_GDN), f32),
                   jax.ShapeDtypeStruct((n, D_GDN), f32),
                   jax.ShapeDtypeStruct((n, LANES), f32)],
        compiler_params=pltpu.CompilerParams(dimension_semantics=("arbitrary",),
                                             vmem_limit_bytes=VMEM_LIMIT),
        name="in_proj_prompt",
    )(x2d, nm, wq, wkvt, wg, wgate, wba, qg, kgc)


def _in_proj_sample_body(x_ref, nm_ref, wqkv_ref, wg_ref, wgate_ref, wba_ref, qg_ref, kg_ref,
                         q_ref, k_ref, v_ref, g_ref, gate_ref, ba_ref):
    x = x_ref[...]
    xb = (_rms(x) * nm_ref[...]).astype(bf16)
    z = jnp.dot(xb, wqkv_ref[...], preferred_element_type=f32)
    qs, ks = [], []
    for h in range(H_ATT):
        lo, hi = h * HEAD_DIM, (h + 1) * HEAD_DIM
        qs.append(_rms(z[:, lo:hi]) * qg_ref[...])
        ks.append(_rms(z[:, D_ATT + lo:D_ATT + hi]) * kg_ref[...])
    q_ref[...] = jnp.concatenate(qs, axis=-1)
    k_ref[...] = jnp.concatenate(ks, axis=-1)
    v_ref[...] = z[:, 2 * D_ATT:]
    g_ref[...] = jnp.dot(xb, wg_ref[...], preferred_element_type=f32)
    gate_ref[...] = jnp.dot(xb, wgate_ref[...], preferred_element_type=f32)
    ba_ref[...] = jnp.dot(xb, wba_ref[...], preferred_element_type=f32)


def _in_proj_sample(x2d, nm, wqkv, wg, wgate, wba, qg, kg, *, tm):
    n = x2d.shape[0]
    assert n % tm == 0
    full = lambda shape: pl.BlockSpec(shape, lambda i: (0,) * len(shape))
    rows = lambda width: pl.BlockSpec((tm, width), lambda i: (i, 0))
    tok = jax.ShapeDtypeStruct((n, D_ATT), f32)
    return pl.pallas_call(
        _in_proj_sample_body,
        grid=(n // tm,),
        in_specs=[rows(D_MODEL), full((1, D_MODEL)), full(wqkv.shape), full(wg.shape),
                  full(wgate.shape), full(wba.shape), full((1, HEAD_DIM)), full((1, HEAD_DIM))],
        out_specs=[rows(D_ATT), rows(D_ATT), rows(D_ATT), rows(3 * D_GDN), rows(D_GDN), rows(LANES)],
        out_shape=[tok, tok, tok,
                   jax.ShapeDtypeStruct((n, 3 * D_GDN), f32),
                   jax.ShapeDtypeStruct((n, D_GDN), f32),
                   jax.ShapeDtypeStruct((n, LANES), f32)],
        compiler_params=pltpu.CompilerParams(dimension_semantics=("arbitrary",),
                                             vmem_limit_bytes=VMEM_LIMIT),
        name="in_proj_sample",
    )(x2d, nm, wqkv, wg, wgate, wba, qg, kg)


def _fold_lanes(s, op):
    out = s[:, :LANES]
    for c0 in range(LANES, s.shape[1], LANES):
        out = op(out, s[:, c0:c0 + LANES])
    return out


def _moba_prompt_body(slope_ref, q_ref, kt_ref, vt_ref, o_ref, ka_ref, va_ref, kmean_ref, *, seq):
    h = pl.program_id(1)
    i = pl.program_id(2)
    n_blk = seq // MOBA_BLOCK
    pair = 2 * MOBA_BLOCK
    slope = slope_ref[pl.ds(h, 1), 0:1]

    @pl.when(i == 0)
    def _():
        kt = kt_ref[0, 0]
        row = _iota((HEAD_DIM, seq), 0)
        pos = _iota((HEAD_DIM, seq), 1)
        blk = jnp.right_shift(pos, MOBA_SHIFT)
        jloc = jnp.bitwise_and(pos, MOBA_BLOCK - 1)
        e = jnp.where(row == blk, 1.0, 0.0)
        e = jnp.where(row == n_blk, slope * jloc.astype(f32), e)
        e = jnp.where(row == n_blk + 1, (slope * MOBA_BLOCK) * blk.astype(f32), e)
        e = jnp.where(row == n_blk + 2, 1.0, e)
        ka_ref[0:HEAD_DIM, :] = kt.astype(bf16)
        ka_ref[HEAD_DIM:, :] = e.astype(bf16)
        va_ref[0:HEAD_DIM, :] = vt_ref[0, 0].astype(bf16)
        va_ref[HEAD_DIM:, :] = jnp.where(row == 0, 1.0, 0.0).astype(bf16)
        cols = [jnp.sum(kt[:, n * MOBA_BLOCK:(n + 1) * MOBA_BLOCK], axis=1, keepdims=True)
                for n in range(n_blk)]
        kmean_ref[...] = (jnp.concatenate(cols, axis=1) * (1.0 / MOBA_BLOCK)).T

    q = q_ref[0, 0]
    gt = _dot(kmean_ref[...], q, NT, HI)
    nrow = _iota((n_blk, MOBA_BLOCK), 0)
    ivec = jnp.full((n_blk, MOBA_BLOCK), i, jnp.int32)
    rank = jnp.zeros((n_blk, MOBA_BLOCK), f32)
    for m in range(n_blk):
        gm = gt[m:m + 1, :]
        beats = (gm > gt) | ((gm == gt) & (nrow > m))
        rank = rank + jnp.where(beats & (ivec > m), 1.0, 0.0)
    sel = (nrow < ivec) & (rank < MOBA_TOPK)
    pen = jnp.where(sel, 0.0, NEG).T
    clane = _iota((MOBA_BLOCK, n_blk), 1)
    shift = -(slope * MOBA_BLOCK) * i.astype(f32)
    c = jnp.where(clane < 2, 1.0, jnp.where(clane == 2, shift, 0.0))
    qs = q * ATT_SCALE
    qa = jnp.concatenate([qs, pen, c], axis=-1).astype(bf16)
    qo = jnp.concatenate([qs, jnp.zeros_like(pen), c], axis=-1).astype(bf16)

    d0 = pl.multiple_of(i * MOBA_BLOCK, MOBA_BLOCK)
    s_own = _dot(qo, ka_ref[:, pl.ds(d0, MOBA_BLOCK)])
    causal = _iota((MOBA_BLOCK, MOBA_BLOCK), 1) <= _iota((MOBA_BLOCK, MOBA_BLOCK), 0)
    s_own = jnp.where(causal, s_own, NEG)
    n_pair = (i + 1) // 2

    def pass1(j, mx):
        r0 = pl.multiple_of(j * pair, pair)
        return jnp.maximum(mx, _fold_lanes(_dot(qa, ka_ref[:, pl.ds(r0, pair)]), jnp.maximum))

    mx = lax.fori_loop(0, n_pair, pass1, _fold_lanes(s_own, jnp.maximum))
    m_row = jnp.max(mx, axis=1, keepdims=True)
    m_b = jnp.broadcast_to(m_row, (MOBA_BLOCK, LANES))

    def probs(s):
        return jnp.concatenate([jnp.exp(s[:, c0:c0 + LANES] - m_b) for c0 in range(0, s.shape[1], LANES)],
                               axis=1).astype(bf16)

    def pass2(j, acc):
        r0 = pl.multiple_of(j * pair, pair)
        p = probs(_dot(qa, ka_ref[:, pl.ds(r0, pair)]))
        return acc + _dot(p, va_ref[:, pl.ds(r0, pair)], NT)

    acc = lax.fori_loop(0, n_pair, pass2, _dot(probs(s_own), va_ref[:, pl.ds(d0, MOBA_BLOCK)], NT))
    o_ref[0, 0] = acc[:, :HEAD_DIM] / acc[:, HEAD_DIM:HEAD_DIM + 1]


def _moba_prompt(q, kt, vt):
    batch, heads, seq, _ = q.shape
    assert seq % (2 * MOBA_BLOCK) == 0 and seq // MOBA_BLOCK + 3 <= HEAD_DIM and MOBA_BLOCK == 1 << MOBA_SHIFT
    slopes = jnp.asarray(np.repeat(np.asarray(ALIBI_SLOPES, np.float32)[:, None], LANES, axis=1))
    tile = pl.BlockSpec((1, 1, MOBA_BLOCK, HEAD_DIM), lambda b, h, i: (b, h, i, 0))
    whole = pl.BlockSpec((1, 1, HEAD_DIM, seq), lambda b, h, i: (b, h, 0, 0))
    return pl.pallas_call(
        functools.partial(_moba_prompt_body, seq=seq),
        grid=(batch, heads, seq // MOBA_BLOCK),
        in_specs=[pl.BlockSpec((H_ATT, LANES), lambda b, h, i: (0, 0)), tile, whole, whole],
        out_specs=tile,
        out_shape=jax.ShapeDtypeStruct(q.shape, f32),
        scratch_shapes=[pltpu.VMEM((2 * HEAD_DIM, seq), bf16), pltpu.VMEM((2 * HEAD_DIM, seq), bf16),
                        pltpu.VMEM((seq // MOBA_BLOCK, HEAD_DIM), f32)],
        compiler_params=pltpu.CompilerParams(dimension_semantics=("arbitrary",) * 3,
                                             vmem_limit_bytes=VMEM_LIMIT),
        name="moba_prompt",
    )(slopes, q, kt, vt)


def _moba_sample_body(pt_ref, q_ref, k_ref, v_ref, *refs, past_len, t_new):
    del pt_ref
    n_pages = past_len // PAGE_SIZE
    kp_refs, vp_refs, o_ref = refs[:n_pages], refs[n_pages:2 * n_pages], refs[2 * n_pages]
    own = past_len // MOBA_BLOCK
    ppb = MOBA_BLOCK // PAGE_SIZE
    n_sel = min(MOBA_TOPK, own)
    tq = _iota((t_new, past_len), 0)
    sk = _iota((t_new, past_len), 1)
    dist = (past_len + tq - sk).astype(f32)
    tq2 = _iota((t_new, t_new), 0)
    sk2 = _iota((t_new, t_new), 1)
    outs = []
    for h in range(H_ATT):
        lo, hi = h * HEAD_DIM, (h + 1) * HEAD_DIM
        slope = ALIBI_SLOPES[h]
        q = q_ref[:, lo:hi]
        qb = (q * ATT_SCALE).astype(bf16)
        kb, cols = [], []
        for n in range(own):
            pages = [kp_refs[n * ppb + j][0, h] for j in range(ppb)]
            cols.append(sum(jnp.sum(pg, axis=1, keepdims=True) for pg in pages))
            kb.append(jnp.concatenate([pg.astype(bf16) for pg in pages], axis=1))
        kmean_t = jnp.concatenate(cols, axis=1) * (1.0 / MOBA_BLOCK)
        gate = _dot(q, kmean_t, NN, HI)
        s_blocks = []
        for n in range(own):
            gn = gate[:, n:n + 1]
            rank = jnp.zeros((t_new, 1), f32)
            for m in range(own):
                if m == n:
                    continue
                gm = gate[:, m:m + 1]
                beats = (gm > gn) | ((gm == gn) & (m < n))
                rank = rank + jnp.where(beats, 1.0, 0.0)
            pen = jnp.where(rank < n_sel, 0.0, NEG)
            s_blocks.append(_dot(qb, kb[n]) + pen)
        s_past = jnp.concatenate(s_blocks, axis=1) - slope * dist
        s_own = _dot(qb, k_ref[:, lo:hi].astype(bf16), NT) - slope * (tq2 - sk2).astype(f32)
        s_own = jnp.where(sk2 <= tq2, s_own, NEG)
        m = jnp.maximum(jnp.max(s_past, axis=1, keepdims=True), jnp.max(s_own, axis=1, keepdims=True))
        p_past = jnp.exp(s_past - m)
        p_own = jnp.exp(s_own - m)
        denom = jnp.sum(p_past, axis=1, keepdims=True) + jnp.sum(p_own, axis=1, keepdims=True)
        o = _dot(p_own.astype(bf16), v_ref[:, lo:hi].astype(bf16))
        for n in range(own):
            vb = jnp.concatenate([vp_refs[n * ppb + j][0, h].astype(bf16) for j in range(ppb)], axis=1)
            o = o + _dot(p_past[:, n * MOBA_BLOCK:(n + 1) * MOBA_BLOCK].astype(bf16), vb, NT)
        outs.append(o / denom)
    o_ref[...] = jnp.concatenate(outs, axis=-1)


def _moba_sample(q, k, v, cache_kt, cache_vt, page_table, *, t_new):
    n_seq, n_pages = page_table.shape
    past_len = n_pages * PAGE_SIZE
    assert past_len % MOBA_BLOCK == 0 and past_len // MOBA_BLOCK >= 1
    rows = pl.BlockSpec((t_new, D_ATT), lambda b, pt: (b, 0))

    def page_spec(p):
        return pl.BlockSpec((1, H_ATT, HEAD_DIM, PAGE_SIZE), lambda b, pt: (pt[b * n_pages + p], 0, 0, 0))

    grid_spec = pltpu.PrefetchScalarGridSpec(
        num_scalar_prefetch=1,
        grid=(n_seq,),
        in_specs=[rows, rows, rows] + [page_spec(p) for p in range(n_pages)] * 2,
        out_specs=rows,
    )
    return pl.pallas_call(
        functools.partial(_moba_sample_body, past_len=past_len, t_new=t_new),
        grid_spec=grid_spec,
        out_shape=jax.ShapeDtypeStruct(q.shape, f32),
        compiler_params=pltpu.CompilerParams(dimension_semantics=("arbitrary",),
                                             vmem_limit_bytes=VMEM_LIMIT),
        name="moba_sample",
    )(page_table.reshape(-1), q, k, v, *([cache_kt] * n_pages), *([cache_vt] * n_pages))


def _conv_silu(cur, prev8, w):
    t = cur.shape[0]
    row8 = _iota((8, cur.shape[1]), 0)
    y = None
    for i in range(CONV_W):
        d = CONV_W - 1 - i
        if d == 0:
            term = cur
        else:
            sh = pltpu.roll(cur, d, 0)
            top = jnp.where(row8 < d, pltpu.roll(prev8, d, 0), sh[0:8])
            term = top if t == 8 else jnp.concatenate([top, sh[8:]], axis=0)
        term = term * w[i:i + 1, :]
        y = term if y is None else y + term
    return _silu(y)


def _gdn_body(uq_ref, uk_ref, uv_ref, pq_ref, pk_ref, pv_ref, cq_ref, ck_ref, cv_ref,
              wq_ref, wk_ref, wv_ref, ba_ref, gate_ref, alog_ref, dtb_ref, gg_ref, m0_ref,
              o_ref, m_ref, *, bb, tb, c):
    t = pl.program_id(1)

    @pl.when(t == 0)
    def _():
        m_ref[...] = m0_ref[...]

    n_chunks = tb // c
    lc = int(np.log2(c))
    assert 1 << lc == c
    ri = _iota((tb, tb), 0)
    ci = _iota((tb, tb), 1)
    same = jnp.right_shift(ri, lc) == jnp.right_shift(ci, lc)
    eye = ri == ci
    keep = same & (ri >= ci)
    cum = jnp.where(keep, 1.0, 0.0)
    tot = jnp.where(same, 1.0, 0.0)
    ident = jnp.where(eye, 1.0, 0.0)
    first = jnp.full((8, D_GDN), t, jnp.int32) == 0

    for s in range(bb):
        q = _conv_silu(uq_ref[s], jnp.where(first, cq_ref[s], pq_ref[s]), wq_ref[...])
        k = _conv_silu(uk_ref[s], jnp.where(first, ck_ref[s], pk_ref[s]), wk_ref[...])
        v = _conv_silu(uv_ref[s], jnp.where(first, cv_ref[s], pv_ref[s]), wv_ref[...])
        ba = ba_ref[s]
        beta_all = jax.nn.sigmoid(ba)
        xa = ba + dtb_ref[...]
        softplus = jnp.maximum(xa, 0.0) + jnp.log1p(jnp.exp(-jnp.abs(xa)))
        g_all = -jnp.exp(alog_ref[...]) * softplus
        gcum_all = _dot(cum, g_all, NN, HI)
        gtot_all = _dot(tot, g_all, NN, HI)
        gcum_t = gcum_all.T if tb % LANES == 0 else None
        outs = []
        for h in range(H_GDN):
            lo, hi = h * HEAD_DIM, (h + 1) * HEAD_DIM
            beta = beta_all[:, h:h + 1]
            gcum = gcum_all[:, H_GDN + h:H_GDN + h + 1]
            gtot = gtot_all[:, H_GDN + h:H_GDN + h + 1]
            if gcum_t is None:
                grow = jnp.sum(jnp.where(eye, gcum, 0.0), axis=0, keepdims=True)
            else:
                grow = gcum_t[H_GDN + h:H_GDN + h + 1, :]
            qh, kh, vh = q[:, lo:hi], k[:, lo:hi], v[:, lo:hi]
            qh = qh * lax.rsqrt(jnp.sum(qh * qh, axis=-1, keepdims=True) + L2_EPS) * (HEAD_DIM ** -0.5)
            kh = kh * lax.rsqrt(jnp.sum(kh * kh, axis=-1, keepdims=True) + L2_EPS)
            decay = jnp.exp(jnp.where(keep, gcum - grow, NEG))
            a = beta * _bdot(kh, kh, NT) * jnp.where(eye, 0.0, decay)
            x = ident - a
            pw = a
            for _ in range(lc - 1):
                pw = _bdot(pw, pw)
                x = x + _bdot(x, pw)
            eg = jnp.exp(gcum)
            sol = _bdot(x, jnp.concatenate([beta * vh, (beta * eg) * kh], axis=-1))
            u, wk = sol[:, :HEAD_DIM], sol[:, HEAD_DIM:]
            aqk = _bdot(qh, kh, NT) * decay
            qg = qh * eg
            kd = kh * jnp.exp(gtot - gcum)
            gdec = jnp.exp(gtot)
            m = m_ref[s, h]
            deltas, oqs = [], []
            for ch in range(n_chunks):
                sl = slice(ch * c, (ch + 1) * c)
                both = _bdot(jnp.concatenate([wk[sl], qg[sl]], axis=0), m)
                delta = u[sl] - both[:c]
                deltas.append(delta)
                oqs.append(both[c:])
                m = gdec[ch * c:ch * c + 1, :] * m + _dot(kd[sl], delta, TN)
            m_ref[s, h] = m
            if n_chunks == 1:
                delta_all, oq = deltas[0], oqs[0]
            else:
                delta_all, oq = jnp.concatenate(deltas, axis=0), jnp.concatenate(oqs, axis=0)
            outs.append(_rms(oq + _bdot(aqk, delta_all)))
        o = jnp.concatenate(outs, axis=-1) * gg_ref[...]
        o_ref[s] = o * _silu(gate_ref[s])


def _gdn(u, conv_pad, conv_w, ba, gate, alog_pad, dtb_pad, gg8, m0, *, bb, tb):
    batch, seq, _ = u.shape
    c = min(GDN_CHUNK, seq)
    assert batch % bb == 0 and seq % tb == 0 and tb % c == 0 and tb % 8 == 0
    cur = lambda g: pl.BlockSpec((bb, tb, D_GDN), lambda b, t: (b, t, g))
    prev = lambda g: pl.BlockSpec((bb, 8, D_GDN), lambda b, t: (b, jnp.maximum(t * (tb // 8) - 1, 0), g))
    cbuf = lambda g: pl.BlockSpec((bb, 8, D_GDN), lambda b, t: (b, 0, g))
    cw = lambda g: pl.BlockSpec((CONV_W, D_GDN), lambda b, t: (0, g))
    row = lambda width: pl.BlockSpec((1, width), lambda b, t: (0, 0))
    state = pl.BlockSpec((bb, H_GDN, HEAD_DIM, HEAD_DIM), lambda b, t: (b, 0, 0, 0))
    return pl.pallas_call(
        functools.partial(_gdn_body, bb=bb, tb=tb, c=c),
        grid=(batch // bb, seq // tb),
        in_specs=[cur(0), cur(1), cur(2), prev(0), prev(1), prev(2), cbuf(0), cbuf(1), cbuf(2),
                  cw(0), cw(1), cw(2),
                  pl.BlockSpec((bb, tb, LANES), lambda b, t: (b, t, 0)),
                  pl.BlockSpec((bb, tb, D_GDN), lambda b, t: (b, t, 0)),
                  row(LANES), row(LANES), row(D_GDN), state],
        out_specs=[pl.BlockSpec((bb, tb, D_GDN), lambda b, t: (b, t, 0)), state],
        out_shape=[jax.ShapeDtypeStruct((batch, seq, D_GDN), f32),
                   jax.ShapeDtypeStruct((batch, H_GDN, HEAD_DIM, HEAD_DIM), f32)],
        compiler_params=pltpu.CompilerParams(dimension_semantics=("arbitrary",) * 2,
                                             vmem_limit_bytes=VMEM_LIMIT),
        name="gdn",
    )(u, u, u, u, u, u, conv_pad, conv_pad, conv_pad, conv_w, conv_w, conv_w,
      ba, gate, alog_pad, dtb_pad, gg8, m0)


FF_CHUNK = 256


def _out_ffn_body(x_ref, oa_ref, og_ref, ag_ref, wo_ref, nf_ref, wg_ref, wu_ref, wd_ref, y_ref):
    oa = jnp.concatenate([oa_ref[0, h] for h in range(H_ATT)], axis=-1)
    oa = _rms(oa) * ag_ref[...]
    mix = jnp.concatenate([oa, og_ref[...]], axis=-1).astype(bf16)
    hid = x_ref[...] + jnp.dot(mix, wo_ref[...], preferred_element_type=f32)
    ub = (_rms(hid) * nf_ref[...]).astype(bf16)
    d_ff = wg_ref.shape[1]
    acc = hid
    for cidx in range(d_ff // FF_CHUNK):
        sl = slice(cidx * FF_CHUNK, (cidx + 1) * FF_CHUNK)
        a = jnp.dot(ub, wg_ref[:, sl], preferred_element_type=f32)
        b = jnp.dot(ub, wu_ref[:, sl], preferred_element_type=f32)
        acc = acc + jnp.dot((_silu(a) * b).astype(bf16), wd_ref[sl, :], preferred_element_type=f32)
    y_ref[...] = acc


def _out_ffn(x2d, o_att, o_gdn, ag, wo, nf, wg, wu, wd, *, tm):
    n = x2d.shape[0]
    batch, _, seq, _ = o_att.shape
    assert n == batch * seq and seq % tm == 0 and wg.shape[1] % FF_CHUNK == 0
    spb = seq // tm
    full = lambda shape: pl.BlockSpec(shape, lambda i: (0,) * len(shape))
    rows = lambda width: pl.BlockSpec((tm, width), lambda i: (i, 0))
    return pl.pallas_call(
        _out_ffn_body,
        grid=(n // tm,),
        in_specs=[rows(D_MODEL),
                  pl.BlockSpec((1, H_ATT, tm, HEAD_DIM), lambda i: (i // spb, 0, i % spb, 0)),
                  rows(D_GDN), full((1, D_ATT)), full(wo.shape), full((1, D_MODEL)),
                  full(wg.shape), full(wu.shape), full(wd.shape)],
        out_specs=rows(D_MODEL),
        out_shape=jax.ShapeDtypeStruct((n, D_MODEL), f32),
        compiler_params=pltpu.CompilerParams(dimension_semantics=("arbitrary",),
                                             vmem_limit_bytes=VMEM_LIMIT),
        name="out_ffn",
    )(x2d, o_att, o_gdn, ag, wo, nf, wg, wu, wd)


def kernel(x_prompt, x_sample, cache_k, cache_v, state_ssm, state_conv, page_table, norm_mix, w_in,
           q_norm_g, k_norm_g, attn_out_g, conv_w, a_log, dt_bias, gdn_out_g, w_out, norm_ffn,
           w_gate, w_up, w_down):
    depth = w_in.shape[0]
    assert depth == 1
    batch, seq, _ = x_prompt.shape
    n_seq, t_new, _ = x_sample.shape

    w = w_in[0]
    c0, c1, c2 = 3 * D_ATT, 3 * D_ATT + 3 * D_GDN, 3 * D_ATT + 4 * D_GDN
    wqkv = w[:, :c0].astype(bf16)
    wq = w[:, :D_ATT].astype(bf16)
    wkvt = w.T[D_ATT:c0].astype(bf16)
    wg = w[:, c0:c1].astype(bf16)
    wgate = w[:, c1:c2].astype(bf16)
    wba = jnp.pad(w[:, c2:], ((0, 0), (0, LANES - 2 * H_GDN))).astype(bf16)
    nm = norm_mix[0].reshape(1, D_MODEL)
    qg = q_norm_g[0].reshape(1, HEAD_DIM)
    kg = k_norm_g[0].reshape(1, HEAD_DIM)
    kgc = k_norm_g[0].reshape(HEAD_DIM, 1)
    ag = attn_out_g[0].reshape(1, D_ATT)
    nf = norm_ffn[0].reshape(1, D_MODEL)
    cw = conv_w[0]
    pad_lo = lambda vec: jnp.pad(vec.astype(f32), (H_GDN, LANES - 2 * H_GDN)).reshape(1, LANES)
    alog_pad = pad_lo(a_log[0])
    dtb_pad = pad_lo(dt_bias[0])
    gg8 = jnp.tile(gdn_out_g[0].astype(f32), H_GDN).reshape(1, D_GDN)
    wo = w_out[0].astype(bf16)
    wgt = w_gate[0].astype(bf16)
    wup = w_up[0].astype(bf16)
    wdn = w_down[0].astype(bf16)

    xp = x_prompt.reshape(batch * seq, D_MODEL)
    qp, ktp, vtp, gp, gatep, bap = _in_proj_prompt(xp, nm, wq, wkvt, wg, wgate, wba, qg, kgc,
                                                   batch=batch, seq=seq, tm=256)
    oap = _moba_prompt(qp, ktp, vtp)
    conv0 = jnp.zeros((batch, 8, 3 * D_GDN), f32)
    ssm0 = jnp.zeros((batch, H_GDN, HEAD_DIM, HEAD_DIM), f32)
    ogp, ssm_p = _gdn(gp.reshape(batch, seq, 3 * D_GDN), conv0, cw, bap.reshape(batch, seq, LANES),
                      gatep.reshape(batch, seq, D_GDN), alog_pad, dtb_pad, gg8, ssm0, bb=1, tb=256)
    yp = _out_ffn(xp, oap, ogp.reshape(batch * seq, D_GDN), ag, wo, nf, wgt, wup, wdn, tm=256)

    ns = n_seq * t_new
    xs = x_sample.reshape(ns, D_MODEL)
    qs, ks, vs, gs, gates, bas = _in_proj_sample(xs, nm, wqkv, wg, wgate, wba, qg, kg, tm=256)
    cache_kt = jnp.swapaxes(cache_k[0], -1, -2)
    cache_vt = jnp.swapaxes(cache_v[0], -1, -2)
    oas = _moba_sample(qs, ks, vs, cache_kt, cache_vt, page_table, t_new=t_new)
    conv_s = jnp.pad(state_conv[0], ((0, 0), (8 - (CONV_W - 1), 0), (0, 0)))
    ogs, ssm_s = _gdn(gs.reshape(n_seq, t_new, 3 * D_GDN), conv_s, cw, bas.reshape(n_seq, t_new, LANES),
                      gates.reshape(n_seq, t_new, D_GDN), alog_pad, dtb_pad, gg8, state_ssm[0],
                      bb=2, tb=t_new)
    to_heads = lambda a: a.reshape(n_seq, t_new, H_ATT, HEAD_DIM).transpose(0, 2, 1, 3)
    oas_hm = oas.reshape(1, ns, H_ATT, HEAD_DIM).transpose(0, 2, 1, 3)
    ys = _out_ffn(xs, oas_hm, ogs.reshape(ns, D_GDN), ag, wo, nf, wgt, wup, wdn, tm=256)

    gp3 = gp.reshape(batch, seq, 3 * D_GDN)
    gs3 = gs.reshape(n_seq, t_new, 3 * D_GDN)
    return (yp.reshape(batch, seq, D_MODEL), ys.reshape(n_seq, t_new, D_MODEL),
            jnp.swapaxes(ktp, -1, -2)[None], jnp.swapaxes(vtp, -1, -2)[None],
            to_heads(ks)[None], to_heads(vs)[None],
            ssm_p[None], ssm_s[None],
            gp3[:, seq - (CONV_W - 1):][None], gs3[:, t_new - (CONV_W - 1):][None])
```

```python
import functools

import numpy as np
import jax
import jax.numpy as jnp
from jax import lax
from jax.experimental import pallas as pl
from jax.experimental.pallas import tpu as pltpu

f32 = jnp.float32
bf16 = jnp.bfloat16

D_MODEL = 1024
HEAD_DIM = 64
H_ATT = 8
H_GDN = 8
D_ATT = H_ATT * HEAD_DIM
D_GDN = H_GDN * HEAD_DIM
MOBA_BLOCK = 256
MOBA_SHIFT = 8
MOBA_TOPK = 3
GDN_CHUNK = 64
CONV_W = 4
PAGE_SIZE = 128
RMS_EPS = 1e-6
L2_EPS = 1e-6
ATT_SCALE = HEAD_DIM ** -0.5
NEG = -1e30
LANES = 128
VMEM_LIMIT = 56 * 1024 * 1024

HI = lax.Precision.HIGHEST
NN = (((1,), (0,)), ((), ()))
NT = (((1,), (1,)), ((), ()))
TN = (((0,), (0,)), ((), ()))

ALIBI_SLOPES = [2.0 ** (-8.0 * (i + 1) / H_ATT) for i in range(H_ATT)]


def _dot(a, b, dims=NN, precision=None):
    return lax.dot_general(a, b, dims, precision=precision, preferred_element_type=f32)


def _bdot(a, b, dims=NN):
    return lax.dot_general(a.astype(bf16), b.astype(bf16), dims, preferred_element_type=f32)


def _rms(x, axis=-1):
    return x * lax.rsqrt(jnp.mean(x * x, axis=axis, keepdims=True) + RMS_EPS)


def _silu(x):
    return x * jax.nn.sigmoid(x)


def _iota(shape, dim):
    return lax.broadcasted_iota(jnp.int32, shape, dim)


def _in_proj_prompt_body(x_ref, nm_ref, wq_ref, wkvt_ref, wg_ref, wgate_ref, wba_ref, qg_ref, kgc_ref,
                         q_ref, kt_ref, vt_ref, g_ref, gate_ref, ba_ref):
    x = x_ref[...]
    xb = (_rms(x) * nm_ref[...]).astype(bf16)
    zq = jnp.dot(xb, wq_ref[...], preferred_element_type=f32)
    zkv = _dot(wkvt_ref[...], xb, NT)
    for h in range(H_ATT):
        lo, hi = h * HEAD_DIM, (h + 1) * HEAD_DIM
        q_ref[0, h] = _rms(zq[:, lo:hi]) * qg_ref[...]
        kt_ref[0, h] = _rms(zkv[lo:hi, :], axis=0) * kgc_ref[...]
        vt_ref[0, h] = zkv[D_ATT + lo:D_ATT + hi, :]
    g_ref[...] = jnp.dot(xb, wg_ref[...], preferred_element_type=f32)
    gate_ref[...] = jnp.dot(xb, wgate_ref[...], preferred_element_type=f32)
    ba_ref[...] = jnp.dot(xb, wba_ref[...], preferred_element_type=f32)


def _in_proj_prompt(x2d, nm, wq, wkvt, wg, wgate, wba, qg, kgc, *, batch, seq, tm):
    n = x2d.shape[0]
    assert n == batch * seq and seq % tm == 0
    spb = seq // tm
    full = lambda shape: pl.BlockSpec(shape, lambda i: (0,) * len(shape))
    rows = lambda width: pl.BlockSpec((tm, width), lambda i: (i, 0))
    q_spec = pl.BlockSpec((1, H_ATT, tm, HEAD_DIM), lambda i: (i // spb, 0, i % spb, 0))
    t_spec = pl.BlockSpec((1, H_ATT, HEAD_DIM, tm), lambda i: (i // spb, 0, 0, i % spb))
    t_shape = jax.ShapeDtypeStruct((batch, H_ATT, HEAD_DIM, seq), f32)
    return pl.pallas_call(
        _in_proj_prompt_body,
        grid=(n // tm,),
        in_specs=[rows(D_MODEL), full((1, D_MODEL)), full(wq.shape), full(wkvt.shape), full(wg.shape),
                  full(wgate.shape), full(wba.shape), full((1, HEAD_DIM)), full((HEAD_DIM, 1))],
        out_specs=[q_spec, t_spec, t_spec, rows(3 * D_GDN), rows(D_GDN), rows(LANES)],
        out_shape=[jax.ShapeDtypeStruct((batch, H_ATT, seq, HEAD_DIM), f32), t_shape, t_shape,
                   jax.ShapeDtypeStruct((n, 3 * D_GDN), f32),
                   jax.ShapeDtypeStruct((n, D_GDN), f32),
                   jax.ShapeDtypeStruct((n, LANES), f32)],
        compiler_params=pltpu.CompilerParams(dimension_semantics=("arbitrary",),
                                             vmem_limit_bytes=VMEM_LIMIT),
        name="in_proj_prompt",
    )(x2d, nm, wq, wkvt, wg, wgate, wba, qg, kgc)


def _in_proj_sample_body(x_ref, nm_ref, wqkv_ref, wg_ref, wgate_ref, wba_ref, qg_ref, kg_ref,
                         q_ref, k_ref, v_ref, g_ref, gate_ref, ba_ref):
    x = x_ref[...]
    xb = (_rms(x) * nm_ref[...]).astype(bf16)
    z = jnp.dot(xb, wqkv_ref[...], preferred_element_type=f32)
    qs, ks = [], []
    for h in range(H_ATT):
        lo, hi = h * HEAD_DIM, (h + 1) * HEAD_DIM
        qs.append(_rms(z[:, lo:hi]) * qg_ref[...])
        ks.append(_rms(z[:, D_ATT + lo:D_ATT + hi]) * kg_ref[...])
    q_ref[...] = jnp.concatenate(qs, axis=-1)
    k_ref[...] = jnp.concatenate(ks, axis=-1)
    v_ref[...] = z[:, 2 * D_ATT:]
    g_ref[...] = jnp.dot(xb, wg_ref[...], preferred_element_type=f32)
    gate_ref[...] = jnp.dot(xb, wgate_ref[...], preferred_element_type=f32)
    ba_ref[...] = jnp.dot(xb, wba_ref[...], preferred_element_type=f32)


def _in_proj_sample(x2d, nm, wqkv, wg, wgate, wba, qg, kg, *, tm):
    n = x2d.shape[0]
    assert n % tm == 0
    full = lambda shape: pl.BlockSpec(shape, lambda i: (0,) * len(shape))
    rows = lambda width: pl.BlockSpec((tm, width), lambda i: (i, 0))
    tok = jax.ShapeDtypeStruct((n, D_ATT), f32)
    return pl.pallas_call(
        _in_proj_sample_body,
        grid=(n // tm,),
        in_specs=[rows(D_MODEL), full((1, D_MODEL)), full(wqkv.shape), full(wg.shape),
                  full(wgate.shape), full(wba.shape), full((1, HEAD_DIM)), full((1, HEAD_DIM))],
        out_specs=[rows(D_ATT), rows(D_ATT), rows(D_ATT), rows(3 * D_GDN), rows(D_GDN), rows(LANES)],
        out_shape=[tok, tok, tok,
                   jax.ShapeDtypeStruct((n, 3 * D_GDN), f32),
                   jax.ShapeDtypeStruct((n, D_GDN), f32),
                   jax.ShapeDtypeStruct((n, LANES), f32)],
        compiler_params=pltpu.CompilerParams(dimension_semantics=("arbitrary",),
                                             vmem_limit_bytes=VMEM_LIMIT),
        name="in_proj_sample",
    )(x2d, nm, wqkv, wg, wgate, wba, qg, kg)


def _fold_lanes(s, op):
    out = s[:, :LANES]
    for c0 in range(LANES, s.shape[1], LANES):
        out = op(out, s[:, c0:c0 + LANES])
    return out


MOBA_HEADS = 2
MOBA_TRIP = 4


def _moba_prompt_body(slope_ref, q_ref, kt_ref, vt_ref, o_ref, ka_ref, va_ref, kmean_ref, *, seq):
    hg = pl.program_id(1)
    i = pl.program_id(2)
    n_blk = seq // MOBA_BLOCK
    trip = MOBA_TRIP * MOBA_BLOCK
    slopes = [slope_ref[pl.ds(hg * MOBA_HEADS + j, 1), 0:1] for j in range(MOBA_HEADS)]

    @pl.when(i == 0)
    def _():
        row = _iota((HEAD_DIM, seq), 0)
        pos = _iota((HEAD_DIM, seq), 1)
        blk = jnp.right_shift(pos, MOBA_SHIFT)
        jloc = jnp.bitwise_and(pos, MOBA_BLOCK - 1)
        onehot = jnp.where(row == blk, 1.0, 0.0)
        ones_row = jnp.where(row == 0, 1.0, 0.0).astype(bf16)
        for j in range(MOBA_HEADS):
            kt = kt_ref[0, j]
            e = jnp.where(row == n_blk, slopes[j] * jloc.astype(f32), onehot)
            e = jnp.where(row == n_blk + 1, (slopes[j] * MOBA_BLOCK) * blk.astype(f32), e)
            e = jnp.where(row == n_blk + 2, 1.0, e)
            ka_ref[j, 0:HEAD_DIM, :] = kt.astype(bf16)
            ka_ref[j, HEAD_DIM:, :] = e.astype(bf16)
            va_ref[j, 0:HEAD_DIM, :] = vt_ref[0, j].astype(bf16)
            va_ref[j, HEAD_DIM:, :] = ones_row
            cols = [jnp.sum(kt[:, n * MOBA_BLOCK:(n + 1) * MOBA_BLOCK], axis=1, keepdims=True)
                    for n in range(n_blk)]
            kmean_ref[j] = (jnp.concatenate(cols, axis=1) * (1.0 / MOBA_BLOCK)).T

    nrow = _iota((n_blk, MOBA_BLOCK), 0)
    row8 = _iota((8, MOBA_BLOCK), 0)
    ivec = jnp.full((n_blk, MOBA_BLOCK), i, jnp.int32)
    eye_q = (_iota((MOBA_BLOCK, MOBA_BLOCK), 0) == _iota((MOBA_BLOCK, MOBA_BLOCK), 1)).astype(bf16)
    causal = _iota((MOBA_BLOCK, MOBA_BLOCK), 1) <= _iota((MOBA_BLOCK, MOBA_BLOCK), 0)
    d0 = pl.multiple_of(i * MOBA_BLOCK, MOBA_BLOCK)
    qa, s_own = [], []
    for j in range(MOBA_HEADS):
        q = q_ref[0, j]
        gt = _dot(kmean_ref[j], q, NT, HI)
        gt = jnp.where(nrow < ivec, gt, -jnp.inf)
        ranks = []
        for g0 in range(0, n_blk, 8):
            grp = gt[g0:g0 + 8, :]
            rank = jnp.zeros((8, MOBA_BLOCK), f32)
            for m in range(n_blk):
                gm = gt[m:m + 1, :]
                if m < g0:
                    beats = gm >= grp
                elif m >= g0 + 8:
                    beats = gm > grp
                else:
                    beats = (gm > grp) | ((gm == grp) & (row8 > m - g0))
                rank = rank + jnp.where(beats, 1.0, 0.0)
            ranks.append(rank)
        sel = (nrow < ivec) & (jnp.concatenate(ranks, axis=0) < MOBA_TOPK)
        shift = -(slopes[j] * MOBA_BLOCK) * i.astype(f32)
        c_t = jnp.where(nrow < 2, 1.0, jnp.where(nrow == 2, shift, 0.0))
        sel_pen = jnp.concatenate([jnp.where(sel, 0.0, NEG), c_t], axis=0).astype(bf16)
        own_pen = jnp.concatenate([jnp.zeros((n_blk, MOBA_BLOCK), f32), c_t], axis=0).astype(bf16)
        qs = q * ATT_SCALE
        qa.append(jnp.concatenate([qs, _dot(eye_q, sel_pen, NT)], axis=-1).astype(bf16))
        qo = jnp.concatenate([qs, _dot(eye_q, own_pen, NT)], axis=-1).astype(bf16)
        s_own.append(jnp.where(causal, _dot(qo, ka_ref[j, :, pl.ds(d0, MOBA_BLOCK)]), NEG))
    n_trip = (i + MOBA_TRIP - 1) // MOBA_TRIP

    def pass1(t, mx):
        r0 = pl.multiple_of(t * trip, trip)
        return tuple(jnp.maximum(mx[j], _fold_lanes(_dot(qa[j], ka_ref[j, :, pl.ds(r0, trip)]), jnp.maximum))
                     for j in range(MOBA_HEADS))

    mx = lax.fori_loop(0, n_trip, pass1, tuple(_fold_lanes(s, jnp.maximum) for s in s_own))
    m_b = [jnp.broadcast_to(jnp.max(mx[j], axis=1, keepdims=True), (MOBA_BLOCK, LANES))
           for j in range(MOBA_HEADS)]

    def probs(s, j):
        return jnp.concatenate([jnp.exp(s[:, c0:c0 + LANES] - m_b[j]) for c0 in range(0, s.shape[1], LANES)],
                               axis=1).astype(bf16)

    def pass2(t, acc):
        r0 = pl.multiple_of(t * trip, trip)
        return tuple(acc[j] + _dot(probs(_dot(qa[j], ka_ref[j, :, pl.ds(r0, trip)]), j),
                                   va_ref[j, :, pl.ds(r0, trip)], NT)
                     for j in range(MOBA_HEADS))

    acc = lax.fori_loop(0, n_trip, pass2,
                        tuple(_dot(probs(s_own[j], j), va_ref[j, :, pl.ds(d0, MOBA_BLOCK)], NT)
                              for j in range(MOBA_HEADS)))
    for j in range(MOBA_HEADS):
        o_ref[0, j] = acc[j][:, :HEAD_DIM] / acc[j][:, HEAD_DIM:HEAD_DIM + 1]


def _moba_prompt(q, kt, vt):
    batch, heads, seq, _ = q.shape
    n_blk = seq // MOBA_BLOCK
    assert n_blk % MOBA_TRIP == 0 and 2 * n_blk == HEAD_DIM and MOBA_BLOCK == 1 << MOBA_SHIFT
    assert heads % MOBA_HEADS == 0
    slopes = jnp.asarray(np.repeat(np.asarray(ALIBI_SLOPES, np.float32)[:, None], LANES, axis=1))
    tile = pl.BlockSpec((1, MOBA_HEADS, MOBA_BLOCK, HEAD_DIM), lambda b, h, i: (b, h, i, 0))
    whole = pl.BlockSpec((1, MOBA_HEADS, HEAD_DIM, seq), lambda b, h, i: (b, h, 0, 0))
    return pl.pallas_call(
        functools.partial(_moba_prompt_body, seq=seq),
        grid=(batch, heads // MOBA_HEADS, n_blk),
        in_specs=[pl.BlockSpec((H_ATT, LANES), lambda b, h, i: (0, 0)), tile, whole, whole],
        out_specs=tile,
        out_shape=jax.ShapeDtypeStruct(q.shape, f32),
        scratch_shapes=[pltpu.VMEM((MOBA_HEADS, 2 * HEAD_DIM, seq), bf16),
                        pltpu.VMEM((MOBA_HEADS, 2 * HEAD_DIM, seq), bf16),
                        pltpu.VMEM((MOBA_HEADS, n_blk, HEAD_DIM), f32)],
        compiler_params=pltpu.CompilerParams(dimension_semantics=("arbitrary",) * 3,
                                             vmem_limit_bytes=VMEM_LIMIT),
        name="moba_prompt",
    )(slopes, q, kt, vt)


def _moba_sample_body(pt_ref, q_ref, k_ref, v_ref, *refs, past_len, t_new):
    del pt_ref
    n_pages = past_len // PAGE_SIZE
    kp_refs, vp_refs, o_ref = refs[:n_pages], refs[n_pages:2 * n_pages], refs[2 * n_pages]
    own = past_len // MOBA_BLOCK
    ppb = MOBA_BLOCK // PAGE_SIZE
    n_sel = min(MOBA_TOPK, own)
    tq = _iota((t_new, past_len), 0)
    sk = _iota((t_new, past_len), 1)
    dist = (past_len + tq - sk).astype(f32)
    tq2 = _iota((t_new, t_new), 0)
    sk2 = _iota((t_new, t_new), 1)
    outs = []
    for h in range(H_ATT):
        lo, hi = h * HEAD_DIM, (h + 1) * HEAD_DIM
        slope = ALIBI_SLOPES[h]
        q = q_ref[:, lo:hi]
        qb = (q * ATT_SCALE).astype(bf16)
        kb, cols = [], []
        for n in range(own):
            pages = [kp_refs[n * ppb + j][0, h] for j in range(ppb)]
            cols.append(sum(jnp.sum(pg, axis=1, keepdims=True) for pg in pages))
            kb.append(jnp.concatenate([pg.astype(bf16) for pg in pages], axis=1))
        kmean_t = jnp.concatenate(cols, axis=1) * (1.0 / MOBA_BLOCK)
        gate = _dot(q, kmean_t, NN, HI)
        s_blocks = []
        for n in range(own):
            gn = gate[:, n:n + 1]
            rank = jnp.zeros((t_new, 1), f32)
            for m in range(own):
                if m == n:
                    continue
                gm = gate[:, m:m + 1]
                beats = (gm > gn) | ((gm == gn) & (m < n))
                rank = rank + jnp.where(beats, 1.0, 0.0)
            pen = jnp.where(rank < n_sel, 0.0, NEG)
            s_blocks.append(_dot(qb, kb[n]) + pen)
        s_past = jnp.concatenate(s_blocks, axis=1) - slope * dist
        s_own = _dot(qb, k_ref[:, lo:hi].astype(bf16), NT) - slope * (tq2 - sk2).astype(f32)
        s_own = jnp.where(sk2 <= tq2, s_own, NEG)
        m = jnp.maximum(jnp.max(s_past, axis=1, keepdims=True), jnp.max(s_own, axis=1, keepdims=True))
        p_past = jnp.exp(s_past - m)
        p_own = jnp.exp(s_own - m)
        denom = jnp.sum(p_past, axis=1, keepdims=True) + jnp.sum(p_own, axis=1, keepdims=True)
        o = _dot(p_own.astype(bf16), v_ref[:, lo:hi].astype(bf16))
        for n in range(own):
            vb = jnp.concatenate([vp_refs[n * ppb + j][0, h].astype(bf16) for j in range(ppb)], axis=1)
            o = o + _dot(p_past[:, n * MOBA_BLOCK:(n + 1) * MOBA_BLOCK].astype(bf16), vb, NT)
        outs.append(o / denom)
    o_ref[...] = jnp.concatenate(outs, axis=-1)


def _moba_sample(q, k, v, cache_kt, cache_vt, page_table, *, t_new):
    n_seq, n_pages = page_table.shape
    past_len = n_pages * PAGE_SIZE
    assert past_len % MOBA_BLOCK == 0 and past_len // MOBA_BLOCK >= 1
    rows = pl.BlockSpec((t_new, D_ATT), lambda b, pt: (b, 0))

    def page_spec(p):
        return pl.BlockSpec((1, H_ATT, HEAD_DIM, PAGE_SIZE), lambda b, pt: (pt[b * n_pages + p], 0, 0, 0))

    grid_spec = pltpu.PrefetchScalarGridSpec(
        num_scalar_prefetch=1,
        grid=(n_seq,),
        in_specs=[rows, rows, rows] + [page_spec(p) for p in range(n_pages)] * 2,
        out_specs=rows,
    )
    return pl.pallas_call(
        functools.partial(_moba_sample_body, past_len=past_len, t_new=t_new),
        grid_spec=grid_spec,
        out_shape=jax.ShapeDtypeStruct(q.shape, f32),
        compiler_params=pltpu.CompilerParams(dimension_semantics=("arbitrary",),
                                             vmem_limit_bytes=VMEM_LIMIT),
        name="moba_sample",
    )(page_table.reshape(-1), q, k, v, *([cache_kt] * n_pages), *([cache_vt] * n_pages))


def _conv_silu(cur, prev8, w):
    t = cur.shape[0]
    row8 = _iota((8, cur.shape[1]), 0)
    y = None
    for i in range(CONV_W):
        d = CONV_W - 1 - i
        if d == 0:
            term = cur
        else:
            sh = pltpu.roll(cur, d, 0)
            top = jnp.where(row8 < d, pltpu.roll(prev8, d, 0), sh[0:8])
            term = top if t == 8 else jnp.concatenate([top, sh[8:]], axis=0)
        term = term * w[i:i + 1, :]
        y = term if y is None else y + term
    return _silu(y)


def _gdn_body(uq_ref, uk_ref, uv_ref, pq_ref, pk_ref, pv_ref, cq_ref, ck_ref, cv_ref,
              wq_ref, wk_ref, wv_ref, ba_ref, gate_ref, alog_ref, dtb_ref, gg_ref, m0_ref,
              o_ref, m_ref, *, bb, tb, c):
    t = pl.program_id(1)

    @pl.when(t == 0)
    def _():
        m_ref[...] = m0_ref[...]

    n_chunks = tb // c
    lc = int(np.log2(c))
    assert 1 << lc == c
    ri = _iota((tb, tb), 0)
    ci = _iota((tb, tb), 1)
    same = jnp.right_shift(ri, lc) == jnp.right_shift(ci, lc)
    eye = ri == ci
    keep = same & (ri >= ci)
    cum = jnp.where(keep, 1.0, 0.0)
    tot = jnp.where(same, 1.0, 0.0)
    ident = jnp.where(eye, 1.0, 0.0)
    first = jnp.full((8, D_GDN), t, jnp.int32) == 0

    for s in range(bb):
        q = _conv_silu(uq_ref[s], jnp.where(first, cq_ref[s], pq_ref[s]), wq_ref[...])
        k = _conv_silu(uk_ref[s], jnp.where(first, ck_ref[s], pk_ref[s]), wk_ref[...])
        v = _conv_silu(uv_ref[s], jnp.where(first, cv_ref[s], pv_ref[s]), wv_ref[...])
        ba = ba_ref[s]
        beta_all = jax.nn.sigmoid(ba)
        xa = ba + dtb_ref[...]
        softplus = jnp.maximum(xa, 0.0) + jnp.log1p(jnp.exp(-jnp.abs(xa)))
        g_all = -jnp.exp(alog_ref[...]) * softplus
        gcum_all = _dot(cum, g_all, NN, HI)
        gtot_all = _dot(tot, g_all, NN, HI)
        gcum_t = gcum_all.T if tb % LANES == 0 else None
        hs = range(H_GDN)
        beta = [beta_all[:, h:h + 1] for h in hs]
        gcum = [gcum_all[:, H_GDN + h:H_GDN + h + 1] for h in hs]
        gtot = [gtot_all[:, H_GDN + h:H_GDN + h + 1] for h in hs]
        if gcum_t is None:
            grow = [jnp.sum(jnp.where(eye, gcum[h], 0.0), axis=0, keepdims=True) for h in hs]
        else:
            grow = [gcum_t[H_GDN + h:H_GDN + h + 1, :] for h in hs]
        l2n = lambda z: z * lax.rsqrt(jnp.sum(z * z, axis=-1, keepdims=True) + L2_EPS)
        qh = [l2n(q[:, h * HEAD_DIM:(h + 1) * HEAD_DIM]) * (HEAD_DIM ** -0.5) for h in hs]
        kh = [l2n(k[:, h * HEAD_DIM:(h + 1) * HEAD_DIM]) for h in hs]
        vh = [v[:, h * HEAD_DIM:(h + 1) * HEAD_DIM] for h in hs]
        decay = [jnp.exp(jnp.where(keep, gcum[h] - grow[h], NEG)) for h in hs]
        pw = [beta[h] * _bdot(kh[h], kh[h], NT) * jnp.where(eye, 0.0, decay[h]) for h in hs]
        x = [ident - pw[h] for h in hs]
        for _ in range(lc - 1):
            pw = [_bdot(pw[h], pw[h]) for h in hs]
            x = [x[h] + _bdot(x[h], pw[h]) for h in hs]
        eg = [jnp.exp(gcum[h]) for h in hs]
        sol = [_bdot(x[h], jnp.concatenate([beta[h] * vh[h], (beta[h] * eg[h]) * kh[h]], axis=-1))
               for h in hs]
        aqk = [_bdot(qh[h], kh[h], NT) * decay[h] for h in hs]
        qg = [qh[h] * eg[h] for h in hs]
        kd = [kh[h] * jnp.exp(gtot[h] - gcum[h]) for h in hs]
        gdec = [jnp.exp(gtot[h]) for h in hs]
        m = [m_ref[s, h] for h in hs]
        deltas = [[] for _ in hs]
        oqs = [[] for _ in hs]
        for ch in range(n_chunks):
            sl = slice(ch * c, (ch + 1) * c)
            both = [_bdot(jnp.concatenate([sol[h][sl, HEAD_DIM:], qg[h][sl]], axis=0), m[h]) for h in hs]
            for h in hs:
                deltas[h].append(sol[h][sl, :HEAD_DIM] - both[h][:c])
                oqs[h].append(both[h][c:])
            m = [gdec[h][ch * c:ch * c + 1, :] * m[h] + _dot(kd[h][sl], deltas[h][ch], TN) for h in hs]
        outs = []
        for h in hs:
            m_ref[s, h] = m[h]
            if n_chunks == 1:
                delta_all, oq = deltas[h][0], oqs[h][0]
            else:
                delta_all, oq = jnp.concatenate(deltas[h], axis=0), jnp.concatenate(oqs[h], axis=0)
            outs.append(_rms(oq + _bdot(aqk[h], delta_all)))
        o = jnp.concatenate(outs, axis=-1) * gg_ref[...]
        o_ref[s] = o * _silu(gate_ref[s])


def _gdn(u, conv_pad, conv_w, ba, gate, alog_pad, dtb_pad, gg8, m0, *, bb, tb):
    batch, seq, _ = u.shape
    c = min(GDN_CHUNK, seq)
    assert batch % bb == 0 and seq % tb == 0 and tb % c == 0 and tb % 8 == 0
    cur = lambda g: pl.BlockSpec((bb, tb, D_GDN), lambda b, t: (b, t, g))
    prev = lambda g: pl.BlockSpec((bb, 8, D_GDN), lambda b, t: (b, jnp.maximum(t * (tb // 8) - 1, 0), g))
    cbuf = lambda g: pl.BlockSpec((bb, 8, D_GDN), lambda b, t: (b, 0, g))
    cw = lambda g: pl.BlockSpec((CONV_W, D_GDN), lambda b, t: (0, g))
    row = lambda width: pl.BlockSpec((1, width), lambda b, t: (0, 0))
    state = pl.BlockSpec((bb, H_GDN, HEAD_DIM, HEAD_DIM), lambda b, t: (b, 0, 0, 0))
    return pl.pallas_call(
        functools.partial(_gdn_body, bb=bb, tb=tb, c=c),
        grid=(batch // bb, seq // tb),
        in_specs=[cur(0), cur(1), cur(2), prev(0), prev(1), prev(2), cbuf(0), cbuf(1), cbuf(2),
                  cw(0), cw(1), cw(2),
                  pl.BlockSpec((bb, tb, LANES), lambda b, t: (b, t, 0)),
                  pl.BlockSpec((bb, tb, D_GDN), lambda b, t: (b, t, 0)),
                  row(LANES), row(LANES), row(D_GDN), state],
        out_specs=[pl.BlockSpec((bb, tb, D_GDN), lambda b, t: (b, t, 0)), state],
        out_shape=[jax.ShapeDtypeStruct((batch, seq, D_GDN), f32),
                   jax.ShapeDtypeStruct((batch, H_GDN, HEAD_DIM, HEAD_DIM), f32)],
        compiler_params=pltpu.CompilerParams(dimension_semantics=("arbitrary",) * 2,
                                             vmem_limit_bytes=VMEM_LIMIT),
        name="gdn",
    )(u, u, u, u, u, u, conv_pad, conv_pad, conv_pad, conv_w, conv_w, conv_w,
      ba, gate, alog_pad, dtb_pad, gg8, m0)


FF_CHUNK = 256


def _out_ffn_body(x_ref, oa_ref, og_ref, ag_ref, wo_ref, nf_ref, wg_ref, wu_ref, wd_ref, y_ref):
    oa = jnp.concatenate([oa_ref[0, h] for h in range(H_ATT)], axis=-1)
    oa = _rms(oa) * ag_ref[...]
    mix = jnp.concatenate([oa, og_ref[...]], axis=-1).astype(bf16)
    hid = x_ref[...] + jnp.dot(mix, wo_ref[...], preferred_element_type=f32)
    ub = (_rms(hid) * nf_ref[...]).astype(bf16)
    d_ff = wg_ref.shape[1]
    acc = hid
    for cidx in range(d_ff // FF_CHUNK):
        sl = slice(cidx * FF_CHUNK, (cidx + 1) * FF_CHUNK)
        a = jnp.dot(ub, wg_ref[:, sl], preferred_element_type=f32)
        b = jnp.dot(ub, wu_ref[:, sl], preferred_element_type=f32)
        acc = acc + jnp.dot((_silu(a) * b).astype(bf16), wd_ref[sl, :], preferred_element_type=f32)
    y_ref[...] = acc


def _out_ffn(x2d, o_att, o_gdn, ag, wo, nf, wg, wu, wd, *, tm):
    n = x2d.shape[0]
    batch, _, seq, _ = o_att.shape
    assert n == batch * seq and seq % tm == 0 and wg.shape[1] % FF_CHUNK == 0
    spb = seq // tm
    full = lambda shape: pl.BlockSpec(shape, lambda i: (0,) * len(shape))
    rows = lambda width: pl.BlockSpec((tm, width), lambda i: (i, 0))
    return pl.pallas_call(
        _out_ffn_body,
        grid=(n // tm,),
        in_specs=[rows(D_MODEL),
                  pl.BlockSpec((1, H_ATT, tm, HEAD_DIM), lambda i: (i // spb, 0, i % spb, 0)),
                  rows(D_GDN), full((1, D_ATT)), full(wo.shape), full((1, D_MODEL)),
                  full(wg.shape), full(wu.shape), full(wd.shape)],
        out_specs=rows(D_MODEL),
        out_shape=jax.ShapeDtypeStruct((n, D_MODEL), f32),
        compiler_params=pltpu.CompilerParams(dimension_semantics=("arbitrary",),
                                             vmem_limit_bytes=VMEM_LIMIT),
        name="out_ffn",
    )(x2d, o_att, o_gdn, ag, wo, nf, wg, wu, wd)


def kernel(x_prompt, x_sample, cache_k, cache_v, state_ssm, state_conv, page_table, norm_mix, w_in,
           q_norm_g, k_norm_g, attn_out_g, conv_w, a_log, dt_bias, gdn_out_g, w_out, norm_ffn,
           w_gate, w_up, w_down):
    depth = w_in.shape[0]
    assert depth == 1
    batch, seq, _ = x_prompt.shape
    n_seq, t_new, _ = x_sample.shape

    w = w_in[0]
    c0, c1, c2 = 3 * D_ATT, 3 * D_ATT + 3 * D_GDN, 3 * D_ATT + 4 * D_GDN
    wqkv = w[:, :c0].astype(bf16)
    wq = w[:, :D_ATT].astype(bf16)
    wkvt = w.T[D_ATT:c0].astype(bf16)
    wg = w[:, c0:c1].astype(bf16)
    wgate = w[:, c1:c2].astype(bf16)
    wba = jnp.pad(w[:, c2:], ((0, 0), (0, LANES - 2 * H_GDN))).astype(bf16)
    nm = norm_mix[0].reshape(1, D_MODEL)
    qg = q_norm_g[0].reshape(1, HEAD_DIM)
    kg = k_norm_g[0].reshape(1, HEAD_DIM)
    kgc = k_norm_g[0].reshape(HEAD_DIM, 1)
    ag = attn_out_g[0].reshape(1, D_ATT)
    nf = norm_ffn[0].reshape(1, D_MODEL)
    cw = conv_w[0]
    pad_lo = lambda vec: jnp.pad(vec.astype(f32), (H_GDN, LANES - 2 * H_GDN)).reshape(1, LANES)
    alog_pad = pad_lo(a_log[0])
    dtb_pad = pad_lo(dt_bias[0])
    gg8 = jnp.tile(gdn_out_g[0].astype(f32), H_GDN).reshape(1, D_GDN)
    wo = w_out[0].astype(bf16)
    wgt = w_gate[0].astype(bf16)
    wup = w_up[0].astype(bf16)
    wdn = w_down[0].astype(bf16)

    xp = x_prompt.reshape(batch * seq, D_MODEL)
    qp, ktp, vtp, gp, gatep, bap = _in_proj_prompt(xp, nm, wq, wkvt, wg, wgate, wba, qg, kgc,
                                                   batch=batch, seq=seq, tm=256)
    oap = _moba_prompt(qp, ktp, vtp)
    conv0 = jnp.zeros((batch, 8, 3 * D_GDN), f32)
    ssm0 = jnp.zeros((batch, H_GDN, HEAD_DIM, HEAD_DIM), f32)
    ogp, ssm_p = _gdn(gp.reshape(batch, seq, 3 * D_GDN), conv0, cw, bap.reshape(batch, seq, LANES),
                      gatep.reshape(batch, seq, D_GDN), alog_pad, dtb_pad, gg8, ssm0, bb=1, tb=256)
    yp = _out_ffn(xp, oap, ogp.reshape(batch * seq, D_GDN), ag, wo, nf, wgt, wup, wdn, tm=256)

    ns = n_seq * t_new
    xs = x_sample.reshape(ns, D_MODEL)
    qs, ks, vs, gs, gates, bas = _in_proj_sample(xs, nm, wqkv, wg, wgate, wba, qg, kg, tm=256)
    cache_kt = jnp.swapaxes(cache_k[0], -1, -2)
    cache_vt = jnp.swapaxes(cache_v[0], -1, -2)
    oas = _moba_sample(qs, ks, vs, cache_kt, cache_vt, page_table, t_new=t_new)
    conv_s = jnp.pad(state_conv[0], ((0, 0), (8 - (CONV_W - 1), 0), (0, 0)))
    ogs, ssm_s = _gdn(gs.reshape(n_seq, t_new, 3 * D_GDN), conv_s, cw, bas.reshape(n_seq, t_new, LANES),
                      gates.reshape(n_seq, t_new, D_GDN), alog_pad, dtb_pad, gg8, state_ssm[0],
                      bb=2, tb=t_new)
    to_heads = lambda a: a.reshape(n_seq, t_new, H_ATT, HEAD_DIM).transpose(0, 2, 1, 3)
    oas_hm = oas.reshape(1, ns, H_ATT, HEAD_DIM).transpose(0, 2, 1, 3)
    ys = _out_ffn(xs, oas_hm, ogs.reshape(ns, D_GDN), ag, wo, nf, wgt, wup, wdn, tm=256)

    gp3 = gp.reshape(batch, seq, 3 * D_GDN)
    gs3 = gs.reshape(n_seq, t_new, 3 * D_GDN)
    return (yp.reshape(batch, seq, D_MODEL), ys.reshape(n_seq, t_new, D_MODEL),
            jnp.swapaxes(ktp, -1, -2)[None], jnp.swapaxes(vtp, -1, -2)[None],
            to_heads(ks)[None], to_heads(vs)[None],
            ssm_p[None], ssm_s[None],
            gp3[:, seq - (CONV_W - 1):][None], gs3[:, t_new - (CONV_W - 1):][None])
```

```python
import functools

import numpy as np
import jax
import jax.numpy as jnp
from jax import lax
from jax.experimental import pallas as pl
from jax.experimental.pallas import tpu as pltpu

f32 = jnp.float32
bf16 = jnp.bfloat16

D_MODEL = 1024
HEAD_DIM = 64
H_ATT = 8
H_GDN = 8
D_ATT = H_ATT * HEAD_DIM
D_GDN = H_GDN * HEAD_DIM
MOBA_BLOCK = 256
MOBA_SHIFT = 8
MOBA_TOPK = 3
GDN_CHUNK = 64
CONV_W = 4
PAGE_SIZE = 128
RMS_EPS = 1e-6
L2_EPS = 1e-6
ATT_SCALE = HEAD_DIM ** -0.5
NEG = -1e30
LANES = 128
VMEM_LIMIT = 56 * 1024 * 1024

HI = lax.Precision.HIGHEST
NN = (((1,), (0,)), ((), ()))
NT = (((1,), (1,)), ((), ()))
TN = (((0,), (0,)), ((), ()))

ALIBI_SLOPES = [2.0 ** (-8.0 * (i + 1) / H_ATT) for i in range(H_ATT)]


def _dot(a, b, dims=NN, precision=None):
    return lax.dot_general(a, b, dims, precision=precision, preferred_element_type=f32)


def _bdot(a, b, dims=NN):
    return lax.dot_general(a.astype(bf16), b.astype(bf16), dims, preferred_element_type=f32)


def _rms(x, axis=-1):
    return x * lax.rsqrt(jnp.mean(x * x, axis=axis, keepdims=True) + RMS_EPS)


def _silu(x):
    return x * jax.nn.sigmoid(x)


def _iota(shape, dim):
    return lax.broadcasted_iota(jnp.int32, shape, dim)


def _resident(shape):
    return pl.BlockSpec(shape, lambda i: (0,) * len(shape), pipeline_mode=pl.Buffered(1))


def _in_proj_prompt_body(x_ref, nm_ref, wq_ref, wkvt_ref, wg_ref, wgate_ref, wba_ref, qg_ref, kgc_ref,
                         q_ref, kt_ref, vt_ref, g_ref, gate_ref, ba_ref):
    x = x_ref[...]
    xb = (_rms(x) * nm_ref[...]).astype(bf16)
    zq = jnp.dot(xb, wq_ref[...], preferred_element_type=f32)
    zkv = _dot(wkvt_ref[...], xb, NT)
    for h in range(H_ATT):
        lo, hi = h * HEAD_DIM, (h + 1) * HEAD_DIM
        q_ref[0, h] = _rms(zq[:, lo:hi]) * qg_ref[...]
        kt_ref[0, h] = _rms(zkv[lo:hi, :], axis=0) * kgc_ref[...]
        vt_ref[0, h] = zkv[D_ATT + lo:D_ATT + hi, :]
    g_ref[...] = jnp.dot(xb, wg_ref[...], preferred_element_type=f32)
    gate_ref[...] = jnp.dot(xb, wgate_ref[...], preferred_element_type=f32)
    ba_ref[...] = jnp.dot(xb, wba_ref[...], preferred_element_type=f32)


def _in_proj_prompt(x2d, nm, wq, wkvt, wg, wgate, wba, qg, kgc, *, batch, seq, tm):
    n = x2d.shape[0]
    assert n == batch * seq and seq % tm == 0
    spb = seq // tm
    full = lambda shape: _resident(shape)
    rows = lambda width: pl.BlockSpec((tm, width), lambda i: (i, 0))
    q_spec = pl.BlockSpec((1, H_ATT, tm, HEAD_DIM), lambda i: (i // spb, 0, i % spb, 0))
    t_spec = pl.BlockSpec((1, H_ATT, HEAD_DIM, tm), lambda i: (i // spb, 0, 0, i % spb))
    t_shape = jax.ShapeDtypeStruct((batch, H_ATT, HEAD_DIM, seq), f32)
    return pl.pallas_call(
        _in_proj_prompt_body,
        grid=(n // tm,),
        in_specs=[rows(D_MODEL), full((1, D_MODEL)), full(wq.shape), full(wkvt.shape), full(wg.shape),
                  full(wgate.shape), full(wba.shape), full((1, HEAD_DIM)), full((HEAD_DIM, 1))],
        out_specs=[q_spec, t_spec, t_spec, rows(3 * D_GDN), rows(D_GDN), rows(LANES)],
        out_shape=[jax.ShapeDtypeStruct((batch, H_ATT, seq, HEAD_DIM), f32), t_shape, t_shape,
                   jax.ShapeDtypeStruct((n, 3 * D_GDN), f32),
                   jax.ShapeDtypeStruct((n, D_GDN), f32),
                   jax.ShapeDtypeStruct((n, LANES), f32)],
        compiler_params=pltpu.CompilerParams(dimension_semantics=("arbitrary",),
                                             vmem_limit_bytes=VMEM_LIMIT),
        name="in_proj_prompt",
    )(x2d, nm, wq, wkvt, wg, wgate, wba, qg, kgc)


def _in_proj_sample_body(x_ref, nm_ref, wqkv_ref, wg_ref, wgate_ref, wba_ref, qg_ref, kg_ref,
                         q_ref, k_ref, v_ref, g_ref, gate_ref, ba_ref):
    x = x_ref[...]
    xb = (_rms(x) * nm_ref[...]).astype(bf16)
    z = jnp.dot(xb, wqkv_ref[...], preferred_element_type=f32)
    qs, ks = [], []
    for h in range(H_ATT):
        lo, hi = h * HEAD_DIM, (h + 1) * HEAD_DIM
        qs.append(_rms(z[:, lo:hi]) * qg_ref[...])
        ks.append(_rms(z[:, D_ATT + lo:D_ATT + hi]) * kg_ref[...])
    q_ref[...] = jnp.concatenate(qs, axis=-1)
    k_ref[...] = jnp.concatenate(ks, axis=-1)
    v_ref[...] = z[:, 2 * D_ATT:]
    g_ref[...] = jnp.dot(xb, wg_ref[...], preferred_element_type=f32)
    gate_ref[...] = jnp.dot(xb, wgate_ref[...], preferred_element_type=f32)
    ba_ref[...] = jnp.dot(xb, wba_ref[...], preferred_element_type=f32)


def _in_proj_sample(x2d, nm, wqkv, wg, wgate, wba, qg, kg, *, tm):
    n = x2d.shape[0]
    assert n % tm == 0
    full = lambda shape: _resident(shape)
    rows = lambda width: pl.BlockSpec((tm, width), lambda i: (i, 0))
    tok = jax.ShapeDtypeStruct((n, D_ATT), f32)
    return pl.pallas_call(
        _in_proj_sample_body,
        grid=(n // tm,),
        in_specs=[rows(D_MODEL), full((1, D_MODEL)), full(wqkv.shape), full(wg.shape),
                  full(wgate.shape), full(wba.shape), full((1, HEAD_DIM)), full((1, HEAD_DIM))],
        out_specs=[rows(D_ATT), rows(D_ATT), rows(D_ATT), rows(3 * D_GDN), rows(D_GDN), rows(LANES)],
        out_shape=[tok, tok, tok,
                   jax.ShapeDtypeStruct((n, 3 * D_GDN), f32),
                   jax.ShapeDtypeStruct((n, D_GDN), f32),
                   jax.ShapeDtypeStruct((n, LANES), f32)],
        compiler_params=pltpu.CompilerParams(dimension_semantics=("arbitrary",),
                                             vmem_limit_bytes=VMEM_LIMIT),
        name="in_proj_sample",
    )(x2d, nm, wqkv, wg, wgate, wba, qg, kg)


def _fold_lanes(s, op):
    out = s[:, :LANES]
    for c0 in range(LANES, s.shape[1], LANES):
        out = op(out, s[:, c0:c0 + LANES])
    return out


MOBA_HEADS = 2
MOBA_TRIP = 4


def _moba_prompt_body(slope_ref, q_ref, kt_ref, vt_ref, o_ref, ka_ref, va_ref, kmean_ref, s_ref, *, seq):
    hg = pl.program_id(1)
    i = pl.program_id(2)
    n_blk = seq // MOBA_BLOCK
    trip = MOBA_TRIP * MOBA_BLOCK
    slopes = [slope_ref[pl.ds(hg * MOBA_HEADS + j, 1), 0:1] for j in range(MOBA_HEADS)]

    @pl.when(i == 0)
    def _():
        row = _iota((HEAD_DIM, seq), 0)
        pos = _iota((HEAD_DIM, seq), 1)
        blk = jnp.right_shift(pos, MOBA_SHIFT)
        jloc = jnp.bitwise_and(pos, MOBA_BLOCK - 1)
        onehot = jnp.where(row == blk, 1.0, 0.0)
        ones_row = jnp.where(row == 0, 1.0, 0.0).astype(bf16)
        for j in range(MOBA_HEADS):
            kt = kt_ref[0, j]
            e = jnp.where(row == n_blk, slopes[j] * jloc.astype(f32), onehot)
            e = jnp.where(row == n_blk + 1, (slopes[j] * MOBA_BLOCK) * blk.astype(f32), e)
            e = jnp.where(row == n_blk + 2, 1.0, e)
            ka_ref[j, 0:HEAD_DIM, :] = kt.astype(bf16)
            ka_ref[j, HEAD_DIM:, :] = e.astype(bf16)
            va_ref[j, 0:HEAD_DIM, :] = vt_ref[0, j].astype(bf16)
            va_ref[j, HEAD_DIM:, :] = ones_row
            cols = [jnp.sum(kt[:, n * MOBA_BLOCK:(n + 1) * MOBA_BLOCK], axis=1, keepdims=True)
                    for n in range(n_blk)]
            kmean_ref[j] = (jnp.concatenate(cols, axis=1) * (1.0 / MOBA_BLOCK)).T

    nrow = _iota((n_blk, MOBA_BLOCK), 0)
    row8 = _iota((8, MOBA_BLOCK), 0)
    ivec = jnp.full((n_blk, MOBA_BLOCK), i, jnp.int32)
    eye_q = (_iota((MOBA_BLOCK, MOBA_BLOCK), 0) == _iota((MOBA_BLOCK, MOBA_BLOCK), 1)).astype(bf16)
    causal = _iota((MOBA_BLOCK, MOBA_BLOCK), 1) <= _iota((MOBA_BLOCK, MOBA_BLOCK), 0)
    d0 = pl.multiple_of(i * MOBA_BLOCK, MOBA_BLOCK)
    qa, s_own = [], []
    for j in range(MOBA_HEADS):
        q = q_ref[0, j]
        gt = _dot(kmean_ref[j], q, NT, HI)
        gt = jnp.where(nrow < ivec, gt, -jnp.inf)
        ranks = []
        for g0 in range(0, n_blk, 8):
            grp = gt[g0:g0 + 8, :]
            rank = jnp.zeros((8, MOBA_BLOCK), f32)
            for m in range(n_blk):
                gm = gt[m:m + 1, :]
                if m < g0:
                    beats = gm >= grp
                elif m >= g0 + 8:
                    beats = gm > grp
                else:
                    beats = (gm > grp) | ((gm == grp) & (row8 > m - g0))
                rank = rank + jnp.where(beats, 1.0, 0.0)
            ranks.append(rank)
        sel = (nrow < ivec) & (jnp.concatenate(ranks, axis=0) < MOBA_TOPK)
        shift = -(slopes[j] * MOBA_BLOCK) * i.astype(f32)
        c_t = jnp.where(nrow < 2, 1.0, jnp.where(nrow == 2, shift, 0.0))
        sel_pen = jnp.concatenate([jnp.where(sel, 0.0, NEG), c_t], axis=0).astype(bf16)
        own_pen = jnp.concatenate([jnp.zeros((n_blk, MOBA_BLOCK), f32), c_t], axis=0).astype(bf16)
        qs = q * ATT_SCALE
        qa.append(jnp.concatenate([qs, _dot(eye_q, sel_pen, NT)], axis=-1).astype(bf16))
        qo = jnp.concatenate([qs, _dot(eye_q, own_pen, NT)], axis=-1).astype(bf16)
        s_own.append(jnp.where(causal, _dot(qo, ka_ref[j, :, pl.ds(d0, MOBA_BLOCK)]), NEG))
    n_trip = (i + MOBA_TRIP - 1) // MOBA_TRIP

    def pass1(t, mx):
        r0 = pl.multiple_of(t * trip, trip)
        out = []
        for j in range(MOBA_HEADS):
            s = _dot(qa[j], ka_ref[j, :, pl.ds(r0, trip)])
            s_ref[j, :, pl.ds(r0, trip)] = s
            out.append(jnp.maximum(mx[j], _fold_lanes(s, jnp.maximum)))
        return tuple(out)

    mx = lax.fori_loop(0, n_trip, pass1, tuple(_fold_lanes(s, jnp.maximum) for s in s_own))
    m_b = [jnp.broadcast_to(jnp.max(mx[j], axis=1, keepdims=True), (MOBA_BLOCK, LANES))
           for j in range(MOBA_HEADS)]

    def probs(s, j):
        return jnp.concatenate([jnp.exp(s[:, c0:c0 + LANES] - m_b[j]) for c0 in range(0, s.shape[1], LANES)],
                               axis=1).astype(bf16)

    def pass2(t, acc):
        r0 = pl.multiple_of(t * trip, trip)
        return tuple(acc[j] + _dot(probs(s_ref[j, :, pl.ds(r0, trip)], j), va_ref[j, :, pl.ds(r0, trip)], NT)
                     for j in range(MOBA_HEADS))

    acc = lax.fori_loop(0, n_trip, pass2,
                        tuple(_dot(probs(s_own[j], j), va_ref[j, :, pl.ds(d0, MOBA_BLOCK)], NT)
                              for j in range(MOBA_HEADS)))
    for j in range(MOBA_HEADS):
        o_ref[0, j] = acc[j][:, :HEAD_DIM] / acc[j][:, HEAD_DIM:HEAD_DIM + 1]


def _moba_prompt(q, kt, vt):
    batch, heads, seq, _ = q.shape
    n_blk = seq // MOBA_BLOCK
    assert n_blk % MOBA_TRIP == 0 and 2 * n_blk == HEAD_DIM and MOBA_BLOCK == 1 << MOBA_SHIFT
    assert heads % MOBA_HEADS == 0
    slopes = jnp.asarray(np.repeat(np.asarray(ALIBI_SLOPES, np.float32)[:, None], LANES, axis=1))
    tile = pl.BlockSpec((1, MOBA_HEADS, MOBA_BLOCK, HEAD_DIM), lambda b, h, i: (b, h, i, 0))
    whole = pl.BlockSpec((1, MOBA_HEADS, HEAD_DIM, seq), lambda b, h, i: (b, h, 0, 0))
    return pl.pallas_call(
        functools.partial(_moba_prompt_body, seq=seq),
        grid=(batch, heads // MOBA_HEADS, n_blk),
        in_specs=[pl.BlockSpec((H_ATT, LANES), lambda b, h, i: (0, 0)), tile, whole, whole],
        out_specs=tile,
        out_shape=jax.ShapeDtypeStruct(q.shape, f32),
        scratch_shapes=[pltpu.VMEM((MOBA_HEADS, 2 * HEAD_DIM, seq), bf16),
                        pltpu.VMEM((MOBA_HEADS, 2 * HEAD_DIM, seq), bf16),
                        pltpu.VMEM((MOBA_HEADS, n_blk, HEAD_DIM), f32),
                        pltpu.VMEM((MOBA_HEADS, MOBA_BLOCK, seq), f32)],
        compiler_params=pltpu.CompilerParams(dimension_semantics=("arbitrary",) * 3,
                                             vmem_limit_bytes=VMEM_LIMIT),
        name="moba_prompt",
    )(slopes, q, kt, vt)


def _moba_sample_body(pt_ref, q_ref, k_ref, v_ref, *refs, past_len, t_new):
    del pt_ref
    n_pages = past_len // PAGE_SIZE
    kp_refs, vp_refs, o_ref = refs[:n_pages], refs[n_pages:2 * n_pages], refs[2 * n_pages]
    own = past_len // MOBA_BLOCK
    ppb = MOBA_BLOCK // PAGE_SIZE
    n_sel = min(MOBA_TOPK, own)
    tq = _iota((t_new, past_len), 0)
    sk = _iota((t_new, past_len), 1)
    dist = (past_len + tq - sk).astype(f32)
    tq2 = _iota((t_new, t_new), 0)
    sk2 = _iota((t_new, t_new), 1)
    outs = []
    for h in range(H_ATT):
        lo, hi = h * HEAD_DIM, (h + 1) * HEAD_DIM
        slope = ALIBI_SLOPES[h]
        q = q_ref[:, lo:hi]
        qb = (q * ATT_SCALE).astype(bf16)
        kb, cols = [], []
        for n in range(own):
            pages = [kp_refs[n * ppb + j][0, h] for j in range(ppb)]
            cols.append(sum(jnp.sum(pg, axis=1, keepdims=True) for pg in pages))
            kb.append(jnp.concatenate([pg.astype(bf16) for pg in pages], axis=1))
        kmean_t = jnp.concatenate(cols, axis=1) * (1.0 / MOBA_BLOCK)
        gate = _dot(q, kmean_t, NN, HI)
        s_blocks = []
        for n in range(own):
            gn = gate[:, n:n + 1]
            rank = jnp.zeros((t_new, 1), f32)
            for m in range(own):
                if m == n:
                    continue
                gm = gate[:, m:m + 1]
                beats = (gm > gn) | ((gm == gn) & (m < n))
                rank = rank + jnp.where(beats, 1.0, 0.0)
            pen = jnp.where(rank < n_sel, 0.0, NEG)
            s_blocks.append(_dot(qb, kb[n]) + pen)
        s_past = jnp.concatenate(s_blocks, axis=1) - slope * dist
        s_own = _dot(qb, k_ref[:, lo:hi].astype(bf16), NT) - slope * (tq2 - sk2).astype(f32)
        s_own = jnp.where(sk2 <= tq2, s_own, NEG)
        m = jnp.maximum(jnp.max(s_past, axis=1, keepdims=True), jnp.max(s_own, axis=1, keepdims=True))
        p_past = jnp.exp(s_past - m)
        p_own = jnp.exp(s_own - m)
        denom = jnp.sum(p_past, axis=1, keepdims=True) + jnp.sum(p_own, axis=1, keepdims=True)
        o = _dot(p_own.astype(bf16), v_ref[:, lo:hi].astype(bf16))
        for n in range(own):
            vb = jnp.concatenate([vp_refs[n * ppb + j][0, h].astype(bf16) for j in range(ppb)], axis=1)
            o = o + _dot(p_past[:, n * MOBA_BLOCK:(n + 1) * MOBA_BLOCK].astype(bf16), vb, NT)
        outs.append(o / denom)
    o_ref[...] = jnp.concatenate(outs, axis=-1)


def _moba_sample(q, k, v, cache_kt, cache_vt, page_table, *, t_new):
    n_seq, n_pages = page_table.shape
    past_len = n_pages * PAGE_SIZE
    assert past_len % MOBA_BLOCK == 0 and past_len // MOBA_BLOCK >= 1
    rows = pl.BlockSpec((t_new, D_ATT), lambda b, pt: (b, 0))

    def page_spec(p):
        return pl.BlockSpec((1, H_ATT, HEAD_DIM, PAGE_SIZE), lambda b, pt: (pt[b * n_pages + p], 0, 0, 0))

    grid_spec = pltpu.PrefetchScalarGridSpec(
        num_scalar_prefetch=1,
        grid=(n_seq,),
        in_specs=[rows, rows, rows] + [page_spec(p) for p in range(n_pages)] * 2,
        out_specs=rows,
    )
    return pl.pallas_call(
        functools.partial(_moba_sample_body, past_len=past_len, t_new=t_new),
        grid_spec=grid_spec,
        out_shape=jax.ShapeDtypeStruct(q.shape, f32),
        compiler_params=pltpu.CompilerParams(dimension_semantics=("arbitrary",),
                                             vmem_limit_bytes=VMEM_LIMIT),
        name="moba_sample",
    )(page_table.reshape(-1), q, k, v, *([cache_kt] * n_pages), *([cache_vt] * n_pages))


def _conv_silu(cur, prev8, w):
    t = cur.shape[0]
    row8 = _iota((8, cur.shape[1]), 0)
    y = None
    for i in range(CONV_W):
        d = CONV_W - 1 - i
        if d == 0:
            term = cur
        else:
            sh = pltpu.roll(cur, d, 0)
            top = jnp.where(row8 < d, pltpu.roll(prev8, d, 0), sh[0:8])
            term = top if t == 8 else jnp.concatenate([top, sh[8:]], axis=0)
        term = term * w[i:i + 1, :]
        y = term if y is None else y + term
    return _silu(y)


def _gdn_body(uq_ref, uk_ref, uv_ref, pq_ref, pk_ref, pv_ref, cq_ref, ck_ref, cv_ref,
              wq_ref, wk_ref, wv_ref, ba_ref, gate_ref, alog_ref, dtb_ref, gg_ref, m0_ref,
              o_ref, m_ref, *, bb, tb, c):
    t = pl.program_id(1)

    @pl.when(t == 0)
    def _():
        m_ref[...] = m0_ref[...]

    n_chunks = tb // c
    lc = int(np.log2(c))
    assert 1 << lc == c
    ri = _iota((tb, tb), 0)
    ci = _iota((tb, tb), 1)
    same = jnp.right_shift(ri, lc) == jnp.right_shift(ci, lc)
    eye = ri == ci
    keep = same & (ri >= ci)
    cum = jnp.where(keep, 1.0, 0.0)
    tot = jnp.where(same, 1.0, 0.0)
    ident = jnp.where(eye, 1.0, 0.0)
    first = jnp.full((8, D_GDN), t, jnp.int32) == 0

    for s in range(bb):
        q = _conv_silu(uq_ref[s], jnp.where(first, cq_ref[s], pq_ref[s]), wq_ref[...])
        k = _conv_silu(uk_ref[s], jnp.where(first, ck_ref[s], pk_ref[s]), wk_ref[...])
        v = _conv_silu(uv_ref[s], jnp.where(first, cv_ref[s], pv_ref[s]), wv_ref[...])
        ba = ba_ref[s]
        beta_all = jax.nn.sigmoid(ba)
        xa = ba + dtb_ref[...]
        softplus = jnp.maximum(xa, 0.0) + jnp.log1p(jnp.exp(-jnp.abs(xa)))
        g_all = -jnp.exp(alog_ref[...]) * softplus
        gcum_all = _dot(cum, g_all, NN, HI)
        gtot_all = _dot(tot, g_all, NN, HI)
        gcum_t = gcum_all.T if tb % LANES == 0 else None
        hs = range(H_GDN)
        beta = [beta_all[:, h:h + 1] for h in hs]
        gcum = [gcum_all[:, H_GDN + h:H_GDN + h + 1] for h in hs]
        gtot = [gtot_all[:, H_GDN + h:H_GDN + h + 1] for h in hs]
        if gcum_t is None:
            grow = [jnp.sum(jnp.where(eye, gcum[h], 0.0), axis=0, keepdims=True) for h in hs]
        else:
            grow = [gcum_t[H_GDN + h:H_GDN + h + 1, :] for h in hs]
        l2n = lambda z: z * lax.rsqrt(jnp.sum(z * z, axis=-1, keepdims=True) + L2_EPS)
        qh = [l2n(q[:, h * HEAD_DIM:(h + 1) * HEAD_DIM]) * (HEAD_DIM ** -0.5) for h in hs]
        kh = [l2n(k[:, h * HEAD_DIM:(h + 1) * HEAD_DIM]) for h in hs]
        vh = [v[:, h * HEAD_DIM:(h + 1) * HEAD_DIM] for h in hs]
        decay = [jnp.exp(jnp.where(keep, gcum[h] - grow[h], NEG)) for h in hs]
        pw = [beta[h] * _bdot(kh[h], kh[h], NT) * jnp.where(eye, 0.0, decay[h]) for h in hs]
        x = [ident - pw[h] for h in hs]
        for _ in range(lc - 1):
            pw = [_bdot(pw[h], pw[h]) for h in hs]
            x = [x[h] + _bdot(x[h], pw[h]) for h in hs]
        eg = [jnp.exp(gcum[h]) for h in hs]
        sol = [_bdot(x[h], jnp.concatenate([beta[h] * vh[h], (beta[h] * eg[h]) * kh[h]], axis=-1))
               for h in hs]
        aqk = [_bdot(qh[h], kh[h], NT) * decay[h] for h in hs]
        qg = [qh[h] * eg[h] for h in hs]
        kd = [kh[h] * jnp.exp(gtot[h] - gcum[h]) for h in hs]
        gdec = [jnp.exp(gtot[h]) for h in hs]
        m = [m_ref[s, h] for h in hs]
        deltas = [[] for _ in hs]
        oqs = [[] for _ in hs]
        for ch in range(n_chunks):
            sl = slice(ch * c, (ch + 1) * c)
            both = [_bdot(jnp.concatenate([sol[h][sl, HEAD_DIM:], qg[h][sl]], axis=0), m[h]) for h in hs]
            for h in hs:
                deltas[h].append(sol[h][sl, :HEAD_DIM] - both[h][:c])
                oqs[h].append(both[h][c:])
            m = [gdec[h][ch * c:ch * c + 1, :] * m[h] + _dot(kd[h][sl], deltas[h][ch], TN) for h in hs]
        outs = []
        for h in hs:
            m_ref[s, h] = m[h]
            if n_chunks == 1:
                delta_all, oq = deltas[h][0], oqs[h][0]
            else:
                delta_all, oq = jnp.concatenate(deltas[h], axis=0), jnp.concatenate(oqs[h], axis=0)
            outs.append(_rms(oq + _bdot(aqk[h], delta_all)))
        o = jnp.concatenate(outs, axis=-1) * gg_ref[...]
        o_ref[s] = o * _silu(gate_ref[s])


def _gdn(u, conv_pad, conv_w, ba, gate, alog_pad, dtb_pad, gg8, m0, *, bb, tb):
    batch, seq, _ = u.shape
    c = min(GDN_CHUNK, seq)
    assert batch % bb == 0 and seq % tb == 0 and tb % c == 0 and tb % 8 == 0
    cur = lambda g: pl.BlockSpec((bb, tb, D_GDN), lambda b, t: (b, t, g))
    prev = lambda g: pl.BlockSpec((bb, 8, D_GDN), lambda b, t: (b, jnp.maximum(t * (tb // 8) - 1, 0), g))
    cbuf = lambda g: pl.BlockSpec((bb, 8, D_GDN), lambda b, t: (b, 0, g))
    cw = lambda g: pl.BlockSpec((CONV_W, D_GDN), lambda b, t: (0, g))
    row = lambda width: pl.BlockSpec((1, width), lambda b, t: (0, 0))
    state = pl.BlockSpec((bb, H_GDN, HEAD_DIM, HEAD_DIM), lambda b, t: (b, 0, 0, 0))
    return pl.pallas_call(
        functools.partial(_gdn_body, bb=bb, tb=tb, c=c),
        grid=(batch // bb, seq // tb),
        in_specs=[cur(0), cur(1), cur(2), prev(0), prev(1), prev(2), cbuf(0), cbuf(1), cbuf(2),
                  cw(0), cw(1), cw(2),
                  pl.BlockSpec((bb, tb, LANES), lambda b, t: (b, t, 0)),
                  pl.BlockSpec((bb, tb, D_GDN), lambda b, t: (b, t, 0)),
                  row(LANES), row(LANES), row(D_GDN), state],
        out_specs=[pl.BlockSpec((bb, tb, D_GDN), lambda b, t: (b, t, 0)), state],
        out_shape=[jax.ShapeDtypeStruct((batch, seq, D_GDN), f32),
                   jax.ShapeDtypeStruct((batch, H_GDN, HEAD_DIM, HEAD_DIM), f32)],
        compiler_params=pltpu.CompilerParams(dimension_semantics=("arbitrary",) * 2,
                                             vmem_limit_bytes=VMEM_LIMIT),
        name="gdn",
    )(u, u, u, u, u, u, conv_pad, conv_pad, conv_pad, conv_w, conv_w, conv_w,
      ba, gate, alog_pad, dtb_pad, gg8, m0)


FF_CHUNK = 256


def _out_ffn_body(x_ref, oa_ref, og_ref, ag_ref, wo_ref, nf_ref, wg_ref, wu_ref, wd_ref, y_ref):
    oa = jnp.concatenate([oa_ref[0, h] for h in range(H_ATT)], axis=-1)
    oa = _rms(oa) * ag_ref[...]
    mix = jnp.concatenate([oa, og_ref[...]], axis=-1).astype(bf16)
    hid = x_ref[...] + jnp.dot(mix, wo_ref[...], preferred_element_type=f32)
    ub = (_rms(hid) * nf_ref[...]).astype(bf16)
    d_ff = wg_ref.shape[1]
    acc = hid
    for cidx in range(d_ff // FF_CHUNK):
        sl = slice(cidx * FF_CHUNK, (cidx + 1) * FF_CHUNK)
        a = jnp.dot(ub, wg_ref[:, sl], preferred_element_type=f32)
        b = jnp.dot(ub, wu_ref[:, sl], preferred_element_type=f32)
        acc = acc + jnp.dot((_silu(a) * b).astype(bf16), wd_ref[sl, :], preferred_element_type=f32)
    y_ref[...] = acc


def _out_ffn(x2d, o_att, o_gdn, ag, wo, nf, wg, wu, wd, *, tm):
    n = x2d.shape[0]
    batch, _, seq, _ = o_att.shape
    assert n == batch * seq and seq % tm == 0 and wg.shape[1] % FF_CHUNK == 0
    spb = seq // tm
    full = lambda shape: _resident(shape)
    rows = lambda width: pl.BlockSpec((tm, width), lambda i: (i, 0))
    return pl.pallas_call(
        _out_ffn_body,
        grid=(n // tm,),
        in_specs=[rows(D_MODEL),
                  pl.BlockSpec((1, H_ATT, tm, HEAD_DIM), lambda i: (i // spb, 0, i % spb, 0)),
                  rows(D_GDN), full((1, D_ATT)), full(wo.shape), full((1, D_MODEL)),
                  full(wg.shape), full(wu.shape), full(wd.shape)],
        out_specs=rows(D_MODEL),
        out_shape=jax.ShapeDtypeStruct((n, D_MODEL), f32),
        compiler_params=pltpu.CompilerParams(dimension_semantics=("arbitrary",),
                                             vmem_limit_bytes=VMEM_LIMIT),
        name="out_ffn",
    )(x2d, o_att, o_gdn, ag, wo, nf, wg, wu, wd)


def kernel(x_prompt, x_sample, cache_k, cache_v, state_ssm, state_conv, page_table, norm_mix, w_in,
           q_norm_g, k_norm_g, attn_out_g, conv_w, a_log, dt_bias, gdn_out_g, w_out, norm_ffn,
           w_gate, w_up, w_down):
    depth = w_in.shape[0]
    assert depth == 1
    batch, seq, _ = x_prompt.shape
    n_seq, t_new, _ = x_sample.shape

    w = w_in[0]
    c0, c1, c2 = 3 * D_ATT, 3 * D_ATT + 3 * D_GDN, 3 * D_ATT + 4 * D_GDN
    wqkv = w[:, :c0].astype(bf16)
    wq = w[:, :D_ATT].astype(bf16)
    wkvt = w.T[D_ATT:c0].astype(bf16)
    wg = w[:, c0:c1].astype(bf16)
    wgate = w[:, c1:c2].astype(bf16)
    wba = jnp.pad(w[:, c2:], ((0, 0), (0, LANES - 2 * H_GDN))).astype(bf16)
    nm = norm_mix[0].reshape(1, D_MODEL)
    qg = q_norm_g[0].reshape(1, HEAD_DIM)
    kg = k_norm_g[0].reshape(1, HEAD_DIM)
    kgc = k_norm_g[0].reshape(HEAD_DIM, 1)
    ag = attn_out_g[0].reshape(1, D_ATT)
    nf = norm_ffn[0].reshape(1, D_MODEL)
    cw = conv_w[0]
    pad_lo = lambda vec: jnp.pad(vec.astype(f32), (H_GDN, LANES - 2 * H_GDN)).reshape(1, LANES)
    alog_pad = pad_lo(a_log[0])
    dtb_pad = pad_lo(dt_bias[0])
    gg8 = jnp.tile(gdn_out_g[0].astype(f32), H_GDN).reshape(1, D_GDN)
    wo = w_out[0].astype(bf16)
    wgt = w_gate[0].astype(bf16)
    wup = w_up[0].astype(bf16)
    wdn = w_down[0].astype(bf16)

    xp = x_prompt.reshape(batch * seq, D_MODEL)
    qp, ktp, vtp, gp, gatep, bap = _in_proj_prompt(xp, nm, wq, wkvt, wg, wgate, wba, qg, kgc,
                                                   batch=batch, seq=seq, tm=512)
    oap = _moba_prompt(qp, ktp, vtp)
    conv0 = jnp.zeros((batch, 8, 3 * D_GDN), f32)
    ssm0 = jnp.zeros((batch, H_GDN, HEAD_DIM, HEAD_DIM), f32)
    ogp, ssm_p = _gdn(gp.reshape(batch, seq, 3 * D_GDN), conv0, cw, bap.reshape(batch, seq, LANES),
                      gatep.reshape(batch, seq, D_GDN), alog_pad, dtb_pad, gg8, ssm0, bb=1, tb=256)
    yp = _out_ffn(xp, oap, ogp.reshape(batch * seq, D_GDN), ag, wo, nf, wgt, wup, wdn, tm=512)

    ns = n_seq * t_new
    xs = x_sample.reshape(ns, D_MODEL)
    qs, ks, vs, gs, gates, bas = _in_proj_sample(xs, nm, wqkv, wg, wgate, wba, qg, kg, tm=256)
    cache_kt = jnp.swapaxes(cache_k[0], -1, -2)
    cache_vt = jnp.swapaxes(cache_v[0], -1, -2)
    oas = _moba_sample(qs, ks, vs, cache_kt, cache_vt, page_table, t_new=t_new)
    conv_s = jnp.pad(state_conv[0], ((0, 0), (8 - (CONV_W - 1), 0), (0, 0)))
    ogs, ssm_s = _gdn(gs.reshape(n_seq, t_new, 3 * D_GDN), conv_s, cw, bas.reshape(n_seq, t_new, LANES),
                      gates.reshape(n_seq, t_new, D_GDN), alog_pad, dtb_pad, gg8, state_ssm[0],
                      bb=2, tb=t_new)
    to_heads = lambda a: a.reshape(n_seq, t_new, H_ATT, HEAD_DIM).transpose(0, 2, 1, 3)
    oas_hm = oas.reshape(1, ns, H_ATT, HEAD_DIM).transpose(0, 2, 1, 3)
    ys = _out_ffn(xs, oas_hm, ogs.reshape(ns, D_GDN), ag, wo, nf, wgt, wup, wdn, tm=512)

    gp3 = gp.reshape(batch, seq, 3 * D_GDN)
    gs3 = gs.reshape(n_seq, t_new, 3 * D_GDN)
    return (yp.reshape(batch, seq, D_MODEL), ys.reshape(n_seq, t_new, D_MODEL),
            jnp.swapaxes(ktp, -1, -2)[None], jnp.swapaxes(vtp, -1, -2)[None],
            to_heads(ks)[None], to_heads(vs)[None],
            ssm_p[None], ssm_s[None],
            gp3[:, seq - (CONV_W - 1):][None], gs3[:, t_new - (CONV_W - 1):][None])
```

```python
import functools

import numpy as np
import jax
import jax.numpy as jnp
from jax import lax
from jax.experimental import pallas as pl
from jax.experimental.pallas import tpu as pltpu

f32 = jnp.float32
bf16 = jnp.bfloat16

D_MODEL = 1024
HEAD_DIM = 64
H_ATT = 8
H_GDN = 8
D_ATT = H_ATT * HEAD_DIM
D_GDN = H_GDN * HEAD_DIM
MOBA_BLOCK = 256
MOBA_SHIFT = 8
MOBA_TOPK = 3
GDN_CHUNK = 64
CONV_W = 4
PAGE_SIZE = 128
RMS_EPS = 1e-6
L2_EPS = 1e-6
ATT_SCALE = HEAD_DIM ** -0.5
NEG = -1e30
LANES = 128
VMEM_LIMIT = 56 * 1024 * 1024

HI = lax.Precision.HIGHEST
NN = (((1,), (0,)), ((), ()))
NT = (((1,), (1,)), ((), ()))
TN = (((0,), (0,)), ((), ()))

ALIBI_SLOPES = [2.0 ** (-8.0 * (i + 1) / H_ATT) for i in range(H_ATT)]


def _dot(a, b, dims=NN, precision=None):
    return lax.dot_general(a, b, dims, precision=precision, preferred_element_type=f32)


def _bdot(a, b, dims=NN):
    return lax.dot_general(a.astype(bf16), b.astype(bf16), dims, preferred_element_type=f32)


def _rms(x, axis=-1):
    return x * lax.rsqrt(jnp.mean(x * x, axis=axis, keepdims=True) + RMS_EPS)


def _silu(x):
    return x * jax.nn.sigmoid(x)


def _iota(shape, dim):
    return lax.broadcasted_iota(jnp.int32, shape, dim)


def _resident(shape):
    return pl.BlockSpec(shape, lambda i: (0,) * len(shape), pipeline_mode=pl.Buffered(1))


def _in_proj_prompt_body(x_ref, nm_ref, wq_ref, wkvt_ref, wg_ref, wgate_ref, wba_ref, qg_ref, kgc_ref,
                         q_ref, kt_ref, vt_ref, g_ref, gate_ref, ba_ref):
    x = x_ref[...]
    xb = (_rms(x) * nm_ref[...]).astype(bf16)
    zq = jnp.dot(xb, wq_ref[...], preferred_element_type=f32)
    zkv = _dot(wkvt_ref[...], xb, NT)
    for h in range(H_ATT):
        lo, hi = h * HEAD_DIM, (h + 1) * HEAD_DIM
        q_ref[0, h] = _rms(zq[:, lo:hi]) * qg_ref[...]
        kt_ref[0, h] = _rms(zkv[lo:hi, :], axis=0) * kgc_ref[...]
        vt_ref[0, h] = zkv[D_ATT + lo:D_ATT + hi, :]
    g_ref[...] = jnp.dot(xb, wg_ref[...], preferred_element_type=f32)
    gate_ref[...] = jnp.dot(xb, wgate_ref[...], preferred_element_type=f32)
    ba_ref[...] = jnp.dot(xb, wba_ref[...], preferred_element_type=f32)


def _in_proj_prompt(x2d, nm, wq, wkvt, wg, wgate, wba, qg, kgc, *, batch, seq, tm):
    n = x2d.shape[0]
    assert n == batch * seq and seq % tm == 0
    spb = seq // tm
    full = lambda shape: _resident(shape)
    rows = lambda width: pl.BlockSpec((tm, width), lambda i: (i, 0))
    q_spec = pl.BlockSpec((1, H_ATT, tm, HEAD_DIM), lambda i: (i // spb, 0, i % spb, 0))
    t_spec = pl.BlockSpec((1, H_ATT, HEAD_DIM, tm), lambda i: (i // spb, 0, 0, i % spb))
    t_shape = jax.ShapeDtypeStruct((batch, H_ATT, HEAD_DIM, seq), f32)
    return pl.pallas_call(
        _in_proj_prompt_body,
        grid=(n // tm,),
        in_specs=[rows(D_MODEL), full((1, D_MODEL)), full(wq.shape), full(wkvt.shape), full(wg.shape),
                  full(wgate.shape), full(wba.shape), full((1, HEAD_DIM)), full((HEAD_DIM, 1))],
        out_specs=[q_spec, t_spec, t_spec, rows(3 * D_GDN), rows(D_GDN), rows(LANES)],
        out_shape=[jax.ShapeDtypeStruct((batch, H_ATT, seq, HEAD_DIM), f32), t_shape, t_shape,
                   jax.ShapeDtypeStruct((n, 3 * D_GDN), f32),
                   jax.ShapeDtypeStruct((n, D_GDN), f32),
                   jax.ShapeDtypeStruct((n, LANES), f32)],
        compiler_params=pltpu.CompilerParams(dimension_semantics=("arbitrary",),
                                             vmem_limit_bytes=VMEM_LIMIT),
        name="in_proj_prompt",
    )(x2d, nm, wq, wkvt, wg, wgate, wba, qg, kgc)


def _in_proj_sample_body(x_ref, nm_ref, wqkv_ref, wg_ref, wgate_ref, wba_ref, qg_ref, kg_ref,
                         q_ref, k_ref, v_ref, g_ref, gate_ref, ba_ref):
    x = x_ref[...]
    xb = (_rms(x) * nm_ref[...]).astype(bf16)
    z = jnp.dot(xb, wqkv_ref[...], preferred_element_type=f32)
    qs, ks = [], []
    for h in range(H_ATT):
        lo, hi = h * HEAD_DIM, (h + 1) * HEAD_DIM
        qs.append(_rms(z[:, lo:hi]) * qg_ref[...])
        ks.append(_rms(z[:, D_ATT + lo:D_ATT + hi]) * kg_ref[...])
    q_ref[...] = jnp.concatenate(qs, axis=-1)
    k_ref[...] = jnp.concatenate(ks, axis=-1)
    v_ref[...] = z[:, 2 * D_ATT:]
    g_ref[...] = jnp.dot(xb, wg_ref[...], preferred_element_type=f32)
    gate_ref[...] = jnp.dot(xb, wgate_ref[...], preferred_element_type=f32)
    ba_ref[...] = jnp.dot(xb, wba_ref[...], preferred_element_type=f32)


def _in_proj_sample(x2d, nm, wqkv, wg, wgate, wba, qg, kg, *, tm):
    n = x2d.shape[0]
    assert n % tm == 0
    full = lambda shape: _resident(shape)
    rows = lambda width: pl.BlockSpec((tm, width), lambda i: (i, 0))
    tok = jax.ShapeDtypeStruct((n, D_ATT), f32)
    return pl.pallas_call(
        _in_proj_sample_body,
        grid=(n // tm,),
        in_specs=[rows(D_MODEL), full((1, D_MODEL)), full(wqkv.shape), full(wg.shape),
                  full(wgate.shape), full(wba.shape), full((1, HEAD_DIM)), full((1, HEAD_DIM))],
        out_specs=[rows(D_ATT), rows(D_ATT), rows(D_ATT), rows(3 * D_GDN), rows(D_GDN), rows(LANES)],
        out_shape=[tok, tok, tok,
                   jax.ShapeDtypeStruct((n, 3 * D_GDN), f32),
                   jax.ShapeDtypeStruct((n, D_GDN), f32),
                   jax.ShapeDtypeStruct((n, LANES), f32)],
        compiler_params=pltpu.CompilerParams(dimension_semantics=("arbitrary",),
                                             vmem_limit_bytes=VMEM_LIMIT),
        name="in_proj_sample",
    )(x2d, nm, wqkv, wg, wgate, wba, qg, kg)


def _fold_lanes(s, op):
    out = s[:, :LANES]
    for c0 in range(LANES, s.shape[1], LANES):
        out = op(out, s[:, c0:c0 + LANES])
    return out


MOBA_HEADS = 2
MOBA_TRIP = 4


def _moba_prompt_body(slope_ref, q_ref, qn_ref, kt_ref, vt_ref, o_ref, ka_ref, va_ref, kmean_ref, s_ref,
                      qa_ref, *, seq):
    hg = pl.program_id(1)
    i = pl.program_id(2)
    n_blk = seq // MOBA_BLOCK
    trip = MOBA_TRIP * MOBA_BLOCK
    slopes = [slope_ref[pl.ds(hg * MOBA_HEADS + j, 1), 0:1] for j in range(MOBA_HEADS)]
    nrow = _iota((n_blk, MOBA_BLOCK), 0)
    row8 = _iota((8, MOBA_BLOCK), 0)
    clane = _iota((MOBA_BLOCK, n_blk), 1)
    eye_q = (_iota((MOBA_BLOCK, MOBA_BLOCK), 0) == _iota((MOBA_BLOCK, MOBA_BLOCK), 1)).astype(bf16)

    def augmented_q(q, tile, j):
        tvec = jnp.full((n_blk, MOBA_BLOCK), tile, jnp.int32)
        gt = _dot(kmean_ref[j], q, NT, HI)
        gt = jnp.where(nrow < tvec, gt, -jnp.inf)
        ranks = []
        for g0 in range(0, n_blk, 8):
            grp = gt[g0:g0 + 8, :]
            rank = jnp.zeros((8, MOBA_BLOCK), f32)
            for m in range(n_blk):
                gm = gt[m:m + 1, :]
                if m < g0:
                    beats = gm >= grp
                elif m >= g0 + 8:
                    beats = gm > grp
                else:
                    beats = (gm > grp) | ((gm == grp) & (row8 > m - g0))
                rank = rank + jnp.where(beats, 1.0, 0.0)
            ranks.append(rank)
        sel = (nrow < tvec) & (jnp.concatenate(ranks, axis=0) < MOBA_TOPK)
        pen = _dot(eye_q, jnp.where(sel, 0.0, NEG).astype(bf16), NT)
        shift = -(slopes[j] * MOBA_BLOCK) * tile.astype(f32)
        c = jnp.where(clane < 2, 1.0, jnp.where(clane == 2, shift, 0.0))
        return jnp.concatenate([q * ATT_SCALE, pen, c], axis=-1).astype(bf16)

    @pl.when(i == 0)
    def _():
        row = _iota((HEAD_DIM, seq), 0)
        pos = _iota((HEAD_DIM, seq), 1)
        blk = jnp.right_shift(pos, MOBA_SHIFT)
        jloc = jnp.bitwise_and(pos, MOBA_BLOCK - 1)
        onehot = jnp.where(row == blk, 1.0, 0.0)
        ones_row = jnp.where(row == 0, 1.0, 0.0).astype(bf16)
        for j in range(MOBA_HEADS):
            kt = kt_ref[0, j]
            e = jnp.where(row == n_blk, slopes[j] * jloc.astype(f32), onehot)
            e = jnp.where(row == n_blk + 1, (slopes[j] * MOBA_BLOCK) * blk.astype(f32), e)
            e = jnp.where(row == n_blk + 2, 1.0, e)
            ka_ref[j, 0:HEAD_DIM, :] = kt.astype(bf16)
            ka_ref[j, HEAD_DIM:, :] = e.astype(bf16)
            va_ref[j, 0:HEAD_DIM, :] = vt_ref[0, j].astype(bf16)
            va_ref[j, HEAD_DIM:, :] = ones_row
            cols = [jnp.sum(kt[:, n * MOBA_BLOCK:(n + 1) * MOBA_BLOCK], axis=1, keepdims=True)
                    for n in range(n_blk)]
            kmean_ref[j] = (jnp.concatenate(cols, axis=1) * (1.0 / MOBA_BLOCK)).T
        for j in range(MOBA_HEADS):
            qa_ref[j] = augmented_q(q_ref[0, j], i, j)

    lane = _iota((MOBA_BLOCK, 2 * HEAD_DIM), 1)
    pen_lanes = (lane >= HEAD_DIM) & (lane < HEAD_DIM + n_blk)
    causal = _iota((MOBA_BLOCK, MOBA_BLOCK), 1) <= _iota((MOBA_BLOCK, MOBA_BLOCK), 0)
    d0 = pl.multiple_of(i * MOBA_BLOCK, MOBA_BLOCK)
    qa = [qa_ref[j] for j in range(MOBA_HEADS)]
    s_own = []
    for j in range(MOBA_HEADS):
        qo = jnp.where(pen_lanes, jnp.zeros_like(qa[j]), qa[j])
        s_own.append(jnp.where(causal, _dot(qo, ka_ref[j, :, pl.ds(d0, MOBA_BLOCK)]), NEG))
    n_trip = (i + MOBA_TRIP - 1) // MOBA_TRIP

    def pass1(t, mx):
        r0 = pl.multiple_of(t * trip, trip)
        out = []
        for j in range(MOBA_HEADS):
            s = _dot(qa[j], ka_ref[j, :, pl.ds(r0, trip)])
            s_ref[j, :, pl.ds(r0, trip)] = s
            out.append(jnp.maximum(mx[j], _fold_lanes(s, jnp.maximum)))
        return tuple(out)

    mx = lax.fori_loop(0, n_trip, pass1, tuple(_fold_lanes(s, jnp.maximum) for s in s_own))
    for j in range(MOBA_HEADS):
        qa_ref[j] = augmented_q(qn_ref[0, j], i + 1, j)
    m_b = [jnp.broadcast_to(jnp.max(mx[j], axis=1, keepdims=True), (MOBA_BLOCK, LANES))
           for j in range(MOBA_HEADS)]

    def probs(s, j):
        return jnp.concatenate([jnp.exp(s[:, c0:c0 + LANES] - m_b[j]) for c0 in range(0, s.shape[1], LANES)],
                               axis=1).astype(bf16)

    def pass2(t, acc):
        r0 = pl.multiple_of(t * trip, trip)
        return tuple(acc[j] + _dot(probs(s_ref[j, :, pl.ds(r0, trip)], j), va_ref[j, :, pl.ds(r0, trip)], NT)
                     for j in range(MOBA_HEADS))

    acc = lax.fori_loop(0, n_trip, pass2,
                        tuple(_dot(probs(s_own[j], j), va_ref[j, :, pl.ds(d0, MOBA_BLOCK)], NT)
                              for j in range(MOBA_HEADS)))
    for j in range(MOBA_HEADS):
        o_ref[0, j] = acc[j][:, :HEAD_DIM] / acc[j][:, HEAD_DIM:HEAD_DIM + 1]


def _moba_prompt(q, kt, vt):
    batch, heads, seq, _ = q.shape
    n_blk = seq // MOBA_BLOCK
    assert n_blk % MOBA_TRIP == 0 and 2 * n_blk == HEAD_DIM and MOBA_BLOCK == 1 << MOBA_SHIFT
    assert heads % MOBA_HEADS == 0
    slopes = jnp.asarray(np.repeat(np.asarray(ALIBI_SLOPES, np.float32)[:, None], LANES, axis=1))
    tile = pl.BlockSpec((1, MOBA_HEADS, MOBA_BLOCK, HEAD_DIM), lambda b, h, i: (b, h, i, 0))
    next_tile = pl.BlockSpec((1, MOBA_HEADS, MOBA_BLOCK, HEAD_DIM),
                             lambda b, h, i: (b, h, jnp.minimum(i + 1, n_blk - 1), 0))
    whole = pl.BlockSpec((1, MOBA_HEADS, HEAD_DIM, seq), lambda b, h, i: (b, h, 0, 0))
    return pl.pallas_call(
        functools.partial(_moba_prompt_body, seq=seq),
        grid=(batch, heads // MOBA_HEADS, n_blk),
        in_specs=[pl.BlockSpec((H_ATT, LANES), lambda b, h, i: (0, 0)), tile, next_tile, whole, whole],
        out_specs=tile,
        out_shape=jax.ShapeDtypeStruct(q.shape, f32),
        scratch_shapes=[pltpu.VMEM((MOBA_HEADS, 2 * HEAD_DIM, seq), bf16),
                        pltpu.VMEM((MOBA_HEADS, 2 * HEAD_DIM, seq), bf16),
                        pltpu.VMEM((MOBA_HEADS, n_blk, HEAD_DIM), f32),
                        pltpu.VMEM((MOBA_HEADS, MOBA_BLOCK, seq), f32),
                        pltpu.VMEM((MOBA_HEADS, MOBA_BLOCK, 2 * HEAD_DIM), bf16)],
        compiler_params=pltpu.CompilerParams(dimension_semantics=("arbitrary",) * 3,
                                             vmem_limit_bytes=VMEM_LIMIT),
        name="moba_prompt",
    )(slopes, q, q, kt, vt)


def _moba_sample_body(pt_ref, q_ref, k_ref, v_ref, *refs, past_len, t_new):
    del pt_ref
    n_pages = past_len // PAGE_SIZE
    kp_refs, vp_refs, o_ref = refs[:n_pages], refs[n_pages:2 * n_pages], refs[2 * n_pages]
    own = past_len // MOBA_BLOCK
    ppb = MOBA_BLOCK // PAGE_SIZE
    n_sel = min(MOBA_TOPK, own)
    tq = _iota((t_new, past_len), 0)
    sk = _iota((t_new, past_len), 1)
    dist = (past_len + tq - sk).astype(f32)
    tq2 = _iota((t_new, t_new), 0)
    sk2 = _iota((t_new, t_new), 1)
    hs = range(H_ATT)
    hsl = [slice(h * HEAD_DIM, (h + 1) * HEAD_DIM) for h in hs]
    q = [q_ref[:, hsl[h]] for h in hs]
    qb = [(q[h] * ATT_SCALE).astype(bf16) for h in hs]
    k_blk = lambda h, n: [kp_refs[n * ppb + j][0, h] for j in range(ppb)]
    kmean_t = [jnp.concatenate([sum(jnp.sum(pg, axis=1, keepdims=True) for pg in k_blk(h, n))
                                for n in range(own)], axis=1) * (1.0 / MOBA_BLOCK) for h in hs]
    gate = [_dot(q[h], kmean_t[h], NN, HI) for h in hs]

    def penalty(g, n):
        gn = g[:, n:n + 1]
        rank = jnp.zeros((t_new, 1), f32)
        for m in range(own):
            if m != n:
                gm = g[:, m:m + 1]
                beats = (gm > gn) | ((gm == gn) & (m < n))
                rank = rank + jnp.where(beats, 1.0, 0.0)
        return jnp.where(rank < n_sel, 0.0, NEG)

    s_past = [jnp.concatenate(
        [_dot(qb[h], jnp.concatenate([pg.astype(bf16) for pg in k_blk(h, n)], axis=1)) + penalty(gate[h], n)
         for n in range(own)], axis=1) - ALIBI_SLOPES[h] * dist for h in hs]
    s_own = [jnp.where(sk2 <= tq2,
                       _dot(qb[h], k_ref[:, hsl[h]].astype(bf16), NT) - ALIBI_SLOPES[h] * (tq2 - sk2).astype(f32),
                       NEG) for h in hs]
    m = [jnp.maximum(jnp.max(s_past[h], axis=1, keepdims=True), jnp.max(s_own[h], axis=1, keepdims=True))
         for h in hs]
    p_past = [jnp.exp(s_past[h] - m[h]) for h in hs]
    p_own = [jnp.exp(s_own[h] - m[h]) for h in hs]
    denom = [jnp.sum(p_past[h], axis=1, keepdims=True) + jnp.sum(p_own[h], axis=1, keepdims=True) for h in hs]
    o = [_dot(p_own[h].astype(bf16), v_ref[:, hsl[h]].astype(bf16)) for h in hs]
    for n in range(own):
        blk = slice(n * MOBA_BLOCK, (n + 1) * MOBA_BLOCK)
        o = [o[h] + _dot(p_past[h][:, blk].astype(bf16),
                         jnp.concatenate([vp_refs[n * ppb + j][0, h].astype(bf16) for j in range(ppb)], axis=1), NT)
             for h in hs]
    o_ref[...] = jnp.concatenate([o[h] / denom[h] for h in hs], axis=-1)


def _moba_sample(q, k, v, cache_kt, cache_vt, page_table, *, t_new):
    n_seq, n_pages = page_table.shape
    past_len = n_pages * PAGE_SIZE
    assert past_len % MOBA_BLOCK == 0 and past_len // MOBA_BLOCK >= 1
    rows = pl.BlockSpec((t_new, D_ATT), lambda b, pt: (b, 0))

    def page_spec(p):
        return pl.BlockSpec((1, H_ATT, HEAD_DIM, PAGE_SIZE), lambda b, pt: (pt[b * n_pages + p], 0, 0, 0))

    grid_spec = pltpu.PrefetchScalarGridSpec(
        num_scalar_prefetch=1,
        grid=(n_seq,),
        in_specs=[rows, rows, rows] + [page_spec(p) for p in range(n_pages)] * 2,
        out_specs=rows,
    )
    return pl.pallas_call(
        functools.partial(_moba_sample_body, past_len=past_len, t_new=t_new),
        grid_spec=grid_spec,
        out_shape=jax.ShapeDtypeStruct(q.shape, f32),
        compiler_params=pltpu.CompilerParams(dimension_semantics=("arbitrary",),
                                             vmem_limit_bytes=VMEM_LIMIT),
        name="moba_sample",
    )(page_table.reshape(-1), q, k, v, *([cache_kt] * n_pages), *([cache_vt] * n_pages))


def _conv_silu(cur, prev8, w):
    t = cur.shape[0]
    row8 = _iota((8, cur.shape[1]), 0)
    y = None
    for i in range(CONV_W):
        d = CONV_W - 1 - i
        if d == 0:
            term = cur
        else:
            sh = pltpu.roll(cur, d, 0)
            top = jnp.where(row8 < d, pltpu.roll(prev8, d, 0), sh[0:8])
            term = top if t == 8 else jnp.concatenate([top, sh[8:]], axis=0)
        term = term * w[i:i + 1, :]
        y = term if y is None else y + term
    return _silu(y)


def _gdn_body(uq_ref, uk_ref, uv_ref, pq_ref, pk_ref, pv_ref, cq_ref, ck_ref, cv_ref,
              wq_ref, wk_ref, wv_ref, ba_ref, gate_ref, alog_ref, dtb_ref, gg_ref, m0_ref,
              o_ref, m_ref, *, bb, tb, c):
    t = pl.program_id(1)

    @pl.when(t == 0)
    def _():
        m_ref[...] = m0_ref[...]

    n_chunks = tb // c
    lc = int(np.log2(c))
    assert 1 << lc == c
    ri = _iota((tb, tb), 0)
    ci = _iota((tb, tb), 1)
    same = jnp.right_shift(ri, lc) == jnp.right_shift(ci, lc)
    eye = ri == ci
    keep = same & (ri >= ci)
    cum = jnp.where(keep, 1.0, 0.0)
    tot = jnp.where(same, 1.0, 0.0)
    ident = jnp.where(eye, 1.0, 0.0)
    first = jnp.full((8, D_GDN), t, jnp.int32) == 0

    for s in range(bb):
        q = _conv_silu(uq_ref[s], jnp.where(first, cq_ref[s], pq_ref[s]), wq_ref[...])
        k = _conv_silu(uk_ref[s], jnp.where(first, ck_ref[s], pk_ref[s]), wk_ref[...])
        v = _conv_silu(uv_ref[s], jnp.where(first, cv_ref[s], pv_ref[s]), wv_ref[...])
        ba = ba_ref[s]
        beta_all = jax.nn.sigmoid(ba)
        xa = ba + dtb_ref[...]
        softplus = jnp.maximum(xa, 0.0) + jnp.log1p(jnp.exp(-jnp.abs(xa)))
        g_all = -jnp.exp(alog_ref[...]) * softplus
        gcum_all = _dot(cum, g_all, NN, HI)
        gtot_all = _dot(tot, g_all, NN, HI)
        gcum_t = gcum_all.T if tb % LANES == 0 else None
        hs = range(H_GDN)
        beta = [beta_all[:, h:h + 1] for h in hs]
        gcum = [gcum_all[:, H_GDN + h:H_GDN + h + 1] for h in hs]
        gtot = [gtot_all[:, H_GDN + h:H_GDN + h + 1] for h in hs]
        if gcum_t is None:
            grow = [jnp.sum(jnp.where(eye, gcum[h], 0.0), axis=0, keepdims=True) for h in hs]
        else:
            grow = [gcum_t[H_GDN + h:H_GDN + h + 1, :] for h in hs]
        l2n = lambda z: z * lax.rsqrt(jnp.sum(z * z, axis=-1, keepdims=True) + L2_EPS)
        qh = [l2n(q[:, h * HEAD_DIM:(h + 1) * HEAD_DIM]) * (HEAD_DIM ** -0.5) for h in hs]
        kh = [l2n(k[:, h * HEAD_DIM:(h + 1) * HEAD_DIM]) for h in hs]
        vh = [v[:, h * HEAD_DIM:(h + 1) * HEAD_DIM] for h in hs]
        decay = [jnp.exp(jnp.where(keep, gcum[h] - grow[h], NEG)) for h in hs]
        pw = [beta[h] * _bdot(kh[h], kh[h], NT) * jnp.where(eye, 0.0, decay[h]) for h in hs]
        x = [ident - pw[h] for h in hs]
        for _ in range(lc - 1):
            pw = [_bdot(pw[h], pw[h]) for h in hs]
            x = [x[h] + _bdot(x[h], pw[h]) for h in hs]
        eg = [jnp.exp(gcum[h]) for h in hs]
        sol = [_bdot(x[h], jnp.concatenate([beta[h] * vh[h], (beta[h] * eg[h]) * kh[h]], axis=-1))
               for h in hs]
        aqk = [_bdot(qh[h], kh[h], NT) * decay[h] for h in hs]
        qg = [qh[h] * eg[h] for h in hs]
        kd = [kh[h] * jnp.exp(gtot[h] - gcum[h]) for h in hs]
        gdec = [jnp.exp(gtot[h]) for h in hs]
        m = [m_ref[s, h] for h in hs]
        deltas = [[] for _ in hs]
        oqs = [[] for _ in hs]
        for ch in range(n_chunks):
            sl = slice(ch * c, (ch + 1) * c)
            both = [_bdot(jnp.concatenate([sol[h][sl, HEAD_DIM:], qg[h][sl]], axis=0), m[h]) for h in hs]
            for h in hs:
                deltas[h].append(sol[h][sl, :HEAD_DIM] - both[h][:c])
                oqs[h].append(both[h][c:])
            m = [gdec[h][ch * c:ch * c + 1, :] * m[h] + _dot(kd[h][sl], deltas[h][ch], TN) for h in hs]
        outs = []
        for h in hs:
            m_ref[s, h] = m[h]
            if n_chunks == 1:
                delta_all, oq = deltas[h][0], oqs[h][0]
            else:
                delta_all, oq = jnp.concatenate(deltas[h], axis=0), jnp.concatenate(oqs[h], axis=0)
            outs.append(_rms(oq + _bdot(aqk[h], delta_all)))
        o = jnp.concatenate(outs, axis=-1) * gg_ref[...]
        o_ref[s] = o * _silu(gate_ref[s])


def _gdn(u, conv_pad, conv_w, ba, gate, alog_pad, dtb_pad, gg8, m0, *, bb, tb):
    batch, seq, _ = u.shape
    c = min(GDN_CHUNK, seq)
    assert batch % bb == 0 and seq % tb == 0 and tb % c == 0 and tb % 8 == 0
    cur = lambda g: pl.BlockSpec((bb, tb, D_GDN), lambda b, t: (b, t, g))
    prev = lambda g: pl.BlockSpec((bb, 8, D_GDN), lambda b, t: (b, jnp.maximum(t * (tb // 8) - 1, 0), g))
    cbuf = lambda g: pl.BlockSpec((bb, 8, D_GDN), lambda b, t: (b, 0, g))
    cw = lambda g: pl.BlockSpec((CONV_W, D_GDN), lambda b, t: (0, g))
    row = lambda width: pl.BlockSpec((1, width), lambda b, t: (0, 0))
    state = pl.BlockSpec((bb, H_GDN, HEAD_DIM, HEAD_DIM), lambda b, t: (b, 0, 0, 0))
    return pl.pallas_call(
        functools.partial(_gdn_body, bb=bb, tb=tb, c=c),
        grid=(batch // bb, seq // tb),
        in_specs=[cur(0), cur(1), cur(2), prev(0), prev(1), prev(2), cbuf(0), cbuf(1), cbuf(2),
                  cw(0), cw(1), cw(2),
                  pl.BlockSpec((bb, tb, LANES), lambda b, t: (b, t, 0)),
                  pl.BlockSpec((bb, tb, D_GDN), lambda b, t: (b, t, 0)),
                  row(LANES), row(LANES), row(D_GDN), state],
        out_specs=[pl.BlockSpec((bb, tb, D_GDN), lambda b, t: (b, t, 0)), state],
        out_shape=[jax.ShapeDtypeStruct((batch, seq, D_GDN), f32),
                   jax.ShapeDtypeStruct((batch, H_GDN, HEAD_DIM, HEAD_DIM), f32)],
        compiler_params=pltpu.CompilerParams(dimension_semantics=("arbitrary",) * 2,
                                             vmem_limit_bytes=VMEM_LIMIT),
        name="gdn",
    )(u, u, u, u, u, u, conv_pad, conv_pad, conv_pad, conv_w, conv_w, conv_w,
      ba, gate, alog_pad, dtb_pad, gg8, m0)


FF_CHUNK = 256


def _out_ffn_body(x_ref, oa_ref, og_ref, ag_ref, wo_ref, nf_ref, wg_ref, wu_ref, wd_ref, y_ref):
    oa = jnp.concatenate([oa_ref[0, h] for h in range(H_ATT)], axis=-1)
    oa = _rms(oa) * ag_ref[...]
    mix = jnp.concatenate([oa, og_ref[...]], axis=-1).astype(bf16)
    hid = x_ref[...] + jnp.dot(mix, wo_ref[...], preferred_element_type=f32)
    ub = (_rms(hid) * nf_ref[...]).astype(bf16)
    d_ff = wg_ref.shape[1]
    acc = hid
    for cidx in range(d_ff // FF_CHUNK):
        sl = slice(cidx * FF_CHUNK, (cidx + 1) * FF_CHUNK)
        a = jnp.dot(ub, wg_ref[:, sl], preferred_element_type=f32)
        b = jnp.dot(ub, wu_ref[:, sl], preferred_element_type=f32)
        acc = acc + jnp.dot((_silu(a) * b).astype(bf16), wd_ref[sl, :], preferred_element_type=f32)
    y_ref[...] = acc


def _out_ffn(x2d, o_att, o_gdn, ag, wo, nf, wg, wu, wd, *, tm):
    n = x2d.shape[0]
    batch, _, seq, _ = o_att.shape
    assert n == batch * seq and seq % tm == 0 and wg.shape[1] % FF_CHUNK == 0
    spb = seq // tm
    full = lambda shape: _resident(shape)
    rows = lambda width: pl.BlockSpec((tm, width), lambda i: (i, 0))
    return pl.pallas_call(
        _out_ffn_body,
        grid=(n // tm,),
        in_specs=[rows(D_MODEL),
                  pl.BlockSpec((1, H_ATT, tm, HEAD_DIM), lambda i: (i // spb, 0, i % spb, 0)),
                  rows(D_GDN), full((1, D_ATT)), full(wo.shape), full((1, D_MODEL)),
                  full(wg.shape), full(wu.shape), full(wd.shape)],
        out_specs=rows(D_MODEL),
        out_shape=jax.ShapeDtypeStruct((n, D_MODEL), f32),
        compiler_params=pltpu.CompilerParams(dimension_semantics=("arbitrary",),
                                             vmem_limit_bytes=VMEM_LIMIT),
        name="out_ffn",
    )(x2d, o_att, o_gdn, ag, wo, nf, wg, wu, wd)


def kernel(x_prompt, x_sample, cache_k, cache_v, state_ssm, state_conv, page_table, norm_mix, w_in,
           q_norm_g, k_norm_g, attn_out_g, conv_w, a_log, dt_bias, gdn_out_g, w_out, norm_ffn,
           w_gate, w_up, w_down):
    depth = w_in.shape[0]
    assert depth == 1
    batch, seq, _ = x_prompt.shape
    n_seq, t_new, _ = x_sample.shape

    w = w_in[0]
    c0, c1, c2 = 3 * D_ATT, 3 * D_ATT + 3 * D_GDN, 3 * D_ATT + 4 * D_GDN
    wqkv = w[:, :c0].astype(bf16)
    wq = w[:, :D_ATT].astype(bf16)
    wkvt = w.T[D_ATT:c0].astype(bf16)
    wg = w[:, c0:c1].astype(bf16)
    wgate = w[:, c1:c2].astype(bf16)
    wba = jnp.pad(w[:, c2:], ((0, 0), (0, LANES - 2 * H_GDN))).astype(bf16)
    nm = norm_mix[0].reshape(1, D_MODEL)
    qg = q_norm_g[0].reshape(1, HEAD_DIM)
    kg = k_norm_g[0].reshape(1, HEAD_DIM)
    kgc = k_norm_g[0].reshape(HEAD_DIM, 1)
    ag = attn_out_g[0].reshape(1, D_ATT)
    nf = norm_ffn[0].reshape(1, D_MODEL)
    cw = conv_w[0]
    pad_lo = lambda vec: jnp.pad(vec.astype(f32), (H_GDN, LANES - 2 * H_GDN)).reshape(1, LANES)
    alog_pad = pad_lo(a_log[0])
    dtb_pad = pad_lo(dt_bias[0])
    gg8 = jnp.tile(gdn_out_g[0].astype(f32), H_GDN).reshape(1, D_GDN)
    wo = w_out[0].astype(bf16)
    wgt = w_gate[0].astype(bf16)
    wup = w_up[0].astype(bf16)
    wdn = w_down[0].astype(bf16)

    xp = x_prompt.reshape(batch * seq, D_MODEL)
    qp, ktp, vtp, gp, gatep, bap = _in_proj_prompt(xp, nm, wq, wkvt, wg, wgate, wba, qg, kgc,
                                                   batch=batch, seq=seq, tm=512)
    oap = _moba_prompt(qp, ktp, vtp)
    conv0 = jnp.zeros((batch, 8, 3 * D_GDN), f32)
    ssm0 = jnp.zeros((batch, H_GDN, HEAD_DIM, HEAD_DIM), f32)
    ogp, ssm_p = _gdn(gp.reshape(batch, seq, 3 * D_GDN), conv0, cw, bap.reshape(batch, seq, LANES),
                      gatep.reshape(batch, seq, D_GDN), alog_pad, dtb_pad, gg8, ssm0, bb=1, tb=256)
    yp = _out_ffn(xp, oap, ogp.reshape(batch * seq, D_GDN), ag, wo, nf, wgt, wup, wdn, tm=512)

    ns = n_seq * t_new
    xs = x_sample.reshape(ns, D_MODEL)
    qs, ks, vs, gs, gates, bas = _in_proj_sample(xs, nm, wqkv, wg, wgate, wba, qg, kg, tm=256)
    cache_kt = jnp.swapaxes(cache_k[0], -1, -2)
    cache_vt = jnp.swapaxes(cache_v[0], -1, -2)
    oas = _moba_sample(qs, ks, vs, cache_kt, cache_vt, page_table, t_new=t_new)
    conv_s = jnp.pad(state_conv[0], ((0, 0), (8 - (CONV_W - 1), 0), (0, 0)))
    ogs, ssm_s = _gdn(gs.reshape(n_seq, t_new, 3 * D_GDN), conv_s, cw, bas.reshape(n_seq, t_new, LANES),
                      gates.reshape(n_seq, t_new, D_GDN), alog_pad, dtb_pad, gg8, state_ssm[0],
                      bb=2, tb=t_new)
    to_heads = lambda a: a.reshape(n_seq, t_new, H_ATT, HEAD_DIM).transpose(0, 2, 1, 3)
    oas_hm = oas.reshape(1, ns, H_ATT, HEAD_DIM).transpose(0, 2, 1, 3)
    ys = _out_ffn(xs, oas_hm, ogs.reshape(ns, D_GDN), ag, wo, nf, wgt, wup, wdn, tm=512)

    gp3 = gp.reshape(batch, seq, 3 * D_GDN)
    gs3 = gs.reshape(n_seq, t_new, 3 * D_GDN)
    return (yp.reshape(batch, seq, D_MODEL), ys.reshape(n_seq, t_new, D_MODEL),
            jnp.swapaxes(ktp, -1, -2)[None], jnp.swapaxes(vtp, -1, -2)[None],
            to_heads(ks)[None], to_heads(vs)[None],
            ssm_p[None], ssm_s[None],
            gp3[:, seq - (CONV_W - 1):][None], gs3[:, t_new - (CONV_W - 1):][None])
```

```python
import functools

import numpy as np
import jax
import jax.numpy as jnp
from jax import lax
from jax.experimental import pallas as pl
from jax.experimental.pallas import tpu as pltpu

f32 = jnp.float32
bf16 = jnp.bfloat16

D_MODEL = 1024
HEAD_DIM = 64
H_ATT = 8
H_GDN = 8
D_ATT = H_ATT * HEAD_DIM
D_GDN = H_GDN * HEAD_DIM
MOBA_BLOCK = 256
MOBA_SHIFT = 8
MOBA_TOPK = 3
GDN_CHUNK = 64
CONV_W = 4
PAGE_SIZE = 128
RMS_EPS = 1e-6
L2_EPS = 1e-6
ATT_SCALE = HEAD_DIM ** -0.5
LOG2E = float(np.log2(np.e))
NEG = -1e30
LANES = 128
VMEM_LIMIT = 56 * 1024 * 1024

HI = lax.Precision.HIGHEST
NN = (((1,), (0,)), ((), ()))
NT = (((1,), (1,)), ((), ()))
TN = (((0,), (0,)), ((), ()))

ALIBI_SLOPES = [2.0 ** (-8.0 * (i + 1) / H_ATT) for i in range(H_ATT)]


def _dot(a, b, dims=NN, precision=None):
    return lax.dot_general(a, b, dims, precision=precision, preferred_element_type=f32)


def _bdot(a, b, dims=NN):
    return lax.dot_general(a.astype(bf16), b.astype(bf16), dims, preferred_element_type=f32)


def _rms(x, axis=-1):
    return x * lax.rsqrt(jnp.mean(x * x, axis=axis, keepdims=True) + RMS_EPS)


def _silu(x):
    return x * jax.nn.sigmoid(x)


def _iota(shape, dim):
    return lax.broadcasted_iota(jnp.int32, shape, dim)


def _resident(shape):
    return pl.BlockSpec(shape, lambda i: (0,) * len(shape), pipeline_mode=pl.Buffered(1))


def _in_proj_prompt_body(x_ref, nm_ref, wqkvt_ref, wg_ref, wgate_ref, wba_ref, qgc_ref, kgc_ref,
                         qt_ref, kt_ref, vt_ref, g_ref, gate_ref, ba_ref):
    x = x_ref[...]
    xb = (_rms(x) * nm_ref[...]).astype(bf16)
    z = _dot(wqkvt_ref[...], xb, NT)
    for h in range(H_ATT):
        lo, hi = h * HEAD_DIM, (h + 1) * HEAD_DIM
        qt_ref[0, h] = _rms(z[lo:hi, :], axis=0) * qgc_ref[...]
        kt_ref[0, h] = _rms(z[D_ATT + lo:D_ATT + hi, :], axis=0) * kgc_ref[...]
        vt_ref[0, h] = z[2 * D_ATT + lo:2 * D_ATT + hi, :]
    g_ref[...] = jnp.dot(xb, wg_ref[...], preferred_element_type=f32)
    gate_ref[...] = jnp.dot(xb, wgate_ref[...], preferred_element_type=f32)
    ba_ref[...] = jnp.dot(xb, wba_ref[...], preferred_element_type=f32)


def _in_proj_prompt(x2d, nm, wqkvt, wg, wgate, wba, qgc, kgc, *, batch, seq, tm):
    n = x2d.shape[0]
    assert n == batch * seq and seq % tm == 0
    spb = seq // tm
    full = lambda shape: _resident(shape)
    rows = lambda width: pl.BlockSpec((tm, width), lambda i: (i, 0))
    t_spec = pl.BlockSpec((1, H_ATT, HEAD_DIM, tm), lambda i: (i // spb, 0, 0, i % spb))
    t_shape = jax.ShapeDtypeStruct((batch, H_ATT, HEAD_DIM, seq), f32)
    return pl.pallas_call(
        _in_proj_prompt_body,
        grid=(n // tm,),
        in_specs=[rows(D_MODEL), full((1, D_MODEL)), full(wqkvt.shape), full(wg.shape),
                  full(wgate.shape), full(wba.shape), full((HEAD_DIM, 1)), full((HEAD_DIM, 1))],
        out_specs=[t_spec, t_spec, t_spec, rows(3 * D_GDN), rows(D_GDN), rows(LANES)],
        out_shape=[t_shape, t_shape, t_shape,
                   jax.ShapeDtypeStruct((n, 3 * D_GDN), f32),
                   jax.ShapeDtypeStruct((n, D_GDN), f32),
                   jax.ShapeDtypeStruct((n, LANES), f32)],
        compiler_params=pltpu.CompilerParams(dimension_semantics=("arbitrary",),
                                             vmem_limit_bytes=VMEM_LIMIT),
        name="in_proj_prompt",
    )(x2d, nm, wqkvt, wg, wgate, wba, qgc, kgc)


def _in_proj_sample_body(x_ref, nm_ref, wqkv_ref, wg_ref, wgate_ref, wba_ref, qg_ref, kg_ref,
                         q_ref, k_ref, v_ref, g_ref, gate_ref, ba_ref):
    x = x_ref[...]
    xb = (_rms(x) * nm_ref[...]).astype(bf16)
    z = jnp.dot(xb, wqkv_ref[...], preferred_element_type=f32)
    qs, ks = [], []
    for h in range(H_ATT):
        lo, hi = h * HEAD_DIM, (h + 1) * HEAD_DIM
        qs.append(_rms(z[:, lo:hi]) * qg_ref[...])
        ks.append(_rms(z[:, D_ATT + lo:D_ATT + hi]) * kg_ref[...])
    q_ref[...] = jnp.concatenate(qs, axis=-1)
    k_ref[...] = jnp.concatenate(ks, axis=-1)
    v_ref[...] = z[:, 2 * D_ATT:]
    g_ref[...] = jnp.dot(xb, wg_ref[...], preferred_element_type=f32)
    gate_ref[...] = jnp.dot(xb, wgate_ref[...], preferred_element_type=f32)
    ba_ref[...] = jnp.dot(xb, wba_ref[...], preferred_element_type=f32)


def _in_proj_sample(x2d, nm, wqkv, wg, wgate, wba, qg, kg, *, tm):
    n = x2d.shape[0]
    assert n % tm == 0
    full = lambda shape: _resident(shape)
    rows = lambda width: pl.BlockSpec((tm, width), lambda i: (i, 0))
    tok = jax.ShapeDtypeStruct((n, D_ATT), f32)
    return pl.pallas_call(
        _in_proj_sample_body,
        grid=(n // tm,),
        in_specs=[rows(D_MODEL), full((1, D_MODEL)), full(wqkv.shape), full(wg.shape),
                  full(wgate.shape), full(wba.shape), full((1, HEAD_DIM)), full((1, HEAD_DIM))],
        out_specs=[rows(D_ATT), rows(D_ATT), rows(D_ATT), rows(3 * D_GDN), rows(D_GDN), rows(LANES)],
        out_shape=[tok, tok, tok,
                   jax.ShapeDtypeStruct((n, 3 * D_GDN), f32),
                   jax.ShapeDtypeStruct((n, D_GDN), f32),
                   jax.ShapeDtypeStruct((n, LANES), f32)],
        compiler_params=pltpu.CompilerParams(dimension_semantics=("arbitrary",),
                                             vmem_limit_bytes=VMEM_LIMIT),
        name="in_proj_sample",
    )(x2d, nm, wqkv, wg, wgate, wba, qg, kg)


def _fold_rows(s, op):
    parts = [s[r0:r0 + 8] for r0 in range(0, s.shape[0], 8)]
    while len(parts) > 1:
        parts = [op(parts[k], parts[k + 1]) for k in range(0, len(parts), 2)]
    return parts[0]


MOBA_HEADS = 2
MOBA_TRIP = 8


def _moba_prompt_body(slope_ref, qt_ref, qtn_ref, kt_ref, vt_ref, o_ref, ka_ref, va_ref, kmean_ref, s_ref,
                      qa_ref, *, seq):
    hg = pl.program_id(1)
    i = pl.program_id(2)
    n_blk = seq // MOBA_BLOCK
    trip = MOBA_TRIP * MOBA_BLOCK
    slopes = [slope_ref[pl.ds(hg * MOBA_HEADS + j, 1), 0:1] for j in range(MOBA_HEADS)]
    nrow = _iota((n_blk, MOBA_BLOCK), 0)
    row8 = _iota((8, MOBA_BLOCK), 0)

    def augmented_qt(qt, tile, j):
        tvec = jnp.full((n_blk, MOBA_BLOCK), tile, jnp.int32)
        gt = _dot(kmean_ref[j], qt, NN, HI)
        gt = jnp.where(nrow < tvec, gt, -jnp.inf)
        ranks = []
        for g0 in range(0, n_blk, 8):
            grp = gt[g0:g0 + 8, :]
            rank = jnp.zeros((8, MOBA_BLOCK), f32)
            for m in range(n_blk):
                gm = gt[m:m + 1, :]
                if m < g0:
                    beats = gm >= grp
                elif m >= g0 + 8:
                    beats = gm > grp
                else:
                    beats = (gm > grp) | ((gm == grp) & (row8 > m - g0))
                rank = rank + jnp.where(beats, 1.0, 0.0)
            ranks.append(rank)
        sel = (nrow < tvec) & (jnp.concatenate(ranks, axis=0) < MOBA_TOPK)
        shift = -(slopes[j] * MOBA_BLOCK) * tile.astype(f32)
        c_t = jnp.where(nrow < 2, 1.0, jnp.where(nrow == 2, shift, 0.0))
        return jnp.concatenate([qt * ATT_SCALE, jnp.where(sel, 0.0, NEG), c_t], axis=0).astype(bf16)

    @pl.when(i == 0)
    def _():
        pos = _iota((seq, HEAD_DIM), 0)
        col = _iota((seq, HEAD_DIM), 1)
        blk = jnp.right_shift(pos, MOBA_SHIFT)
        jloc = jnp.bitwise_and(pos, MOBA_BLOCK - 1)
        onehot = jnp.where(col == blk, 1.0, 0.0)
        ones_row = jnp.where(_iota((HEAD_DIM, seq), 0) == 0, 1.0, 0.0).astype(bf16)
        for j in range(MOBA_HEADS):
            k_tok = kt_ref[0, j].T
            e = jnp.where(col == n_blk, slopes[j] * jloc.astype(f32), onehot)
            e = jnp.where(col == n_blk + 1, (slopes[j] * MOBA_BLOCK) * blk.astype(f32), e)
            e = jnp.where(col == n_blk + 2, 1.0, e)
            ka_ref[j] = jnp.concatenate([k_tok.astype(bf16), e.astype(bf16)], axis=-1)
            va_ref[j, 0:HEAD_DIM, :] = vt_ref[0, j].astype(bf16)
            va_ref[j, HEAD_DIM:, :] = ones_row
            kmean_ref[j] = k_tok.reshape(n_blk, MOBA_BLOCK, HEAD_DIM).sum(axis=1) * (1.0 / MOBA_BLOCK)
        for j in range(MOBA_HEADS):
            qa_ref[j] = augmented_qt(qt_ref[0, j], i, j)

    arow = _iota((2 * HEAD_DIM, MOBA_BLOCK), 0)
    pen_rows = (arow >= HEAD_DIM) & (arow < HEAD_DIM + n_blk)
    causal = _iota((MOBA_BLOCK, MOBA_BLOCK), 0) <= _iota((MOBA_BLOCK, MOBA_BLOCK), 1)
    d0 = pl.multiple_of(i * MOBA_BLOCK, MOBA_BLOCK)
    qa = [qa_ref[j] for j in range(MOBA_HEADS)]
    s_own = []
    for j in range(MOBA_HEADS):
        qo = jnp.where(pen_rows, jnp.zeros_like(qa[j]), qa[j])
        s_own.append(jnp.where(causal, _dot(ka_ref[j, pl.ds(d0, MOBA_BLOCK), :], qo) * LOG2E, NEG))
    n_trip = (i + MOBA_TRIP - 1) // MOBA_TRIP

    def pass1(t, mx):
        r0 = pl.multiple_of(t * trip, trip)
        out = []
        for j in range(MOBA_HEADS):
            s = _dot(ka_ref[j, pl.ds(r0, trip), :], qa[j]) * LOG2E
            s_ref[j, pl.ds(r0, trip), :] = s
            out.append(jnp.maximum(mx[j], _fold_rows(s, jnp.maximum)))
        return tuple(out)

    mx = lax.fori_loop(0, n_trip, pass1, tuple(_fold_rows(s, jnp.maximum) for s in s_own))
    for j in range(MOBA_HEADS):
        qa_ref[j] = augmented_qt(qtn_ref[0, j], i + 1, j)
    m_row = [jnp.max(mx[j], axis=0, keepdims=True) for j in range(MOBA_HEADS)]

    def pass2(t, acc):
        r0 = pl.multiple_of(t * trip, trip)
        return tuple(acc[j] + _dot(va_ref[j, :, pl.ds(r0, trip)],
                                   jnp.exp2((s_ref[j, pl.ds(r0, trip), :] - m_row[j]).astype(bf16)))
                     for j in range(MOBA_HEADS))

    acc = lax.fori_loop(0, n_trip, pass2,
                        tuple(_dot(va_ref[j, :, pl.ds(d0, MOBA_BLOCK)], jnp.exp2((s_own[j] - m_row[j]).astype(bf16)))
                              for j in range(MOBA_HEADS)))
    for j in range(MOBA_HEADS):
        o_ref[0, j] = (acc[j][:HEAD_DIM] / acc[j][HEAD_DIM:HEAD_DIM + 1]).T


def _moba_prompt(qt, kt, vt):
    batch, heads, _, seq = qt.shape
    n_blk = seq // MOBA_BLOCK
    assert n_blk % MOBA_TRIP == 0 and 2 * n_blk == HEAD_DIM and MOBA_BLOCK == 1 << MOBA_SHIFT
    assert heads % MOBA_HEADS == 0
    slopes = jnp.asarray(np.repeat(np.asarray(ALIBI_SLOPES, np.float32)[:, None], LANES, axis=1))
    tile = pl.BlockSpec((1, MOBA_HEADS, HEAD_DIM, MOBA_BLOCK), lambda b, h, i: (b, h, 0, i))
    next_tile = pl.BlockSpec((1, MOBA_HEADS, HEAD_DIM, MOBA_BLOCK),
                             lambda b, h, i: (b, h, 0, jnp.minimum(i + 1, n_blk - 1)))
    whole = pl.BlockSpec((1, MOBA_HEADS, HEAD_DIM, seq), lambda b, h, i: (b, h, 0, 0),
                         pipeline_mode=pl.Buffered(1))
    return pl.pallas_call(
        functools.partial(_moba_prompt_body, seq=seq),
        grid=(batch, heads // MOBA_HEADS, n_blk),
        in_specs=[pl.BlockSpec((H_ATT, LANES), lambda b, h, i: (0, 0)), tile, next_tile, whole, whole],
        out_specs=pl.BlockSpec((1, MOBA_HEADS, MOBA_BLOCK, HEAD_DIM), lambda b, h, i: (b, h, i, 0)),
        out_shape=jax.ShapeDtypeStruct((batch, heads, seq, HEAD_DIM), f32),
        scratch_shapes=[pltpu.VMEM((MOBA_HEADS, seq, 2 * HEAD_DIM), bf16),
                        pltpu.VMEM((MOBA_HEADS, 2 * HEAD_DIM, seq), bf16),
                        pltpu.VMEM((MOBA_HEADS, n_blk, HEAD_DIM), f32),
                        pltpu.VMEM((MOBA_HEADS, seq, MOBA_BLOCK), f32),
                        pltpu.VMEM((MOBA_HEADS, 2 * HEAD_DIM, MOBA_BLOCK), bf16)],
        compiler_params=pltpu.CompilerParams(dimension_semantics=("arbitrary",) * 3,
                                             vmem_limit_bytes=VMEM_LIMIT),
        name="moba_prompt",
    )(slopes, qt, qt, kt, vt)


def _moba_sample_body(pt_ref, q_ref, k_ref, v_ref, *refs, past_len, t_new):
    del pt_ref
    n_pages = past_len // PAGE_SIZE
    kp_refs, vp_refs, o_ref = refs[:n_pages], refs[n_pages:2 * n_pages], refs[2 * n_pages]
    own = past_len // MOBA_BLOCK
    ppb = MOBA_BLOCK // PAGE_SIZE
    n_sel = min(MOBA_TOPK, own)
    tq = _iota((t_new, past_len), 0)
    sk = _iota((t_new, past_len), 1)
    dist = (past_len + tq - sk).astype(f32)
    tq2 = _iota((t_new, t_new), 0)
    sk2 = _iota((t_new, t_new), 1)
    hs = range(H_ATT)
    hsl = [slice(h * HEAD_DIM, (h + 1) * HEAD_DIM) for h in hs]
    q = [q_ref[:, hsl[h]] for h in hs]
    qb = [(q[h] * ATT_SCALE).astype(bf16) for h in hs]
    k_blk = lambda h, n: [kp_refs[n * ppb + j][0, h] for j in range(ppb)]
    kmean_t = [jnp.concatenate([sum(jnp.sum(pg, axis=1, keepdims=True) for pg in k_blk(h, n))
                                for n in range(own)], axis=1) * (1.0 / MOBA_BLOCK) for h in hs]
    gate = [_dot(q[h], kmean_t[h], NN, HI) for h in hs]

    def penalty(g, n):
        gn = g[:, n:n + 1]
        rank = jnp.zeros((t_new, 1), f32)
        for m in range(own):
            if m != n:
                gm = g[:, m:m + 1]
                beats = (gm > gn) | ((gm == gn) & (m < n))
                rank = rank + jnp.where(beats, 1.0, 0.0)
        return jnp.where(rank < n_sel, 0.0, NEG)

    s_past = [jnp.concatenate(
        [_dot(qb[h], jnp.concatenate([pg.astype(bf16) for pg in k_blk(h, n)], axis=1)) + penalty(gate[h], n)
         for n in range(own)], axis=1) - ALIBI_SLOPES[h] * dist for h in hs]
    s_own = [jnp.where(sk2 <= tq2,
                       _dot(qb[h], k_ref[:, hsl[h]].astype(bf16), NT) - ALIBI_SLOPES[h] * (tq2 - sk2).astype(f32),
                       NEG) for h in hs]
    m = [jnp.maximum(jnp.max(s_past[h], axis=1, keepdims=True), jnp.max(s_own[h], axis=1, keepdims=True))
         for h in hs]
    p_past = [jnp.exp(s_past[h] - m[h]) for h in hs]
    p_own = [jnp.exp(s_own[h] - m[h]) for h in hs]
    denom = [jnp.sum(p_past[h], axis=1, keepdims=True) + jnp.sum(p_own[h], axis=1, keepdims=True) for h in hs]
    o = [_dot(p_own[h].astype(bf16), v_ref[:, hsl[h]].astype(bf16)) for h in hs]
    for n in range(own):
        blk = slice(n * MOBA_BLOCK, (n + 1) * MOBA_BLOCK)
        o = [o[h] + _dot(p_past[h][:, blk].astype(bf16),
                         jnp.concatenate([vp_refs[n * ppb + j][0, h].astype(bf16) for j in range(ppb)], axis=1), NT)
             for h in hs]
    o_ref[...] = jnp.concatenate([o[h] / denom[h] for h in hs], axis=-1)


def _moba_sample(q, k, v, cache_kt, cache_vt, page_table, *, t_new):
    n_seq, n_pages = page_table.shape
    past_len = n_pages * PAGE_SIZE
    assert past_len % MOBA_BLOCK == 0 and past_len // MOBA_BLOCK >= 1
    rows = pl.BlockSpec((t_new, D_ATT), lambda b, pt: (b, 0))

    def page_spec(p):
        return pl.BlockSpec((1, H_ATT, HEAD_DIM, PAGE_SIZE), lambda b, pt: (pt[b * n_pages + p], 0, 0, 0))

    grid_spec = pltpu.PrefetchScalarGridSpec(
        num_scalar_prefetch=1,
        grid=(n_seq,),
        in_specs=[rows, rows, rows] + [page_spec(p) for p in range(n_pages)] * 2,
        out_specs=rows,
    )
    return pl.pallas_call(
        functools.partial(_moba_sample_body, past_len=past_len, t_new=t_new),
        grid_spec=grid_spec,
        out_shape=jax.ShapeDtypeStruct(q.shape, f32),
        compiler_params=pltpu.CompilerParams(dimension_semantics=("arbitrary",),
                                             vmem_limit_bytes=VMEM_LIMIT),
        name="moba_sample",
    )(page_table.reshape(-1), q, k, v, *([cache_kt] * n_pages), *([cache_vt] * n_pages))


def _conv_silu(cur, prev8, w):
    t = cur.shape[0]
    row8 = _iota((8, cur.shape[1]), 0)
    y = None
    for i in range(CONV_W):
        d = CONV_W - 1 - i
        if d == 0:
            term = cur
        else:
            sh = pltpu.roll(cur, d, 0)
            top = jnp.where(row8 < d, pltpu.roll(prev8, d, 0), sh[0:8])
            term = top if t == 8 else jnp.concatenate([top, sh[8:]], axis=0)
        term = term * w[i:i + 1, :]
        y = term if y is None else y + term
    return _silu(y)


def _gdn_body(uq_ref, uk_ref, uv_ref, pq_ref, pk_ref, pv_ref, cq_ref, ck_ref, cv_ref,
              wq_ref, wk_ref, wv_ref, ba_ref, gate_ref, alog_ref, dtb_ref, gg_ref, m0_ref,
              o_ref, m_ref, *, bb, tb, c):
    t = pl.program_id(1)

    @pl.when(t == 0)
    def _():
        m_ref[...] = m0_ref[...]

    n_chunks = tb // c
    lc = int(np.log2(c))
    assert 1 << lc == c
    ri = _iota((tb, tb), 0)
    ci = _iota((tb, tb), 1)
    same = jnp.right_shift(ri, lc) == jnp.right_shift(ci, lc)
    eye = ri == ci
    keep = same & (ri >= ci)
    cum = jnp.where(keep, 1.0, 0.0)
    tot = jnp.where(same, 1.0, 0.0)
    ident = jnp.where(eye, 1.0, 0.0)
    first = jnp.full((8, D_GDN), t, jnp.int32) == 0

    for s in range(bb):
        q = _conv_silu(uq_ref[s], jnp.where(first, cq_ref[s], pq_ref[s]), wq_ref[...])
        k = _conv_silu(uk_ref[s], jnp.where(first, ck_ref[s], pk_ref[s]), wk_ref[...])
        v = _conv_silu(uv_ref[s], jnp.where(first, cv_ref[s], pv_ref[s]), wv_ref[...])
        ba = ba_ref[s]
        beta_all = jax.nn.sigmoid(ba)
        xa = ba + dtb_ref[...]
        softplus = jnp.maximum(xa, 0.0) + jnp.log1p(jnp.exp(-jnp.abs(xa)))
        g_all = -jnp.exp(alog_ref[...]) * softplus
        gcum_all = _dot(cum, g_all, NN, HI)
        gtot_all = _dot(tot, g_all, NN, HI)
        gcum_t = gcum_all.T if tb % LANES == 0 else None
        hs = range(H_GDN)
        beta = [beta_all[:, h:h + 1] for h in hs]
        gcum = [gcum_all[:, H_GDN + h:H_GDN + h + 1] for h in hs]
        gtot = [gtot_all[:, H_GDN + h:H_GDN + h + 1] for h in hs]
        if gcum_t is None:
            grow = [jnp.sum(jnp.where(eye, gcum[h], 0.0), axis=0, keepdims=True) for h in hs]
        else:
            grow = [gcum_t[H_GDN + h:H_GDN + h + 1, :] for h in hs]
        l2n = lambda z: z * lax.rsqrt(jnp.sum(z * z, axis=-1, keepdims=True) + L2_EPS)
        qh = [l2n(q[:, h * HEAD_DIM:(h + 1) * HEAD_DIM]) * (HEAD_DIM ** -0.5) for h in hs]
        kh = [l2n(k[:, h * HEAD_DIM:(h + 1) * HEAD_DIM]) for h in hs]
        vh = [v[:, h * HEAD_DIM:(h + 1) * HEAD_DIM] for h in hs]
        decay = [jnp.exp(jnp.where(keep, gcum[h] - grow[h], NEG)) for h in hs]
        pw = [beta[h] * _bdot(kh[h], kh[h], NT) * jnp.where(eye, 0.0, decay[h]) for h in hs]
        x = [ident - pw[h] for h in hs]
        for _ in range(lc - 1):
            pw = [_bdot(pw[h], pw[h]) for h in hs]
            x = [x[h] + _bdot(x[h], pw[h]) for h in hs]
        eg = [jnp.exp(gcum[h]) for h in hs]
        sol = [_bdot(x[h], jnp.concatenate([beta[h] * vh[h], (beta[h] * eg[h]) * kh[h]], axis=-1))
               for h in hs]
        aqk = [_bdot(qh[h], kh[h], NT) * decay[h] for h in hs]
        qg = [qh[h] * eg[h] for h in hs]
        kd = [kh[h] * jnp.exp(gtot[h] - gcum[h]) for h in hs]
        gdec = [jnp.exp(gtot[h]) for h in hs]
        m = [m_ref[s, h] for h in hs]
        deltas = [[] for _ in hs]
        oqs = [[] for _ in hs]
        for ch in range(n_chunks):
            sl = slice(ch * c, (ch + 1) * c)
            both = [_bdot(jnp.concatenate([sol[h][sl, HEAD_DIM:], qg[h][sl]], axis=0), m[h]) for h in hs]
            for h in hs:
                deltas[h].append(sol[h][sl, :HEAD_DIM] - both[h][:c])
                oqs[h].append(both[h][c:])
            m = [gdec[h][ch * c:ch * c + 1, :] * m[h] + _dot(kd[h][sl], deltas[h][ch], TN) for h in hs]
        outs = []
        for h in hs:
            m_ref[s, h] = m[h]
            if n_chunks == 1:
                delta_all, oq = deltas[h][0], oqs[h][0]
            else:
                delta_all, oq = jnp.concatenate(deltas[h], axis=0), jnp.concatenate(oqs[h], axis=0)
            outs.append(_rms(oq + _bdot(aqk[h], delta_all)))
        o = jnp.concatenate(outs, axis=-1) * gg_ref[...]
        o_ref[s] = o * _silu(gate_ref[s])


def _gdn(u, conv_pad, conv_w, ba, gate, alog_pad, dtb_pad, gg8, m0, *, bb, tb):
    batch, seq, _ = u.shape
    c = min(GDN_CHUNK, seq)
    assert batch % bb == 0 and seq % tb == 0 and tb % c == 0 and tb % 8 == 0
    cur = lambda g: pl.BlockSpec((bb, tb, D_GDN), lambda b, t: (b, t, g))
    prev = lambda g: pl.BlockSpec((bb, 8, D_GDN), lambda b, t: (b, jnp.maximum(t * (tb // 8) - 1, 0), g))
    cbuf = lambda g: pl.BlockSpec((bb, 8, D_GDN), lambda b, t: (b, 0, g))
    cw = lambda g: pl.BlockSpec((CONV_W, D_GDN), lambda b, t: (0, g))
    row = lambda width: pl.BlockSpec((1, width), lambda b, t: (0, 0))
    state = pl.BlockSpec((bb, H_GDN, HEAD_DIM, HEAD_DIM), lambda b, t: (b, 0, 0, 0))
    return pl.pallas_call(
        functools.partial(_gdn_body, bb=bb, tb=tb, c=c),
        grid=(batch // bb, seq // tb),
        in_specs=[cur(0), cur(1), cur(2), prev(0), prev(1), prev(2), cbuf(0), cbuf(1), cbuf(2),
                  cw(0), cw(1), cw(2),
                  pl.BlockSpec((bb, tb, LANES), lambda b, t: (b, t, 0)),
                  pl.BlockSpec((bb, tb, D_GDN), lambda b, t: (b, t, 0)),
                  row(LANES), row(LANES), row(D_GDN), state],
        out_specs=[pl.BlockSpec((bb, tb, D_GDN), lambda b, t: (b, t, 0)), state],
        out_shape=[jax.ShapeDtypeStruct((batch, seq, D_GDN), f32),
                   jax.ShapeDtypeStruct((batch, H_GDN, HEAD_DIM, HEAD_DIM), f32)],
        compiler_params=pltpu.CompilerParams(dimension_semantics=("arbitrary",) * 2,
                                             vmem_limit_bytes=VMEM_LIMIT),
        name="gdn",
    )(u, u, u, u, u, u, conv_pad, conv_pad, conv_pad, conv_w, conv_w, conv_w,
      ba, gate, alog_pad, dtb_pad, gg8, m0)


FF_CHUNK = 256


def _out_ffn_body(x_ref, oa_ref, og_ref, ag_ref, wo_ref, nf_ref, wg_ref, wu_ref, wd_ref, y_ref):
    oa = jnp.concatenate([oa_ref[0, h] for h in range(H_ATT)], axis=-1)
    oa = _rms(oa) * ag_ref[...]
    mix = jnp.concatenate([oa, og_ref[...]], axis=-1).astype(bf16)
    hid = x_ref[...] + jnp.dot(mix, wo_ref[...], preferred_element_type=f32)
    ub = (_rms(hid) * nf_ref[...]).astype(bf16)
    d_ff = wg_ref.shape[1]
    acc = hid
    for cidx in range(d_ff // FF_CHUNK):
        sl = slice(cidx * FF_CHUNK, (cidx + 1) * FF_CHUNK)
        a = jnp.dot(ub, wg_ref[:, sl], preferred_element_type=f32)
        b = jnp.dot(ub, wu_ref[:, sl], preferred_element_type=f32)
        acc = acc + jnp.dot((_silu(a) * b).astype(bf16), wd_ref[sl, :], preferred_element_type=f32)
    y_ref[...] = acc


def _out_ffn(x2d, o_att, o_gdn, ag, wo, nf, wg, wu, wd, *, tm):
    n = x2d.shape[0]
    batch, _, seq, _ = o_att.shape
    assert n == batch * seq and seq % tm == 0 and wg.shape[1] % FF_CHUNK == 0
    spb = seq // tm
    full = lambda shape: _resident(shape)
    rows = lambda width: pl.BlockSpec((tm, width), lambda i: (i, 0))
    return pl.pallas_call(
        _out_ffn_body,
        grid=(n // tm,),
        in_specs=[rows(D_MODEL),
                  pl.BlockSpec((1, H_ATT, tm, HEAD_DIM), lambda i: (i // spb, 0, i % spb, 0)),
                  rows(D_GDN), full((1, D_ATT)), full(wo.shape), full((1, D_MODEL)),
                  full(wg.shape), full(wu.shape), full(wd.shape)],
        out_specs=rows(D_MODEL),
        out_shape=jax.ShapeDtypeStruct((n, D_MODEL), f32),
        compiler_params=pltpu.CompilerParams(dimension_semantics=("arbitrary",),
                                             vmem_limit_bytes=VMEM_LIMIT),
        name="out_ffn",
    )(x2d, o_att, o_gdn, ag, wo, nf, wg, wu, wd)


def kernel(x_prompt, x_sample, cache_k, cache_v, state_ssm, state_conv, page_table, norm_mix, w_in,
           q_norm_g, k_norm_g, attn_out_g, conv_w, a_log, dt_bias, gdn_out_g, w_out, norm_ffn,
           w_gate, w_up, w_down):
    depth = w_in.shape[0]
    assert depth == 1
    batch, seq, _ = x_prompt.shape
    n_seq, t_new, _ = x_sample.shape

    w = w_in[0]
    c0, c1, c2 = 3 * D_ATT, 3 * D_ATT + 3 * D_GDN, 3 * D_ATT + 4 * D_GDN
    wqkv = w[:, :c0].astype(bf16)
    wqkvt = w.T[:c0].astype(bf16)
    wg = w[:, c0:c1].astype(bf16)
    wgate = w[:, c1:c2].astype(bf16)
    wba = jnp.pad(w[:, c2:], ((0, 0), (0, LANES - 2 * H_GDN))).astype(bf16)
    nm = norm_mix[0].reshape(1, D_MODEL)
    qg = q_norm_g[0].reshape(1, HEAD_DIM)
    qgc = q_norm_g[0].reshape(HEAD_DIM, 1)
    kg = k_norm_g[0].reshape(1, HEAD_DIM)
    kgc = k_norm_g[0].reshape(HEAD_DIM, 1)
    ag = attn_out_g[0].reshape(1, D_ATT)
    nf = norm_ffn[0].reshape(1, D_MODEL)
    cw = conv_w[0]
    pad_lo = lambda vec: jnp.pad(vec.astype(f32), (H_GDN, LANES - 2 * H_GDN)).reshape(1, LANES)
    alog_pad = pad_lo(a_log[0])
    dtb_pad = pad_lo(dt_bias[0])
    gg8 = jnp.tile(gdn_out_g[0].astype(f32), H_GDN).reshape(1, D_GDN)
    wo = w_out[0].astype(bf16)
    wgt = w_gate[0].astype(bf16)
    wup = w_up[0].astype(bf16)
    wdn = w_down[0].astype(bf16)

    xp = x_prompt.reshape(batch * seq, D_MODEL)
    qtp, ktp, vtp, gp, gatep, bap = _in_proj_prompt(xp, nm, wqkvt, wg, wgate, wba, qgc, kgc,
                                                    batch=batch, seq=seq, tm=512)
    oap = _moba_prompt(qtp, ktp, vtp)
    conv0 = jnp.zeros((batch, 8, 3 * D_GDN), f32)
    ssm0 = jnp.zeros((batch, H_GDN, HEAD_DIM, HEAD_DIM), f32)
    ogp, ssm_p = _gdn(gp.reshape(batch, seq, 3 * D_GDN), conv0, cw, bap.reshape(batch, seq, LANES),
                      gatep.reshape(batch, seq, D_GDN), alog_pad, dtb_pad, gg8, ssm0, bb=1, tb=256)
    yp = _out_ffn(xp, oap, ogp.reshape(batch * seq, D_GDN), ag, wo, nf, wgt, wup, wdn, tm=512)

    ns = n_seq * t_new
    xs = x_sample.reshape(ns, D_MODEL)
    qs, ks, vs, gs, gates, bas = _in_proj_sample(xs, nm, wqkv, wg, wgate, wba, qg, kg, tm=256)
    cache_kt = jnp.swapaxes(cache_k[0], -1, -2)
    cache_vt = jnp.swapaxes(cache_v[0], -1, -2)
    oas = _moba_sample(qs, ks, vs, cache_kt, cache_vt, page_table, t_new=t_new)
    conv_s = jnp.pad(state_conv[0], ((0, 0), (8 - (CONV_W - 1), 0), (0, 0)))
    ogs, ssm_s = _gdn(gs.reshape(n_seq, t_new, 3 * D_GDN), conv_s, cw, bas.reshape(n_seq, t_new, LANES),
                      gates.reshape(n_seq, t_new, D_GDN), alog_pad, dtb_pad, gg8, state_ssm[0],
                      bb=2, tb=t_new)
    to_heads = lambda a: a.reshape(n_seq, t_new, H_ATT, HEAD_DIM).transpose(0, 2, 1, 3)
    oas_hm = oas.reshape(1, ns, H_ATT, HEAD_DIM).transpose(0, 2, 1, 3)
    ys = _out_ffn(xs, oas_hm, ogs.reshape(ns, D_GDN), ag, wo, nf, wgt, wup, wdn, tm=512)

    gp3 = gp.reshape(batch, seq, 3 * D_GDN)
    gs3 = gs.reshape(n_seq, t_new, 3 * D_GDN)
    return (yp.reshape(batch, seq, D_MODEL), ys.reshape(n_seq, t_new, D_MODEL),
            jnp.swapaxes(ktp, -1, -2)[None], jnp.swapaxes(vtp, -1, -2)[None],
            to_heads(ks)[None], to_heads(vs)[None],
            ssm_p[None], ssm_s[None],
            gp3[:, seq - (CONV_W - 1):][None], gs3[:, t_new - (CONV_W - 1):][None])
```

```python
import functools

import numpy as np
import jax
import jax.numpy as jnp
from jax import lax
from jax.experimental import pallas as pl
from jax.experimental.pallas import tpu as pltpu

f32 = jnp.float32
bf16 = jnp.bfloat16

D_MODEL = 1024
HEAD_DIM = 64
H_ATT = 8
H_GDN = 8
D_ATT = H_ATT * HEAD_DIM
D_GDN = H_GDN * HEAD_DIM
MOBA_BLOCK = 256
MOBA_SHIFT = 8
MOBA_TOPK = 3
GDN_CHUNK = 64
CONV_W = 4
PAGE_SIZE = 128
RMS_EPS = 1e-6
L2_EPS = 1e-6
ATT_SCALE = HEAD_DIM ** -0.5
LOG2E = float(np.log2(np.e))
NEG = -1e30
LANES = 128
VMEM_LIMIT = 56 * 1024 * 1024

HI = lax.Precision.HIGHEST
NN = (((1,), (0,)), ((), ()))
NT = (((1,), (1,)), ((), ()))
TN = (((0,), (0,)), ((), ()))

ALIBI_SLOPES = [2.0 ** (-8.0 * (i + 1) / H_ATT) for i in range(H_ATT)]


def _dot(a, b, dims=NN, precision=None):
    return lax.dot_general(a, b, dims, precision=precision, preferred_element_type=f32)


def _bdot(a, b, dims=NN):
    return lax.dot_general(a.astype(bf16), b.astype(bf16), dims, preferred_element_type=f32)


def _rms(x, axis=-1):
    return x * lax.rsqrt(jnp.mean(x * x, axis=axis, keepdims=True) + RMS_EPS)


def _silu(x):
    return x * jax.nn.sigmoid(x)


def _iota(shape, dim):
    return lax.broadcasted_iota(jnp.int32, shape, dim)


def _resident(shape):
    return pl.BlockSpec(shape, lambda i: (0,) * len(shape), pipeline_mode=pl.Buffered(1))


def _in_proj_prompt_body(x_ref, nm_ref, wqkvt_ref, wg_ref, wgate_ref, wba_ref, qgc_ref, kgc_ref,
                         qt_ref, kt_ref, vt_ref, g_ref, gate_ref, ba_ref):
    x = x_ref[...]
    xb = (_rms(x) * nm_ref[...]).astype(bf16)
    z = _dot(wqkvt_ref[...], xb, NT)
    for h in range(H_ATT):
        lo, hi = h * HEAD_DIM, (h + 1) * HEAD_DIM
        qt_ref[0, h] = _rms(z[lo:hi, :], axis=0) * qgc_ref[...]
        kt_ref[0, h] = _rms(z[D_ATT + lo:D_ATT + hi, :], axis=0) * kgc_ref[...]
        vt_ref[0, h] = z[2 * D_ATT + lo:2 * D_ATT + hi, :]
    g_ref[...] = jnp.dot(xb, wg_ref[...], preferred_element_type=f32)
    gate_ref[...] = jnp.dot(xb, wgate_ref[...], preferred_element_type=f32)
    ba_ref[...] = jnp.dot(xb, wba_ref[...], preferred_element_type=f32)


def _in_proj_prompt(x2d, nm, wqkvt, wg, wgate, wba, qgc, kgc, *, batch, seq, tm):
    n = x2d.shape[0]
    assert n == batch * seq and seq % tm == 0
    spb = seq // tm
    full = lambda shape: _resident(shape)
    rows = lambda width: pl.BlockSpec((tm, width), lambda i: (i, 0))
    t_spec = pl.BlockSpec((1, H_ATT, HEAD_DIM, tm), lambda i: (i // spb, 0, 0, i % spb))
    t_shape = jax.ShapeDtypeStruct((batch, H_ATT, HEAD_DIM, seq), f32)
    return pl.pallas_call(
        _in_proj_prompt_body,
        grid=(n // tm,),
        in_specs=[rows(D_MODEL), full((1, D_MODEL)), full(wqkvt.shape), full(wg.shape),
                  full(wgate.shape), full(wba.shape), full((HEAD_DIM, 1)), full((HEAD_DIM, 1))],
        out_specs=[t_spec, t_spec, t_spec, rows(3 * D_GDN), rows(D_GDN), rows(LANES)],
        out_shape=[t_shape, t_shape, t_shape,
                   jax.ShapeDtypeStruct((n, 3 * D_GDN), f32),
                   jax.ShapeDtypeStruct((n, D_GDN), f32),
                   jax.ShapeDtypeStruct((n, LANES), f32)],
        compiler_params=pltpu.CompilerParams(dimension_semantics=("arbitrary",),
                                             vmem_limit_bytes=VMEM_LIMIT),
        name="in_proj_prompt",
    )(x2d, nm, wqkvt, wg, wgate, wba, qgc, kgc)


def _in_proj_sample_body(x_ref, nm_ref, wqkv_ref, wg_ref, wgate_ref, wba_ref, qg_ref, kg_ref,
                         q_ref, k_ref, v_ref, g_ref, gate_ref, ba_ref):
    x = x_ref[...]
    xb = (_rms(x) * nm_ref[...]).astype(bf16)
    z = jnp.dot(xb, wqkv_ref[...], preferred_element_type=f32)
    qs, ks = [], []
    for h in range(H_ATT):
        lo, hi = h * HEAD_DIM, (h + 1) * HEAD_DIM
        qs.append(_rms(z[:, lo:hi]) * qg_ref[...])
        ks.append(_rms(z[:, D_ATT + lo:D_ATT + hi]) * kg_ref[...])
    q_ref[...] = jnp.concatenate(qs, axis=-1)
    k_ref[...] = jnp.concatenate(ks, axis=-1)
    v_ref[...] = z[:, 2 * D_ATT:]
    g_ref[...] = jnp.dot(xb, wg_ref[...], preferred_element_type=f32)
    gate_ref[...] = jnp.dot(xb, wgate_ref[...], preferred_element_type=f32)
    ba_ref[...] = jnp.dot(xb, wba_ref[...], preferred_element_type=f32)


def _in_proj_sample(x2d, nm, wqkv, wg, wgate, wba, qg, kg, *, tm):
    n = x2d.shape[0]
    assert n % tm == 0
    full = lambda shape: _resident(shape)
    rows = lambda width: pl.BlockSpec((tm, width), lambda i: (i, 0))
    tok = jax.ShapeDtypeStruct((n, D_ATT), f32)
    return pl.pallas_call(
        _in_proj_sample_body,
        grid=(n // tm,),
        in_specs=[rows(D_MODEL), full((1, D_MODEL)), full(wqkv.shape), full(wg.shape),
                  full(wgate.shape), full(wba.shape), full((1, HEAD_DIM)), full((1, HEAD_DIM))],
        out_specs=[rows(D_ATT), rows(D_ATT), rows(D_ATT), rows(3 * D_GDN), rows(D_GDN), rows(LANES)],
        out_shape=[tok, tok, tok,
                   jax.ShapeDtypeStruct((n, 3 * D_GDN), f32),
                   jax.ShapeDtypeStruct((n, D_GDN), f32),
                   jax.ShapeDtypeStruct((n, LANES), f32)],
        compiler_params=pltpu.CompilerParams(dimension_semantics=("arbitrary",),
                                             vmem_limit_bytes=VMEM_LIMIT),
        name="in_proj_sample",
    )(x2d, nm, wqkv, wg, wgate, wba, qg, kg)


def _fold_rows(s, op):
    parts = [s[r0:r0 + 8] for r0 in range(0, s.shape[0], 8)]
    while len(parts) > 1:
        parts = [op(parts[k], parts[k + 1]) for k in range(0, len(parts), 2)]
    return parts[0]


MOBA_HEADS = 2
MOBA_TRIP = 8


def _moba_prompt_body(slope_ref, qt_ref, qtn_ref, kt_ref, vt_ref, o_ref, ka_ref, va_ref, kmean_ref, s_ref,
                      qa_ref, *, seq):
    hg = pl.program_id(1)
    i = pl.program_id(2)
    n_blk = seq // MOBA_BLOCK
    trip = MOBA_TRIP * MOBA_BLOCK
    slopes = [slope_ref[pl.ds(hg * MOBA_HEADS + j, 1), 0:1] for j in range(MOBA_HEADS)]
    nrow = _iota((n_blk, MOBA_BLOCK), 0)
    row8 = _iota((8, MOBA_BLOCK), 0)

    def augmented_qt(qt, tile, j):
        tvec = jnp.full((n_blk, MOBA_BLOCK), tile, jnp.int32)
        gt = _dot(kmean_ref[j], qt, NN, HI)
        gt = jnp.where(nrow < tvec, gt, -jnp.inf)
        ranks = []
        for g0 in range(0, n_blk, 8):
            grp = gt[g0:g0 + 8, :]
            rank = jnp.zeros((8, MOBA_BLOCK), f32)
            for m in range(n_blk):
                gm = gt[m:m + 1, :]
                if m < g0:
                    beats = gm >= grp
                elif m >= g0 + 8:
                    beats = gm > grp
                else:
                    beats = (gm > grp) | ((gm == grp) & (row8 > m - g0))
                rank = rank + jnp.where(beats, 1.0, 0.0)
            ranks.append(rank)
        sel = (nrow < tvec) & (jnp.concatenate(ranks, axis=0) < MOBA_TOPK)
        shift = -(slopes[j] * MOBA_BLOCK) * tile.astype(f32)
        c_t = jnp.where(nrow < 2, 1.0, jnp.where(nrow == 2, shift, 0.0))
        return jnp.concatenate([qt * ATT_SCALE, jnp.where(sel, 0.0, NEG), c_t], axis=0).astype(bf16)

    @pl.when(i == 0)
    def _():
        pos = _iota((seq, HEAD_DIM), 0)
        col = _iota((seq, HEAD_DIM), 1)
        blk = jnp.right_shift(pos, MOBA_SHIFT)
        jloc = jnp.bitwise_and(pos, MOBA_BLOCK - 1)
        onehot = jnp.where(col == blk, 1.0, 0.0)
        ones_row = jnp.where(_iota((HEAD_DIM, seq), 0) == 0, 1.0, 0.0).astype(bf16)
        for j in range(MOBA_HEADS):
            k_tok = kt_ref[0, j].T
            e = jnp.where(col == n_blk, slopes[j] * jloc.astype(f32), onehot)
            e = jnp.where(col == n_blk + 1, (slopes[j] * MOBA_BLOCK) * blk.astype(f32), e)
            e = jnp.where(col == n_blk + 2, 1.0, e)
            ka_ref[j] = jnp.concatenate([k_tok.astype(bf16), e.astype(bf16)], axis=-1)
            va_ref[j, 0:HEAD_DIM, :] = vt_ref[0, j].astype(bf16)
            va_ref[j, HEAD_DIM:, :] = ones_row
            kmean_ref[j] = k_tok.reshape(n_blk, MOBA_BLOCK, HEAD_DIM).sum(axis=1) * (1.0 / MOBA_BLOCK)
        for j in range(MOBA_HEADS):
            qa_ref[j] = augmented_qt(qt_ref[0, j], i, j)

    arow = _iota((2 * HEAD_DIM, MOBA_BLOCK), 0)
    pen_rows = (arow >= HEAD_DIM) & (arow < HEAD_DIM + n_blk)
    causal = _iota((MOBA_BLOCK, MOBA_BLOCK), 0) <= _iota((MOBA_BLOCK, MOBA_BLOCK), 1)
    d0 = pl.multiple_of(i * MOBA_BLOCK, MOBA_BLOCK)
    qa = [qa_ref[j] for j in range(MOBA_HEADS)]
    s_own = []
    for j in range(MOBA_HEADS):
        qo = jnp.where(pen_rows, jnp.zeros_like(qa[j]), qa[j])
        s_own.append(jnp.where(causal, _dot(ka_ref[j, pl.ds(d0, MOBA_BLOCK), :], qo) * LOG2E, NEG))
    half = trip // 2
    rem = i % MOBA_TRIP
    n_full = i // MOBA_TRIP + (rem > MOBA_TRIP // 2).astype(jnp.int32)
    n_half = ((rem > 0) & (rem <= MOBA_TRIP // 2)).astype(jnp.int32)
    half0 = n_full * trip

    def pass1(size, base):
        def body(t, mx):
            r0 = pl.multiple_of(base + t * size, half)
            out = []
            for j in range(MOBA_HEADS):
                s = _dot(ka_ref[j, pl.ds(r0, size), :], qa[j]) * LOG2E
                s_ref[j, pl.ds(r0, size), :] = s
                out.append(jnp.maximum(mx[j], _fold_rows(s, jnp.maximum)))
            return tuple(out)
        return body

    mx = lax.fori_loop(0, n_full, pass1(trip, 0), tuple(_fold_rows(s, jnp.maximum) for s in s_own))
    mx = lax.fori_loop(0, n_half, pass1(half, half0), mx)
    for j in range(MOBA_HEADS):
        qa_ref[j] = augmented_qt(qtn_ref[0, j], i + 1, j)
    m_row = [jnp.max(mx[j], axis=0, keepdims=True) for j in range(MOBA_HEADS)]

    def pass2(size, base):
        def body(t, acc):
            r0 = pl.multiple_of(base + t * size, half)
            return tuple(acc[j] + _dot(va_ref[j, :, pl.ds(r0, size)],
                                       jnp.exp2((s_ref[j, pl.ds(r0, size), :] - m_row[j]).astype(bf16)))
                         for j in range(MOBA_HEADS))
        return body

    acc = lax.fori_loop(0, n_full, pass2(trip, 0),
                        tuple(_dot(va_ref[j, :, pl.ds(d0, MOBA_BLOCK)], jnp.exp2((s_own[j] - m_row[j]).astype(bf16)))
                              for j in range(MOBA_HEADS)))
    acc = lax.fori_loop(0, n_half, pass2(half, half0), acc)
    for j in range(MOBA_HEADS):
        o_ref[0, j] = (acc[j][:HEAD_DIM] / acc[j][HEAD_DIM:HEAD_DIM + 1]).T


def _moba_prompt(qt, kt, vt):
    batch, heads, _, seq = qt.shape
    n_blk = seq // MOBA_BLOCK
    assert n_blk % MOBA_TRIP == 0 and 2 * n_blk == HEAD_DIM and MOBA_BLOCK == 1 << MOBA_SHIFT
    assert heads % MOBA_HEADS == 0
    slopes = jnp.asarray(np.repeat(np.asarray(ALIBI_SLOPES, np.float32)[:, None], LANES, axis=1))
    tile = pl.BlockSpec((1, MOBA_HEADS, HEAD_DIM, MOBA_BLOCK), lambda b, h, i: (b, h, 0, i))
    next_tile = pl.BlockSpec((1, MOBA_HEADS, HEAD_DIM, MOBA_BLOCK),
                             lambda b, h, i: (b, h, 0, jnp.minimum(i + 1, n_blk - 1)))
    whole = pl.BlockSpec((1, MOBA_HEADS, HEAD_DIM, seq), lambda b, h, i: (b, h, 0, 0),
                         pipeline_mode=pl.Buffered(1))
    return pl.pallas_call(
        functools.partial(_moba_prompt_body, seq=seq),
        grid=(batch, heads // MOBA_HEADS, n_blk),
        in_specs=[pl.BlockSpec((H_ATT, LANES), lambda b, h, i: (0, 0)), tile, next_tile, whole, whole],
        out_specs=pl.BlockSpec((1, MOBA_HEADS, MOBA_BLOCK, HEAD_DIM), lambda b, h, i: (b, h, i, 0)),
        out_shape=jax.ShapeDtypeStruct((batch, heads, seq, HEAD_DIM), f32),
        scratch_shapes=[pltpu.VMEM((MOBA_HEADS, seq, 2 * HEAD_DIM), bf16),
                        pltpu.VMEM((MOBA_HEADS, 2 * HEAD_DIM, seq), bf16),
                        pltpu.VMEM((MOBA_HEADS, n_blk, HEAD_DIM), f32),
                        pltpu.VMEM((MOBA_HEADS, seq, MOBA_BLOCK), f32),
                        pltpu.VMEM((MOBA_HEADS, 2 * HEAD_DIM, MOBA_BLOCK), bf16)],
        compiler_params=pltpu.CompilerParams(dimension_semantics=("arbitrary",) * 3,
                                             vmem_limit_bytes=VMEM_LIMIT),
        name="moba_prompt",
    )(slopes, qt, qt, kt, vt)


SAMPLE_SEQS = 2


def _moba_sample_body(pt_ref, q_ref, k_ref, v_ref, *refs, past_len, t_new, spb):
    del pt_ref
    n_pages = past_len // PAGE_SIZE
    kp_refs, vp_refs = refs[:spb * n_pages], refs[spb * n_pages:2 * spb * n_pages]
    o_ref = refs[2 * spb * n_pages]
    own = past_len // MOBA_BLOCK
    n_sel = min(MOBA_TOPK, own)
    tq = _iota((t_new, past_len), 0)
    sk = _iota((t_new, past_len), 1)
    dist = (past_len + tq - sk).astype(f32)
    tq2 = _iota((t_new, t_new), 0)
    sk2 = _iota((t_new, t_new), 1)
    units = [(s, h) for s in range(spb) for h in range(H_ATT)]
    us = range(len(units))
    rows = [slice(s * t_new, (s + 1) * t_new) for s, _ in units]
    cols = [slice(h * HEAD_DIM, (h + 1) * HEAD_DIM) for _, h in units]
    slope = [ALIBI_SLOPES[h] for _, h in units]
    q = [q_ref[rows[u], cols[u]] for u in us]
    qb = [(q[u] * ATT_SCALE).astype(bf16) for u in us]
    kb = [jnp.concatenate([kp_refs[s * n_pages + p][0, h].astype(bf16) for p in range(n_pages)], axis=1)
          for s, h in units]
    s_raw = [_dot(qb[u], kb[u]) for u in us]
    gate = [jnp.concatenate([jnp.sum(s_raw[u][:, n * MOBA_BLOCK:(n + 1) * MOBA_BLOCK], axis=1, keepdims=True)
                             for n in range(own)], axis=1) for u in us]

    def penalty(g, n):
        gn = g[:, n:n + 1]
        rank = jnp.zeros((t_new, 1), f32)
        for m in range(own):
            if m != n:
                gm = g[:, m:m + 1]
                beats = (gm > gn) | ((gm == gn) & (m < n))
                rank = rank + jnp.where(beats, 1.0, 0.0)
        return jnp.broadcast_to(jnp.where(rank < n_sel, 0.0, NEG), (t_new, MOBA_BLOCK))

    pen = [jnp.concatenate([penalty(gate[u], n) for n in range(own)], axis=1) for u in us]
    s_past = [s_raw[u] + pen[u] - slope[u] * dist for u in us]
    s_own = [jnp.where(sk2 <= tq2,
                       _dot(qb[u], k_ref[rows[u], cols[u]].astype(bf16), NT) - slope[u] * (tq2 - sk2).astype(f32),
                       NEG) for u in us]
    m = [jnp.maximum(jnp.max(s_past[u], axis=1, keepdims=True), jnp.max(s_own[u], axis=1, keepdims=True))
         for u in us]
    p_past = [jnp.exp(s_past[u] - m[u]) for u in us]
    p_own = [jnp.exp(s_own[u] - m[u]) for u in us]
    denom = [jnp.sum(p_past[u], axis=1, keepdims=True) + jnp.sum(p_own[u], axis=1, keepdims=True) for u in us]
    vb = [jnp.concatenate([vp_refs[s * n_pages + p][0, h].astype(bf16) for p in range(n_pages)], axis=1)
          for s, h in units]
    o = [(_dot(p_own[u].astype(bf16), v_ref[rows[u], cols[u]].astype(bf16))
          + _dot(p_past[u].astype(bf16), vb[u], NT)) / denom[u] for u in us]
    for s in range(spb):
        o_ref[s * t_new:(s + 1) * t_new, :] = jnp.concatenate(o[s * H_ATT:(s + 1) * H_ATT], axis=-1)


def _moba_sample(q, k, v, cache_kt, cache_vt, page_table, *, t_new):
    n_seq, n_pages = page_table.shape
    past_len = n_pages * PAGE_SIZE
    assert past_len % MOBA_BLOCK == 0 and past_len // MOBA_BLOCK >= 1
    spb = SAMPLE_SEQS
    assert n_seq % spb == 0
    rows = pl.BlockSpec((spb * t_new, D_ATT), lambda b, pt: (b, 0))

    def page_spec(s, p):
        return pl.BlockSpec((1, H_ATT, HEAD_DIM, PAGE_SIZE),
                            lambda b, pt: (pt[(b * spb + s) * n_pages + p], 0, 0, 0))

    pages = [page_spec(s, p) for s in range(spb) for p in range(n_pages)]
    grid_spec = pltpu.PrefetchScalarGridSpec(
        num_scalar_prefetch=1,
        grid=(n_seq // spb,),
        in_specs=[rows, rows, rows] + pages * 2,
        out_specs=rows,
    )
    return pl.pallas_call(
        functools.partial(_moba_sample_body, past_len=past_len, t_new=t_new, spb=spb),
        grid_spec=grid_spec,
        out_shape=jax.ShapeDtypeStruct(q.shape, f32),
        compiler_params=pltpu.CompilerParams(dimension_semantics=("arbitrary",),
                                             vmem_limit_bytes=VMEM_LIMIT),
        name="moba_sample",
    )(page_table.reshape(-1), q, k, v, *([cache_kt] * (spb * n_pages)), *([cache_vt] * (spb * n_pages)))


def _conv_silu(cur, prev8, w):
    t = cur.shape[0]
    row8 = _iota((8, cur.shape[1]), 0)
    y = None
    for i in range(CONV_W):
        d = CONV_W - 1 - i
        if d == 0:
            term = cur
        else:
            sh = pltpu.roll(cur, d, 0)
            top = jnp.where(row8 < d, pltpu.roll(prev8, d, 0), sh[0:8])
            term = top if t == 8 else jnp.concatenate([top, sh[8:]], axis=0)
        term = term * w[i:i + 1, :]
        y = term if y is None else y + term
    return _silu(y)


def _gdn_body(uq_ref, uk_ref, uv_ref, pq_ref, pk_ref, pv_ref, cq_ref, ck_ref, cv_ref,
              wq_ref, wk_ref, wv_ref, ba_ref, gate_ref, alog_ref, dtb_ref, gg_ref, m0_ref,
              o_ref, m_ref, *, bb, tb, c):
    t = pl.program_id(1)

    @pl.when(t == 0)
    def _():
        m_ref[...] = m0_ref[...]

    n_chunks = tb // c
    lc = int(np.log2(c))
    assert 1 << lc == c
    ri = _iota((tb, tb), 0)
    ci = _iota((tb, tb), 1)
    same = jnp.right_shift(ri, lc) == jnp.right_shift(ci, lc)
    eye = ri == ci
    keep = same & (ri >= ci)
    cum = jnp.where(keep, 1.0, 0.0)
    tot = jnp.where(same, 1.0, 0.0)
    ident = jnp.where(eye, 1.0, 0.0)
    first = jnp.full((8, D_GDN), t, jnp.int32) == 0

    for s in range(bb):
        q = _conv_silu(uq_ref[s], jnp.where(first, cq_ref[s], pq_ref[s]), wq_ref[...])
        k = _conv_silu(uk_ref[s], jnp.where(first, ck_ref[s], pk_ref[s]), wk_ref[...])
        v = _conv_silu(uv_ref[s], jnp.where(first, cv_ref[s], pv_ref[s]), wv_ref[...])
        ba = ba_ref[s]
        beta_all = jax.nn.sigmoid(ba)
        xa = ba + dtb_ref[...]
        softplus = jnp.maximum(xa, 0.0) + jnp.log1p(jnp.exp(-jnp.abs(xa)))
        g_all = -jnp.exp(alog_ref[...]) * softplus
        gcum_all = _dot(cum, g_all, NN, HI)
        gtot_all = _dot(tot, g_all, NN, HI)
        gcum_t = gcum_all.T if tb % LANES == 0 else None
        hs = range(H_GDN)
        beta = [beta_all[:, h:h + 1] for h in hs]
        gcum = [gcum_all[:, H_GDN + h:H_GDN + h + 1] for h in hs]
        gtot = [gtot_all[:, H_GDN + h:H_GDN + h + 1] for h in hs]
        if gcum_t is None:
            grow = [jnp.sum(jnp.where(eye, gcum[h], 0.0), axis=0, keepdims=True) for h in hs]
        else:
            grow = [gcum_t[H_GDN + h:H_GDN + h + 1, :] for h in hs]
        l2n = lambda z: z * lax.rsqrt(jnp.sum(z * z, axis=-1, keepdims=True) + L2_EPS)
        qh = [l2n(q[:, h * HEAD_DIM:(h + 1) * HEAD_DIM]) * (HEAD_DIM ** -0.5) for h in hs]
        kh = [l2n(k[:, h * HEAD_DIM:(h + 1) * HEAD_DIM]) for h in hs]
        vh = [v[:, h * HEAD_DIM:(h + 1) * HEAD_DIM] for h in hs]
        decay = [jnp.exp(jnp.where(keep, gcum[h] - grow[h], NEG)) for h in hs]
        pw = [beta[h] * _bdot(kh[h], kh[h], NT) * jnp.where(eye, 0.0, decay[h]) for h in hs]
        x = [ident - pw[h] for h in hs]
        for _ in range(lc - 1):
            pw = [_bdot(pw[h], pw[h]) for h in hs]
            x = [x[h] + _bdot(x[h], pw[h]) for h in hs]
        eg = [jnp.exp(gcum[h]) for h in hs]
        sol = [_bdot(x[h], jnp.concatenate([beta[h] * vh[h], (beta[h] * eg[h]) * kh[h]], axis=-1))
               for h in hs]
        aqk = [_bdot(qh[h], kh[h], NT) * decay[h] for h in hs]
        qg = [qh[h] * eg[h] for h in hs]
        kd = [kh[h] * jnp.exp(gtot[h] - gcum[h]) for h in hs]
        gdec = [jnp.exp(gtot[h]) for h in hs]
        m = [m_ref[s, h] for h in hs]
        deltas = [[] for _ in hs]
        oqs = [[] for _ in hs]
        for ch in range(n_chunks):
            sl = slice(ch * c, (ch + 1) * c)
            both = [_bdot(jnp.concatenate([sol[h][sl, HEAD_DIM:], qg[h][sl]], axis=0), m[h]) for h in hs]
            for h in hs:
                deltas[h].append(sol[h][sl, :HEAD_DIM] - both[h][:c])
                oqs[h].append(both[h][c:])
            m = [gdec[h][ch * c:ch * c + 1, :] * m[h] + _dot(kd[h][sl], deltas[h][ch], TN) for h in hs]
        outs = []
        for h in hs:
            m_ref[s, h] = m[h]
            if n_chunks == 1:
                delta_all, oq = deltas[h][0], oqs[h][0]
            else:
                delta_all, oq = jnp.concatenate(deltas[h], axis=0), jnp.concatenate(oqs[h], axis=0)
            outs.append(_rms(oq + _bdot(aqk[h], delta_all)))
        o = jnp.concatenate(outs, axis=-1) * gg_ref[...]
        o_ref[s] = o * _silu(gate_ref[s])


def _gdn(u, conv_pad, conv_w, ba, gate, alog_pad, dtb_pad, gg8, m0, *, bb, tb):
    batch, seq, _ = u.shape
    c = min(GDN_CHUNK, seq)
    assert batch % bb == 0 and seq % tb == 0 and tb % c == 0 and tb % 8 == 0
    cur = lambda g: pl.BlockSpec((bb, tb, D_GDN), lambda b, t: (b, t, g))
    prev = lambda g: pl.BlockSpec((bb, 8, D_GDN), lambda b, t: (b, jnp.maximum(t * (tb // 8) - 1, 0), g))
    cbuf = lambda g: pl.BlockSpec((bb, 8, D_GDN), lambda b, t: (b, 0, g))
    cw = lambda g: pl.BlockSpec((CONV_W, D_GDN), lambda b, t: (0, g))
    row = lambda width: pl.BlockSpec((1, width), lambda b, t: (0, 0))
    state = pl.BlockSpec((bb, H_GDN, HEAD_DIM, HEAD_DIM), lambda b, t: (b, 0, 0, 0))
    return pl.pallas_call(
        functools.partial(_gdn_body, bb=bb, tb=tb, c=c),
        grid=(batch // bb, seq // tb),
        in_specs=[cur(0), cur(1), cur(2), prev(0), prev(1), prev(2), cbuf(0), cbuf(1), cbuf(2),
                  cw(0), cw(1), cw(2),
                  pl.BlockSpec((bb, tb, LANES), lambda b, t: (b, t, 0)),
                  pl.BlockSpec((bb, tb, D_GDN), lambda b, t: (b, t, 0)),
                  row(LANES), row(LANES), row(D_GDN), state],
        out_specs=[pl.BlockSpec((bb, tb, D_GDN), lambda b, t: (b, t, 0)), state],
        out_shape=[jax.ShapeDtypeStruct((batch, seq, D_GDN), f32),
                   jax.ShapeDtypeStruct((batch, H_GDN, HEAD_DIM, HEAD_DIM), f32)],
        compiler_params=pltpu.CompilerParams(dimension_semantics=("arbitrary",) * 2,
                                             vmem_limit_bytes=VMEM_LIMIT),
        name="gdn",
    )(u, u, u, u, u, u, conv_pad, conv_pad, conv_pad, conv_w, conv_w, conv_w,
      ba, gate, alog_pad, dtb_pad, gg8, m0)


FF_CHUNK = 256


def _out_ffn_body(x_ref, oa_ref, og_ref, ag_ref, wo_ref, nf_ref, wg_ref, wu_ref, wd_ref, y_ref):
    oa = jnp.concatenate([oa_ref[0, h] for h in range(H_ATT)], axis=-1)
    oa = _rms(oa) * ag_ref[...]
    mix = jnp.concatenate([oa, og_ref[...]], axis=-1).astype(bf16)
    hid = x_ref[...] + jnp.dot(mix, wo_ref[...], preferred_element_type=f32)
    ub = (_rms(hid) * nf_ref[...]).astype(bf16)
    d_ff = wg_ref.shape[1]
    acc = hid
    for cidx in range(d_ff // FF_CHUNK):
        sl = slice(cidx * FF_CHUNK, (cidx + 1) * FF_CHUNK)
        a = jnp.dot(ub, wg_ref[:, sl], preferred_element_type=f32)
        b = jnp.dot(ub, wu_ref[:, sl], preferred_element_type=f32)
        acc = acc + jnp.dot((_silu(a) * b).astype(bf16), wd_ref[sl, :], preferred_element_type=f32)
    y_ref[...] = acc


def _out_ffn(x2d, o_att, o_gdn, ag, wo, nf, wg, wu, wd, *, tm):
    n = x2d.shape[0]
    batch, _, seq, _ = o_att.shape
    assert n == batch * seq and seq % tm == 0 and wg.shape[1] % FF_CHUNK == 0
    spb = seq // tm
    full = lambda shape: _resident(shape)
    rows = lambda width: pl.BlockSpec((tm, width), lambda i: (i, 0))
    return pl.pallas_call(
        _out_ffn_body,
        grid=(n // tm,),
        in_specs=[rows(D_MODEL),
                  pl.BlockSpec((1, H_ATT, tm, HEAD_DIM), lambda i: (i // spb, 0, i % spb, 0)),
                  rows(D_GDN), full((1, D_ATT)), full(wo.shape), full((1, D_MODEL)),
                  full(wg.shape), full(wu.shape), full(wd.shape)],
        out_specs=rows(D_MODEL),
        out_shape=jax.ShapeDtypeStruct((n, D_MODEL), f32),
        compiler_params=pltpu.CompilerParams(dimension_semantics=("arbitrary",),
                                             vmem_limit_bytes=VMEM_LIMIT),
        name="out_ffn",
    )(x2d, o_att, o_gdn, ag, wo, nf, wg, wu, wd)


def kernel(x_prompt, x_sample, cache_k, cache_v, state_ssm, state_conv, page_table, norm_mix, w_in,
           q_norm_g, k_norm_g, attn_out_g, conv_w, a_log, dt_bias, gdn_out_g, w_out, norm_ffn,
           w_gate, w_up, w_down):
    depth = w_in.shape[0]
    assert depth == 1
    batch, seq, _ = x_prompt.shape
    n_seq, t_new, _ = x_sample.shape

    w = w_in[0]
    c0, c1, c2 = 3 * D_ATT, 3 * D_ATT + 3 * D_GDN, 3 * D_ATT + 4 * D_GDN
    wqkv = w[:, :c0].astype(bf16)
    wqkvt = w.T[:c0].astype(bf16)
    wg = w[:, c0:c1].astype(bf16)
    wgate = w[:, c1:c2].astype(bf16)
    wba = jnp.pad(w[:, c2:], ((0, 0), (0, LANES - 2 * H_GDN))).astype(bf16)
    nm = norm_mix[0].reshape(1, D_MODEL)
    qg = q_norm_g[0].reshape(1, HEAD_DIM)
    qgc = q_norm_g[0].reshape(HEAD_DIM, 1)
    kg = k_norm_g[0].reshape(1, HEAD_DIM)
    kgc = k_norm_g[0].reshape(HEAD_DIM, 1)
    ag = attn_out_g[0].reshape(1, D_ATT)
    nf = norm_ffn[0].reshape(1, D_MODEL)
    cw = conv_w[0]
    pad_lo = lambda vec: jnp.pad(vec.astype(f32), (H_GDN, LANES - 2 * H_GDN)).reshape(1, LANES)
    alog_pad = pad_lo(a_log[0])
    dtb_pad = pad_lo(dt_bias[0])
    gg8 = jnp.tile(gdn_out_g[0].astype(f32), H_GDN).reshape(1, D_GDN)
    wo = w_out[0].astype(bf16)
    wgt = w_gate[0].astype(bf16)
    wup = w_up[0].astype(bf16)
    wdn = w_down[0].astype(bf16)

    xp = x_prompt.reshape(batch * seq, D_MODEL)
    qtp, ktp, vtp, gp, gatep, bap = _in_proj_prompt(xp, nm, wqkvt, wg, wgate, wba, qgc, kgc,
                                                    batch=batch, seq=seq, tm=512)
    oap = _moba_prompt(qtp, ktp, vtp)
    conv0 = jnp.zeros((batch, 8, 3 * D_GDN), f32)
    ssm0 = jnp.zeros((batch, H_GDN, HEAD_DIM, HEAD_DIM), f32)
    ogp, ssm_p = _gdn(gp.reshape(batch, seq, 3 * D_GDN), conv0, cw, bap.reshape(batch, seq, LANES),
                      gatep.reshape(batch, seq, D_GDN), alog_pad, dtb_pad, gg8, ssm0, bb=1, tb=256)
    yp = _out_ffn(xp, oap, ogp.reshape(batch * seq, D_GDN), ag, wo, nf, wgt, wup, wdn, tm=512)

    ns = n_seq * t_new
    xs = x_sample.reshape(ns, D_MODEL)
    qs, ks, vs, gs, gates, bas = _in_proj_sample(xs, nm, wqkv, wg, wgate, wba, qg, kg, tm=256)
    cache_kt = jnp.swapaxes(cache_k[0], -1, -2)
    cache_vt = jnp.swapaxes(cache_v[0], -1, -2)
    oas = _moba_sample(qs, ks, vs, cache_kt, cache_vt, page_table, t_new=t_new)
    conv_s = jnp.pad(state_conv[0], ((0, 0), (8 - (CONV_W - 1), 0), (0, 0)))
    ogs, ssm_s = _gdn(gs.reshape(n_seq, t_new, 3 * D_GDN), conv_s, cw, bas.reshape(n_seq, t_new, LANES),
                      gates.reshape(n_seq, t_new, D_GDN), alog_pad, dtb_pad, gg8, state_ssm[0],
                      bb=4, tb=t_new)
    to_heads = lambda a: a.reshape(n_seq, t_new, H_ATT, HEAD_DIM).transpose(0, 2, 1, 3)
    oas_hm = oas.reshape(1, ns, H_ATT, HEAD_DIM).transpose(0, 2, 1, 3)
    ys = _out_ffn(xs, oas_hm, ogs.reshape(ns, D_GDN), ag, wo, nf, wgt, wup, wdn, tm=512)

    gp3 = gp.reshape(batch, seq, 3 * D_GDN)
    gs3 = gs.reshape(n_seq, t_new, 3 * D_GDN)
    return (yp.reshape(batch, seq, D_MODEL), ys.reshape(n_seq, t_new, D_MODEL),
            jnp.swapaxes(ktp, -1, -2)[None], jnp.swapaxes(vtp, -1, -2)[None],
            to_heads(ks)[None], to_heads(vs)[None],
            ssm_p[None], ssm_s[None],
            gp3[:, seq - (CONV_W - 1):][None], gs3[:, t_new - (CONV_W - 1):][None])
```

```python
import functools

import numpy as np
import jax
import jax.numpy as jnp
from jax import lax
from jax.experimental import pallas as pl
from jax.experimental.pallas import tpu as pltpu

f32 = jnp.float32
bf16 = jnp.bfloat16

D_MODEL = 1024
HEAD_DIM = 64
H_ATT = 8
H_GDN = 8
D_ATT = H_ATT * HEAD_DIM
D_GDN = H_GDN * HEAD_DIM
MOBA_BLOCK = 256
MOBA_SHIFT = 8
MOBA_TOPK = 3
GDN_CHUNK = 64
GDN_SOLVE = 128
CONV_W = 4
PAGE_SIZE = 128
RMS_EPS = 1e-6
L2_EPS = 1e-6
ATT_SCALE = HEAD_DIM ** -0.5
LOG2E = float(np.log2(np.e))
NEG = -1e30
LANES = 128
VMEM_LIMIT = 56 * 1024 * 1024

HI = lax.Precision.HIGHEST
NN = (((1,), (0,)), ((), ()))
NT = (((1,), (1,)), ((), ()))
TN = (((0,), (0,)), ((), ()))

ALIBI_SLOPES = [2.0 ** (-8.0 * (i + 1) / H_ATT) for i in range(H_ATT)]


def _dot(a, b, dims=NN, precision=None):
    return lax.dot_general(a, b, dims, precision=precision, preferred_element_type=f32)


def _bdot(a, b, dims=NN):
    return lax.dot_general(a.astype(bf16), b.astype(bf16), dims, preferred_element_type=f32)


def _rms(x, axis=-1):
    return x * lax.rsqrt(jnp.mean(x * x, axis=axis, keepdims=True) + RMS_EPS)


def _silu(x):
    return x * jax.nn.sigmoid(x)


def _iota(shape, dim):
    return lax.broadcasted_iota(jnp.int32, shape, dim)


def _resident(shape):
    return pl.BlockSpec(shape, lambda i: (0,) * len(shape), pipeline_mode=pl.Buffered(1))


def _in_proj_prompt_body(x_ref, nm_ref, wqkvt_ref, wg_ref, wgate_ref, wba_ref, qgc_ref, kgc_ref,
                         qt_ref, kt_ref, vt_ref, g_ref, gate_ref, ba_ref):
    x = x_ref[...]
    xb = (_rms(x) * nm_ref[...]).astype(bf16)
    z = _dot(wqkvt_ref[...], xb, NT)
    for h in range(H_ATT):
        lo, hi = h * HEAD_DIM, (h + 1) * HEAD_DIM
        qt_ref[0, h] = _rms(z[lo:hi, :], axis=0) * qgc_ref[...]
        kt_ref[0, h] = _rms(z[D_ATT + lo:D_ATT + hi, :], axis=0) * kgc_ref[...]
        vt_ref[0, h] = z[2 * D_ATT + lo:2 * D_ATT + hi, :]
    g_ref[...] = jnp.dot(xb, wg_ref[...], preferred_element_type=f32)
    gate_ref[...] = jnp.dot(xb, wgate_ref[...], preferred_element_type=f32)
    ba_ref[...] = jnp.dot(xb, wba_ref[...], preferred_element_type=f32)


def _in_proj_prompt(x2d, nm, wqkvt, wg, wgate, wba, qgc, kgc, *, batch, seq, tm):
    n = x2d.shape[0]
    assert n == batch * seq and seq % tm == 0
    spb = seq // tm
    full = lambda shape: _resident(shape)
    rows = lambda width: pl.BlockSpec((tm, width), lambda i: (i, 0))
    t_spec = pl.BlockSpec((1, H_ATT, HEAD_DIM, tm), lambda i: (i // spb, 0, 0, i % spb))
    t_shape = jax.ShapeDtypeStruct((batch, H_ATT, HEAD_DIM, seq), f32)
    return pl.pallas_call(
        _in_proj_prompt_body,
        grid=(n // tm,),
        in_specs=[rows(D_MODEL), full((1, D_MODEL)), full(wqkvt.shape), full(wg.shape),
                  full(wgate.shape), full(wba.shape), full((HEAD_DIM, 1)), full((HEAD_DIM, 1))],
        out_specs=[t_spec, t_spec, t_spec, rows(3 * D_GDN), rows(D_GDN), rows(LANES)],
        out_shape=[t_shape, t_shape, t_shape,
                   jax.ShapeDtypeStruct((n, 3 * D_GDN), f32),
                   jax.ShapeDtypeStruct((n, D_GDN), f32),
                   jax.ShapeDtypeStruct((n, LANES), f32)],
        compiler_params=pltpu.CompilerParams(dimension_semantics=("arbitrary",),
                                             vmem_limit_bytes=VMEM_LIMIT),
        name="in_proj_prompt",
    )(x2d, nm, wqkvt, wg, wgate, wba, qgc, kgc)


def _in_proj_sample_body(x_ref, nm_ref, wqkv_ref, wg_ref, wgate_ref, wba_ref, qg_ref, kg_ref,
                         q_ref, k_ref, v_ref, g_ref, gate_ref, ba_ref):
    x = x_ref[...]
    xb = (_rms(x) * nm_ref[...]).astype(bf16)
    z = jnp.dot(xb, wqkv_ref[...], preferred_element_type=f32)
    qs, ks = [], []
    for h in range(H_ATT):
        lo, hi = h * HEAD_DIM, (h + 1) * HEAD_DIM
        qs.append(_rms(z[:, lo:hi]) * qg_ref[...])
        ks.append(_rms(z[:, D_ATT + lo:D_ATT + hi]) * kg_ref[...])
    q_ref[...] = jnp.concatenate(qs, axis=-1)
    k_ref[...] = jnp.concatenate(ks, axis=-1)
    v_ref[...] = z[:, 2 * D_ATT:]
    g_ref[...] = jnp.dot(xb, wg_ref[...], preferred_element_type=f32)
    gate_ref[...] = jnp.dot(xb, wgate_ref[...], preferred_element_type=f32)
    ba_ref[...] = jnp.dot(xb, wba_ref[...], preferred_element_type=f32)


def _in_proj_sample(x2d, nm, wqkv, wg, wgate, wba, qg, kg, *, tm):
    n = x2d.shape[0]
    assert n % tm == 0
    full = lambda shape: _resident(shape)
    rows = lambda width: pl.BlockSpec((tm, width), lambda i: (i, 0))
    tok = jax.ShapeDtypeStruct((n, D_ATT), f32)
    return pl.pallas_call(
        _in_proj_sample_body,
        grid=(n // tm,),
        in_specs=[rows(D_MODEL), full((1, D_MODEL)), full(wqkv.shape), full(wg.shape),
                  full(wgate.shape), full(wba.shape), full((1, HEAD_DIM)), full((1, HEAD_DIM))],
        out_specs=[rows(D_ATT), rows(D_ATT), rows(D_ATT), rows(3 * D_GDN), rows(D_GDN), rows(LANES)],
        out_shape=[tok, tok, tok,
                   jax.ShapeDtypeStruct((n, 3 * D_GDN), f32),
                   jax.ShapeDtypeStruct((n, D_GDN), f32),
                   jax.ShapeDtypeStruct((n, LANES), f32)],
        compiler_params=pltpu.CompilerParams(dimension_semantics=("arbitrary",),
                                             vmem_limit_bytes=VMEM_LIMIT),
        name="in_proj_sample",
    )(x2d, nm, wqkv, wg, wgate, wba, qg, kg)


def _fold_rows(s, op):
    parts = [s[r0:r0 + 8] for r0 in range(0, s.shape[0], 8)]
    while len(parts) > 1:
        parts = [op(parts[k], parts[k + 1]) for k in range(0, len(parts), 2)]
    return parts[0]


MOBA_HEADS = 2
MOBA_TRIP = 8


def _moba_prompt_body(slope_ref, qt_ref, qtn_ref, kt_ref, vt_ref, o_ref, ka_ref, va_ref, kmean_ref, s_ref,
                      qa_ref, *, seq):
    hg = pl.program_id(1)
    i = pl.program_id(2)
    n_blk = seq // MOBA_BLOCK
    trip = MOBA_TRIP * MOBA_BLOCK
    slopes = [slope_ref[pl.ds(hg * MOBA_HEADS + j, 1), 0:1] for j in range(MOBA_HEADS)]
    nrow = _iota((n_blk, MOBA_BLOCK), 0)
    row8 = _iota((8, MOBA_BLOCK), 0)

    def augmented_qt(qt, tile, j):
        tvec = jnp.full((n_blk, MOBA_BLOCK), tile, jnp.int32)
        gt = _dot(kmean_ref[j], qt, NN, HI)
        gt = jnp.where(nrow < tvec, gt, -jnp.inf)
        ranks = []
        for g0 in range(0, n_blk, 8):
            grp = gt[g0:g0 + 8, :]
            rank = jnp.zeros((8, MOBA_BLOCK), f32)
            for m in range(n_blk):
                gm = gt[m:m + 1, :]
                if m < g0:
                    beats = gm >= grp
                elif m >= g0 + 8:
                    beats = gm > grp
                else:
                    beats = (gm > grp) | ((gm == grp) & (row8 > m - g0))
                rank = rank + jnp.where(beats, 1.0, 0.0)
            ranks.append(rank)
        sel = (nrow < tvec) & (jnp.concatenate(ranks, axis=0) < MOBA_TOPK)
        shift = -(slopes[j] * MOBA_BLOCK) * tile.astype(f32)
        c_t = jnp.where(nrow < 2, 1.0, jnp.where(nrow == 2, shift, 0.0))
        return jnp.concatenate([qt * ATT_SCALE, jnp.where(sel, 0.0, NEG), c_t], axis=0).astype(bf16)

    @pl.when(i == 0)
    def _():
        row = _iota((HEAD_DIM, seq), 0)
        pos = _iota((HEAD_DIM, seq), 1)
        blk = jnp.right_shift(pos, MOBA_SHIFT)
        jloc = jnp.bitwise_and(pos, MOBA_BLOCK - 1)
        onehot = jnp.where(row == blk, 1.0, 0.0)
        ones_row = jnp.where(row == 0, 1.0, 0.0).astype(bf16)
        eye_b = (_iota((MOBA_BLOCK, MOBA_BLOCK), 0) == _iota((MOBA_BLOCK, MOBA_BLOCK), 1)).astype(bf16)
        for j in range(MOBA_HEADS):
            kt = kt_ref[0, j]
            e = jnp.where(row == n_blk, slopes[j] * jloc.astype(f32), onehot)
            e = jnp.where(row == n_blk + 1, (slopes[j] * MOBA_BLOCK) * blk.astype(f32), e)
            e = jnp.where(row == n_blk + 2, 1.0, e)
            ka_t = jnp.concatenate([kt.astype(bf16), e.astype(bf16)], axis=0)
            for n in range(n_blk):
                blk_cols = slice(n * MOBA_BLOCK, (n + 1) * MOBA_BLOCK)
                ka_ref[j, blk_cols, :] = _dot(eye_b, ka_t[:, blk_cols], NT).astype(bf16)
            va_ref[j, 0:HEAD_DIM, :] = vt_ref[0, j].astype(bf16)
            va_ref[j, HEAD_DIM:, :] = ones_row
            sums = [jnp.sum(kt[:, n * MOBA_BLOCK:(n + 1) * MOBA_BLOCK], axis=1, keepdims=True)
                    for n in range(n_blk)]
            kmean_ref[j] = (jnp.concatenate(sums, axis=1) * (1.0 / MOBA_BLOCK)).T
        for j in range(MOBA_HEADS):
            qa_ref[j] = augmented_qt(qt_ref[0, j], i, j)

    arow = _iota((2 * HEAD_DIM, MOBA_BLOCK), 0)
    pen_rows = (arow >= HEAD_DIM) & (arow < HEAD_DIM + n_blk)
    causal = _iota((MOBA_BLOCK, MOBA_BLOCK), 0) <= _iota((MOBA_BLOCK, MOBA_BLOCK), 1)
    d0 = pl.multiple_of(i * MOBA_BLOCK, MOBA_BLOCK)
    qa = [qa_ref[j] for j in range(MOBA_HEADS)]
    s_own = []
    for j in range(MOBA_HEADS):
        qo = jnp.where(pen_rows, jnp.zeros_like(qa[j]), qa[j])
        s_own.append(jnp.where(causal, _dot(ka_ref[j, pl.ds(d0, MOBA_BLOCK), :], qo) * LOG2E, NEG))
    half = trip // 2
    rem = i % MOBA_TRIP
    n_full = i // MOBA_TRIP + (rem > MOBA_TRIP // 2).astype(jnp.int32)
    n_half = ((rem > 0) & (rem <= MOBA_TRIP // 2)).astype(jnp.int32)
    half0 = n_full * trip

    def pass1(size, base):
        def body(t, mx):
            r0 = pl.multiple_of(base + t * size, half)
            out = []
            for j in range(MOBA_HEADS):
                s = _dot(ka_ref[j, pl.ds(r0, size), :], qa[j]) * LOG2E
                s_ref[j, pl.ds(r0, size), :] = s
                out.append(jnp.maximum(mx[j], _fold_rows(s, jnp.maximum)))
            return tuple(out)
        return body

    mx = lax.fori_loop(0, n_full, pass1(trip, 0), tuple(_fold_rows(s, jnp.maximum) for s in s_own))
    mx = lax.fori_loop(0, n_half, pass1(half, half0), mx)
    for j in range(MOBA_HEADS):
        qa_ref[j] = augmented_qt(qtn_ref[0, j], i + 1, j)
    m_row = [jnp.max(mx[j], axis=0, keepdims=True) for j in range(MOBA_HEADS)]

    def pass2(size, base):
        def body(t, acc):
            r0 = pl.multiple_of(base + t * size, half)
            return tuple(acc[j] + _dot(va_ref[j, :, pl.ds(r0, size)],
                                       jnp.exp2((s_ref[j, pl.ds(r0, size), :] - m_row[j]).astype(bf16)))
                         for j in range(MOBA_HEADS))
        return body

    acc = lax.fori_loop(0, n_full, pass2(trip, 0),
                        tuple(_dot(va_ref[j, :, pl.ds(d0, MOBA_BLOCK)], jnp.exp2((s_own[j] - m_row[j]).astype(bf16)))
                              for j in range(MOBA_HEADS)))
    acc = lax.fori_loop(0, n_half, pass2(half, half0), acc)
    for j in range(MOBA_HEADS):
        o_ref[0, j] = (acc[j][:HEAD_DIM] / acc[j][HEAD_DIM:HEAD_DIM + 1]).T


def _moba_prompt(qt, kt, vt):
    batch, heads, _, seq = qt.shape
    n_blk = seq // MOBA_BLOCK
    assert n_blk % MOBA_TRIP == 0 and 2 * n_blk == HEAD_DIM and MOBA_BLOCK == 1 << MOBA_SHIFT
    assert heads % MOBA_HEADS == 0
    slopes = jnp.asarray(np.repeat(np.asarray(ALIBI_SLOPES, np.float32)[:, None], LANES, axis=1))
    tile = pl.BlockSpec((1, MOBA_HEADS, HEAD_DIM, MOBA_BLOCK), lambda b, h, i: (b, h, 0, i))
    next_tile = pl.BlockSpec((1, MOBA_HEADS, HEAD_DIM, MOBA_BLOCK),
                             lambda b, h, i: (b, h, 0, jnp.minimum(i + 1, n_blk - 1)))
    whole = pl.BlockSpec((1, MOBA_HEADS, HEAD_DIM, seq), lambda b, h, i: (b, h, 0, 0),
                         pipeline_mode=pl.Buffered(1))
    return pl.pallas_call(
        functools.partial(_moba_prompt_body, seq=seq),
        grid=(batch, heads // MOBA_HEADS, n_blk),
        in_specs=[pl.BlockSpec((H_ATT, LANES), lambda b, h, i: (0, 0)), tile, next_tile, whole, whole],
        out_specs=pl.BlockSpec((1, MOBA_HEADS, MOBA_BLOCK, HEAD_DIM), lambda b, h, i: (b, h, i, 0)),
        out_shape=jax.ShapeDtypeStruct((batch, heads, seq, HEAD_DIM), f32),
        scratch_shapes=[pltpu.VMEM((MOBA_HEADS, seq, 2 * HEAD_DIM), bf16),
                        pltpu.VMEM((MOBA_HEADS, 2 * HEAD_DIM, seq), bf16),
                        pltpu.VMEM((MOBA_HEADS, n_blk, HEAD_DIM), f32),
                        pltpu.VMEM((MOBA_HEADS, seq, MOBA_BLOCK), f32),
                        pltpu.VMEM((MOBA_HEADS, 2 * HEAD_DIM, MOBA_BLOCK), bf16)],
        compiler_params=pltpu.CompilerParams(dimension_semantics=("arbitrary",) * 3,
                                             vmem_limit_bytes=VMEM_LIMIT),
        name="moba_prompt",
    )(slopes, qt, qt, kt, vt)


SAMPLE_SEQS = 2


def _moba_sample_body(pt_ref, q_ref, k_ref, v_ref, *refs, past_len, t_new, spb):
    del pt_ref
    n_pages = past_len // PAGE_SIZE
    kp_refs, vp_refs = refs[:spb * n_pages], refs[spb * n_pages:2 * spb * n_pages]
    o_ref = refs[2 * spb * n_pages]
    own = past_len // MOBA_BLOCK
    n_sel = min(MOBA_TOPK, own)
    tq = _iota((t_new, past_len), 0)
    sk = _iota((t_new, past_len), 1)
    dist = (past_len + tq - sk).astype(f32)
    tq2 = _iota((t_new, t_new), 0)
    sk2 = _iota((t_new, t_new), 1)
    units = [(s, h) for s in range(spb) for h in range(H_ATT)]
    us = range(len(units))
    rows = [slice(s * t_new, (s + 1) * t_new) for s, _ in units]
    cols = [slice(h * HEAD_DIM, (h + 1) * HEAD_DIM) for _, h in units]
    slope = [ALIBI_SLOPES[h] for _, h in units]
    q = [q_ref[rows[u], cols[u]] for u in us]
    qb = [(q[u] * ATT_SCALE).astype(bf16) for u in us]
    kb = [jnp.concatenate([kp_refs[s * n_pages + p][0, h].astype(bf16) for p in range(n_pages)], axis=1)
          for s, h in units]
    s_raw = [_dot(qb[u], kb[u]) for u in us]
    gate = [jnp.concatenate([jnp.sum(s_raw[u][:, n * MOBA_BLOCK:(n + 1) * MOBA_BLOCK], axis=1, keepdims=True)
                             for n in range(own)], axis=1) for u in us]

    def penalty(g, n):
        gn = g[:, n:n + 1]
        rank = jnp.zeros((t_new, 1), f32)
        for m in range(own):
            if m != n:
                gm = g[:, m:m + 1]
                beats = (gm > gn) | ((gm == gn) & (m < n))
                rank = rank + jnp.where(beats, 1.0, 0.0)
        return jnp.broadcast_to(jnp.where(rank < n_sel, 0.0, NEG), (t_new, MOBA_BLOCK))

    pen = [jnp.concatenate([penalty(gate[u], n) for n in range(own)], axis=1) for u in us]
    s_past = [s_raw[u] + pen[u] - slope[u] * dist for u in us]
    s_own = [jnp.where(sk2 <= tq2,
                       _dot(qb[u], k_ref[rows[u], cols[u]].astype(bf16), NT) - slope[u] * (tq2 - sk2).astype(f32),
                       NEG) for u in us]
    m = [jnp.maximum(jnp.max(s_past[u], axis=1, keepdims=True), jnp.max(s_own[u], axis=1, keepdims=True))
         for u in us]
    p_past = [jnp.exp(s_past[u] - m[u]) for u in us]
    p_own = [jnp.exp(s_own[u] - m[u]) for u in us]
    denom = [jnp.sum(p_past[u], axis=1, keepdims=True) + jnp.sum(p_own[u], axis=1, keepdims=True) for u in us]
    vb = [jnp.concatenate([vp_refs[s * n_pages + p][0, h].astype(bf16) for p in range(n_pages)], axis=1)
          for s, h in units]
    o = [(_dot(p_own[u].astype(bf16), v_ref[rows[u], cols[u]].astype(bf16))
          + _dot(p_past[u].astype(bf16), vb[u], NT)) / denom[u] for u in us]
    for s in range(spb):
        o_ref[s * t_new:(s + 1) * t_new, :] = jnp.concatenate(o[s * H_ATT:(s + 1) * H_ATT], axis=-1)


def _moba_sample(q, k, v, cache_kt, cache_vt, page_table, *, t_new):
    n_seq, n_pages = page_table.shape
    past_len = n_pages * PAGE_SIZE
    assert past_len % MOBA_BLOCK == 0 and past_len // MOBA_BLOCK >= 1
    spb = SAMPLE_SEQS
    assert n_seq % spb == 0
    rows = pl.BlockSpec((spb * t_new, D_ATT), lambda b, pt: (b, 0))

    def page_spec(s, p):
        return pl.BlockSpec((1, H_ATT, HEAD_DIM, PAGE_SIZE),
                            lambda b, pt: (pt[(b * spb + s) * n_pages + p], 0, 0, 0))

    pages = [page_spec(s, p) for s in range(spb) for p in range(n_pages)]
    grid_spec = pltpu.PrefetchScalarGridSpec(
        num_scalar_prefetch=1,
        grid=(n_seq // spb,),
        in_specs=[rows, rows, rows] + pages * 2,
        out_specs=rows,
    )
    return pl.pallas_call(
        functools.partial(_moba_sample_body, past_len=past_len, t_new=t_new, spb=spb),
        grid_spec=grid_spec,
        out_shape=jax.ShapeDtypeStruct(q.shape, f32),
        compiler_params=pltpu.CompilerParams(dimension_semantics=("arbitrary",),
                                             vmem_limit_bytes=VMEM_LIMIT),
        name="moba_sample",
    )(page_table.reshape(-1), q, k, v, *([cache_kt] * (spb * n_pages)), *([cache_vt] * (spb * n_pages)))


def _conv_silu(cur, prev8, w):
    t = cur.shape[0]
    row8 = _iota((8, cur.shape[1]), 0)
    y = None
    for i in range(CONV_W):
        d = CONV_W - 1 - i
        if d == 0:
            term = cur
        else:
            sh = pltpu.roll(cur, d, 0)
            top = jnp.where(row8 < d, pltpu.roll(prev8, d, 0), sh[0:8])
            term = top if t == 8 else jnp.concatenate([top, sh[8:]], axis=0)
        term = term * w[i:i + 1, :]
        y = term if y is None else y + term
    return _silu(y)


def _gdn_body(uq_ref, uk_ref, uv_ref, pq_ref, pk_ref, pv_ref, cq_ref, ck_ref, cv_ref,
              wq_ref, wk_ref, wv_ref, ba_ref, gate_ref, alog_ref, dtb_ref, gg_ref, m0_ref,
              o_ref, m_ref, *, bb, tb, c):
    t = pl.program_id(1)

    @pl.when(t == 0)
    def _():
        m_ref[...] = m0_ref[...]

    n_chunks = tb // c
    lc = int(np.log2(c))
    assert 1 << lc == c
    ri = _iota((tb, tb), 0)
    ci = _iota((tb, tb), 1)
    same = jnp.right_shift(ri, lc) == jnp.right_shift(ci, lc)
    eye = ri == ci
    keep = same & (ri >= ci)
    cum = jnp.where(keep, 1.0, 0.0)
    tot = jnp.where(same, 1.0, 0.0)
    sb = min(tb, GDN_SOLVE)
    assert tb % sb == 0 and sb % c == 0
    rs_i = _iota((sb, sb), 0)
    cs_i = _iota((sb, sb), 1)
    eye_s = rs_i == cs_i
    keep_s = (jnp.right_shift(rs_i, lc) == jnp.right_shift(cs_i, lc)) & (rs_i >= cs_i)
    ident_s = jnp.where(eye_s, 1.0, 0.0)
    first = jnp.full((8, D_GDN), t, jnp.int32) == 0

    for s in range(bb):
        q = _conv_silu(uq_ref[s], jnp.where(first, cq_ref[s], pq_ref[s]), wq_ref[...])
        k = _conv_silu(uk_ref[s], jnp.where(first, ck_ref[s], pk_ref[s]), wk_ref[...])
        v = _conv_silu(uv_ref[s], jnp.where(first, cv_ref[s], pv_ref[s]), wv_ref[...])
        ba = ba_ref[s]
        beta_all = jax.nn.sigmoid(ba)
        xa = ba + dtb_ref[...]
        softplus = jnp.maximum(xa, 0.0) + jnp.log1p(jnp.exp(-jnp.abs(xa)))
        g_all = -jnp.exp(alog_ref[...]) * softplus
        gcum_all = _dot(cum, g_all, NN, HI)
        gtot_all = _dot(tot, g_all, NN, HI)
        gcum_t = gcum_all.T if tb % LANES == 0 else None
        hs = range(H_GDN)
        beta = [beta_all[:, h:h + 1] for h in hs]
        gcum = [gcum_all[:, H_GDN + h:H_GDN + h + 1] for h in hs]
        gtot = [gtot_all[:, H_GDN + h:H_GDN + h + 1] for h in hs]
        if gcum_t is None:
            grow = [jnp.sum(jnp.where(eye, gcum[h], 0.0), axis=0, keepdims=True) for h in hs]
        else:
            grow = [gcum_t[H_GDN + h:H_GDN + h + 1, :] for h in hs]
        l2n = lambda z: z * lax.rsqrt(jnp.sum(z * z, axis=-1, keepdims=True) + L2_EPS)
        qh = [l2n(q[:, h * HEAD_DIM:(h + 1) * HEAD_DIM]) * (HEAD_DIM ** -0.5) for h in hs]
        kh = [l2n(k[:, h * HEAD_DIM:(h + 1) * HEAD_DIM]) for h in hs]
        vh = [v[:, h * HEAD_DIM:(h + 1) * HEAD_DIM] for h in hs]
        eg = [jnp.exp(gcum[h]) for h in hs]
        rhs = [jnp.concatenate([beta[h] * vh[h], (beta[h] * eg[h]) * kh[h]], axis=-1) for h in hs]
        units = [(h, slice(r0, r0 + sb)) for h in hs for r0 in range(0, tb, sb)]
        decay = [jnp.exp(jnp.where(keep_s, gcum[h][rs] - grow[h][:, rs], NEG)) for h, rs in units]
        pw = [beta[h][rs] * _bdot(kh[h][rs], kh[h][rs], NT) * jnp.where(eye_s, 0.0, decay[u])
              for u, (h, rs) in enumerate(units)]
        x = [ident_s - a for a in pw]
        for _ in range(lc - 1):
            pw = [_bdot(a, a) for a in pw]
            x = [xu + _bdot(xu, a) for xu, a in zip(x, pw)]
        sol = [_bdot(x[u], rhs[h][rs]) for u, (h, rs) in enumerate(units)]
        aqk = [_bdot(qh[h][rs], kh[h][rs], NT) * decay[u] for u, (h, rs) in enumerate(units)]
        qg = [qh[h] * eg[h] for h in hs]
        kd = [kh[h] * jnp.exp(gtot[h] - gcum[h]) for h in hs]
        gdec = [jnp.exp(gtot[h]) for h in hs]
        m = [m_ref[s, h] for h in hs]
        n_sub = tb // sb
        deltas = [[] for _ in hs]
        oqs = [[] for _ in hs]
        for ch in range(n_chunks):
            sl = slice(ch * c, (ch + 1) * c)
            lsl = slice((ch * c) % sb, (ch * c) % sb + c)
            su = [sol[h * n_sub + (ch * c) // sb] for h in hs]
            both = [_bdot(jnp.concatenate([su[h][lsl, HEAD_DIM:], qg[h][sl]], axis=0), m[h]) for h in hs]
            for h in hs:
                deltas[h].append(su[h][lsl, :HEAD_DIM] - both[h][:c])
                oqs[h].append(both[h][c:])
            m = [gdec[h][ch * c:ch * c + 1, :] * m[h] + _dot(kd[h][sl], deltas[h][ch], TN) for h in hs]
        cps = sb // c
        cat = lambda parts: parts[0] if len(parts) == 1 else jnp.concatenate(parts, axis=0)
        outs = []
        for h in hs:
            m_ref[s, h] = m[h]
            intra = cat([_bdot(aqk[h * n_sub + r], cat(deltas[h][r * cps:(r + 1) * cps])) for r in range(n_sub)])
            outs.append(_rms(cat(oqs[h]) + intra))
        o = jnp.concatenate(outs, axis=-1) * gg_ref[...]
        o_ref[s] = o * _silu(gate_ref[s])


def _gdn(u, conv_pad, conv_w, ba, gate, alog_pad, dtb_pad, gg8, m0, *, bb, tb):
    batch, seq, _ = u.shape
    c = min(GDN_CHUNK, seq)
    assert batch % bb == 0 and seq % tb == 0 and tb % c == 0 and tb % 8 == 0
    cur = lambda g: pl.BlockSpec((bb, tb, D_GDN), lambda b, t: (b, t, g))
    prev = lambda g: pl.BlockSpec((bb, 8, D_GDN), lambda b, t: (b, jnp.maximum(t * (tb // 8) - 1, 0), g))
    cbuf = lambda g: pl.BlockSpec((bb, 8, D_GDN), lambda b, t: (b, 0, g))
    cw = lambda g: pl.BlockSpec((CONV_W, D_GDN), lambda b, t: (0, g))
    row = lambda width: pl.BlockSpec((1, width), lambda b, t: (0, 0))
    state = pl.BlockSpec((bb, H_GDN, HEAD_DIM, HEAD_DIM), lambda b, t: (b, 0, 0, 0))
    return pl.pallas_call(
        functools.partial(_gdn_body, bb=bb, tb=tb, c=c),
        grid=(batch // bb, seq // tb),
        in_specs=[cur(0), cur(1), cur(2), prev(0), prev(1), prev(2), cbuf(0), cbuf(1), cbuf(2),
                  cw(0), cw(1), cw(2),
                  pl.BlockSpec((bb, tb, LANES), lambda b, t: (b, t, 0)),
                  pl.BlockSpec((bb, tb, D_GDN), lambda b, t: (b, t, 0)),
                  row(LANES), row(LANES), row(D_GDN), state],
        out_specs=[pl.BlockSpec((bb, tb, D_GDN), lambda b, t: (b, t, 0)), state],
        out_shape=[jax.ShapeDtypeStruct((batch, seq, D_GDN), f32),
                   jax.ShapeDtypeStruct((batch, H_GDN, HEAD_DIM, HEAD_DIM), f32)],
        compiler_params=pltpu.CompilerParams(dimension_semantics=("arbitrary",) * 2,
                                             vmem_limit_bytes=VMEM_LIMIT),
        name="gdn",
    )(u, u, u, u, u, u, conv_pad, conv_pad, conv_pad, conv_w, conv_w, conv_w,
      ba, gate, alog_pad, dtb_pad, gg8, m0)


FF_CHUNK = 256


def _out_ffn_body(x_ref, oa_ref, og_ref, ag_ref, wo_ref, nf_ref, wg_ref, wu_ref, wd_ref, y_ref):
    oa = jnp.concatenate([oa_ref[0, h] for h in range(H_ATT)], axis=-1)
    oa = _rms(oa) * ag_ref[...]
    mix = jnp.concatenate([oa, og_ref[...]], axis=-1).astype(bf16)
    hid = x_ref[...] + jnp.dot(mix, wo_ref[...], preferred_element_type=f32)
    ub = (_rms(hid) * nf_ref[...]).astype(bf16)
    d_ff = wg_ref.shape[1]
    acc = hid
    for cidx in range(d_ff // FF_CHUNK):
        sl = slice(cidx * FF_CHUNK, (cidx + 1) * FF_CHUNK)
        a = jnp.dot(ub, wg_ref[:, sl], preferred_element_type=f32)
        b = jnp.dot(ub, wu_ref[:, sl], preferred_element_type=f32)
        acc = acc + jnp.dot((_silu(a) * b).astype(bf16), wd_ref[sl, :], preferred_element_type=f32)
    y_ref[...] = acc


def _out_ffn(x2d, o_att, o_gdn, ag, wo, nf, wg, wu, wd, *, tm):
    n = x2d.shape[0]
    batch, _, seq, _ = o_att.shape
    assert n == batch * seq and seq % tm == 0 and wg.shape[1] % FF_CHUNK == 0
    spb = seq // tm
    full = lambda shape: _resident(shape)
    rows = lambda width: pl.BlockSpec((tm, width), lambda i: (i, 0))
    return pl.pallas_call(
        _out_ffn_body,
        grid=(n // tm,),
        in_specs=[rows(D_MODEL),
                  pl.BlockSpec((1, H_ATT, tm, HEAD_DIM), lambda i: (i // spb, 0, i % spb, 0)),
                  rows(D_GDN), full((1, D_ATT)), full(wo.shape), full((1, D_MODEL)),
                  full(wg.shape), full(wu.shape), full(wd.shape)],
        out_specs=rows(D_MODEL),
        out_shape=jax.ShapeDtypeStruct((n, D_MODEL), f32),
        compiler_params=pltpu.CompilerParams(dimension_semantics=("arbitrary",),
                                             vmem_limit_bytes=VMEM_LIMIT),
        name="out_ffn",
    )(x2d, o_att, o_gdn, ag, wo, nf, wg, wu, wd)


def kernel(x_prompt, x_sample, cache_k, cache_v, state_ssm, state_conv, page_table, norm_mix, w_in,
           q_norm_g, k_norm_g, attn_out_g, conv_w, a_log, dt_bias, gdn_out_g, w_out, norm_ffn,
           w_gate, w_up, w_down):
    depth = w_in.shape[0]
    assert depth == 1
    batch, seq, _ = x_prompt.shape
    n_seq, t_new, _ = x_sample.shape

    w = w_in[0]
    c0, c1, c2 = 3 * D_ATT, 3 * D_ATT + 3 * D_GDN, 3 * D_ATT + 4 * D_GDN
    wqkv = w[:, :c0].astype(bf16)
    wqkvt = w.T[:c0].astype(bf16)
    wg = w[:, c0:c1].astype(bf16)
    wgate = w[:, c1:c2].astype(bf16)
    wba = jnp.pad(w[:, c2:], ((0, 0), (0, LANES - 2 * H_GDN))).astype(bf16)
    nm = norm_mix[0].reshape(1, D_MODEL)
    qg = q_norm_g[0].reshape(1, HEAD_DIM)
    qgc = q_norm_g[0].reshape(HEAD_DIM, 1)
    kg = k_norm_g[0].reshape(1, HEAD_DIM)
    kgc = k_norm_g[0].reshape(HEAD_DIM, 1)
    ag = attn_out_g[0].reshape(1, D_ATT)
    nf = norm_ffn[0].reshape(1, D_MODEL)
    cw = conv_w[0]
    pad_lo = lambda vec: jnp.pad(vec.astype(f32), (H_GDN, LANES - 2 * H_GDN)).reshape(1, LANES)
    alog_pad = pad_lo(a_log[0])
    dtb_pad = pad_lo(dt_bias[0])
    gg8 = jnp.tile(gdn_out_g[0].astype(f32), H_GDN).reshape(1, D_GDN)
    wo = w_out[0].astype(bf16)
    wgt = w_gate[0].astype(bf16)
    wup = w_up[0].astype(bf16)
    wdn = w_down[0].astype(bf16)

    xp = x_prompt.reshape(batch * seq, D_MODEL)
    qtp, ktp, vtp, gp, gatep, bap = _in_proj_prompt(xp, nm, wqkvt, wg, wgate, wba, qgc, kgc,
                                                    batch=batch, seq=seq, tm=512)
    oap = _moba_prompt(qtp, ktp, vtp)
    conv0 = jnp.zeros((batch, 8, 3 * D_GDN), f32)
    ssm0 = jnp.zeros((batch, H_GDN, HEAD_DIM, HEAD_DIM), f32)
    ogp, ssm_p = _gdn(gp.reshape(batch, seq, 3 * D_GDN), conv0, cw, bap.reshape(batch, seq, LANES),
                      gatep.reshape(batch, seq, D_GDN), alog_pad, dtb_pad, gg8, ssm0, bb=1, tb=256)
    yp = _out_ffn(xp, oap, ogp.reshape(batch * seq, D_GDN), ag, wo, nf, wgt, wup, wdn, tm=512)

    ns = n_seq * t_new
    xs = x_sample.reshape(ns, D_MODEL)
    qs, ks, vs, gs, gates, bas = _in_proj_sample(xs, nm, wqkv, wg, wgate, wba, qg, kg, tm=256)
    cache_kt = jnp.swapaxes(cache_k[0], -1, -2)
    cache_vt = jnp.swapaxes(cache_v[0], -1, -2)
    oas = _moba_sample(qs, ks, vs, cache_kt, cache_vt, page_table, t_new=t_new)
    conv_s = jnp.pad(state_conv[0], ((0, 0), (8 - (CONV_W - 1), 0), (0, 0)))
    ogs, ssm_s = _gdn(gs.reshape(n_seq, t_new, 3 * D_GDN), conv_s, cw, bas.reshape(n_seq, t_new, LANES),
                      gates.reshape(n_seq, t_new, D_GDN), alog_pad, dtb_pad, gg8, state_ssm[0],
                      bb=4, tb=t_new)
    to_heads = lambda a: a.reshape(n_seq, t_new, H_ATT, HEAD_DIM).transpose(0, 2, 1, 3)
    oas_hm = oas.reshape(1, ns, H_ATT, HEAD_DIM).transpose(0, 2, 1, 3)
    ys = _out_ffn(xs, oas_hm, ogs.reshape(ns, D_GDN), ag, wo, nf, wgt, wup, wdn, tm=512)

    gp3 = gp.reshape(batch, seq, 3 * D_GDN)
    gs3 = gs.reshape(n_seq, t_new, 3 * D_GDN)
    return (yp.reshape(batch, seq, D_MODEL), ys.reshape(n_seq, t_new, D_MODEL),
            jnp.swapaxes(ktp, -1, -2)[None], jnp.swapaxes(vtp, -1, -2)[None],
            to_heads(ks)[None], to_heads(vs)[None],
            ssm_p[None], ssm_s[None],
            gp3[:, seq - (CONV_W - 1):][None], gs3[:, t_new - (CONV_W - 1):][None])
```

```python
import functools

import numpy as np
import jax
import jax.numpy as jnp
from jax import lax
from jax.experimental import pallas as pl
from jax.experimental.pallas import tpu as pltpu

f32 = jnp.float32
bf16 = jnp.bfloat16

D_MODEL = 1024
HEAD_DIM = 64
H_ATT = 8
H_GDN = 8
D_ATT = H_ATT * HEAD_DIM
D_GDN = H_GDN * HEAD_DIM
MOBA_BLOCK = 256
MOBA_SHIFT = 8
MOBA_TOPK = 3
GDN_CHUNK = 64
GDN_SOLVE = 128
CONV_W = 4
PAGE_SIZE = 128
RMS_EPS = 1e-6
L2_EPS = 1e-6
ATT_SCALE = HEAD_DIM ** -0.5
LOG2E = float(np.log2(np.e))
NEG = -1e30
LANES = 128
VMEM_LIMIT = 56 * 1024 * 1024

HI = lax.Precision.HIGHEST
NN = (((1,), (0,)), ((), ()))
NT = (((1,), (1,)), ((), ()))
TN = (((0,), (0,)), ((), ()))

ALIBI_SLOPES = [2.0 ** (-8.0 * (i + 1) / H_ATT) for i in range(H_ATT)]


def _dot(a, b, dims=NN, precision=None):
    return lax.dot_general(a, b, dims, precision=precision, preferred_element_type=f32)


def _bdot(a, b, dims=NN):
    return lax.dot_general(a.astype(bf16), b.astype(bf16), dims, preferred_element_type=f32)


def _rms(x, axis=-1):
    return x * lax.rsqrt(jnp.mean(x * x, axis=axis, keepdims=True) + RMS_EPS)


def _silu(x):
    return x * jax.nn.sigmoid(x)


def _iota(shape, dim):
    return lax.broadcasted_iota(jnp.int32, shape, dim)


def _resident(shape):
    return pl.BlockSpec(shape, lambda i: (0,) * len(shape), pipeline_mode=pl.Buffered(1))


def _in_proj_prompt_body(x_ref, nm_ref, wqkvt_ref, wg_ref, wgate_ref, wba_ref, qgc_ref, kgc_ref,
                         qt_ref, kt_ref, vt_ref, g_ref, gate_ref, ba_ref):
    x = x_ref[...]
    xb = (_rms(x) * nm_ref[...]).astype(bf16)
    z = _dot(wqkvt_ref[...], xb, NT)
    for h in range(H_ATT):
        lo, hi = h * HEAD_DIM, (h + 1) * HEAD_DIM
        qt_ref[0, h] = _rms(z[lo:hi, :], axis=0) * qgc_ref[...]
        kt_ref[0, h] = _rms(z[D_ATT + lo:D_ATT + hi, :], axis=0) * kgc_ref[...]
        vt_ref[0, h] = z[2 * D_ATT + lo:2 * D_ATT + hi, :]
    g_ref[...] = jnp.dot(xb, wg_ref[...], preferred_element_type=f32)
    gate_ref[...] = jnp.dot(xb, wgate_ref[...], preferred_element_type=f32)
    ba_ref[...] = jnp.dot(xb, wba_ref[...], preferred_element_type=f32)


def _in_proj_prompt(x2d, nm, wqkvt, wg, wgate, wba, qgc, kgc, *, batch, seq, tm):
    n = x2d.shape[0]
    assert n == batch * seq and seq % tm == 0
    spb = seq // tm
    full = lambda shape: _resident(shape)
    rows = lambda width: pl.BlockSpec((tm, width), lambda i: (i, 0))
    t_spec = pl.BlockSpec((1, H_ATT, HEAD_DIM, tm), lambda i: (i // spb, 0, 0, i % spb))
    t_shape = jax.ShapeDtypeStruct((batch, H_ATT, HEAD_DIM, seq), f32)
    return pl.pallas_call(
        _in_proj_prompt_body,
        grid=(n // tm,),
        in_specs=[rows(D_MODEL), full((1, D_MODEL)), full(wqkvt.shape), full(wg.shape),
                  full(wgate.shape), full(wba.shape), full((HEAD_DIM, 1)), full((HEAD_DIM, 1))],
        out_specs=[t_spec, t_spec, t_spec, rows(3 * D_GDN), rows(D_GDN), rows(LANES)],
        out_shape=[t_shape, t_shape, t_shape,
                   jax.ShapeDtypeStruct((n, 3 * D_GDN), f32),
                   jax.ShapeDtypeStruct((n, D_GDN), f32),
                   jax.ShapeDtypeStruct((n, LANES), f32)],
        compiler_params=pltpu.CompilerParams(dimension_semantics=("arbitrary",),
                                             vmem_limit_bytes=VMEM_LIMIT),
        name="in_proj_prompt",
    )(x2d, nm, wqkvt, wg, wgate, wba, qgc, kgc)


def _in_proj_sample_body(x_ref, nm_ref, wqkv_ref, wg_ref, wgate_ref, wba_ref, qg_ref, kg_ref,
                         q_ref, k_ref, v_ref, g_ref, gate_ref, ba_ref):
    x = x_ref[...]
    xb = (_rms(x) * nm_ref[...]).astype(bf16)
    z = jnp.dot(xb, wqkv_ref[...], preferred_element_type=f32)
    qs, ks = [], []
    for h in range(H_ATT):
        lo, hi = h * HEAD_DIM, (h + 1) * HEAD_DIM
        qs.append(_rms(z[:, lo:hi]) * qg_ref[...])
        ks.append(_rms(z[:, D_ATT + lo:D_ATT + hi]) * kg_ref[...])
    q_ref[...] = jnp.concatenate(qs, axis=-1)
    k_ref[...] = jnp.concatenate(ks, axis=-1)
    v_ref[...] = z[:, 2 * D_ATT:]
    g_ref[...] = jnp.dot(xb, wg_ref[...], preferred_element_type=f32)
    gate_ref[...] = jnp.dot(xb, wgate_ref[...], preferred_element_type=f32)
    ba_ref[...] = jnp.dot(xb, wba_ref[...], preferred_element_type=f32)


def _in_proj_sample(x2d, nm, wqkv, wg, wgate, wba, qg, kg, *, tm):
    n = x2d.shape[0]
    assert n % tm == 0
    full = lambda shape: _resident(shape)
    rows = lambda width: pl.BlockSpec((tm, width), lambda i: (i, 0))
    tok = jax.ShapeDtypeStruct((n, D_ATT), f32)
    return pl.pallas_call(
        _in_proj_sample_body,
        grid=(n // tm,),
        in_specs=[rows(D_MODEL), full((1, D_MODEL)), full(wqkv.shape), full(wg.shape),
                  full(wgate.shape), full(wba.shape), full((1, HEAD_DIM)), full((1, HEAD_DIM))],
        out_specs=[rows(D_ATT), rows(D_ATT), rows(D_ATT), rows(3 * D_GDN), rows(D_GDN), rows(LANES)],
        out_shape=[tok, tok, tok,
                   jax.ShapeDtypeStruct((n, 3 * D_GDN), f32),
                   jax.ShapeDtypeStruct((n, D_GDN), f32),
                   jax.ShapeDtypeStruct((n, LANES), f32)],
        compiler_params=pltpu.CompilerParams(dimension_semantics=("arbitrary",),
                                             vmem_limit_bytes=VMEM_LIMIT),
        name="in_proj_sample",
    )(x2d, nm, wqkv, wg, wgate, wba, qg, kg)


def _fold_rows(s, op):
    parts = [s[r0:r0 + 8] for r0 in range(0, s.shape[0], 8)]
    while len(parts) > 1:
        parts = [op(parts[k], parts[k + 1]) for k in range(0, len(parts), 2)]
    return parts[0]


MOBA_HEADS = 2
MOBA_TRIP = 8


def _moba_prompt_body(slope_ref, qt_ref, qtn_ref, kt_ref, vt_ref, o_ref, ka_ref, va_ref, kmean_ref, s_ref,
                      qa_ref, *, seq):
    hg = pl.program_id(1)
    i = pl.program_id(2)
    n_blk = seq // MOBA_BLOCK
    trip = MOBA_TRIP * MOBA_BLOCK
    slopes = [slope_ref[pl.ds(hg * MOBA_HEADS + j, 1), 0:1] for j in range(MOBA_HEADS)]
    nrow = _iota((n_blk, MOBA_BLOCK), 0)
    row8 = _iota((8, MOBA_BLOCK), 0)

    def augmented_qt(qt, tile, j):
        tvec = jnp.full((n_blk, MOBA_BLOCK), tile, jnp.int32)
        gt = _dot(kmean_ref[j], qt, NN, HI)
        gt = jnp.where(nrow < tvec, gt, -jnp.inf)
        ranks = []
        for g0 in range(0, n_blk, 8):
            grp = gt[g0:g0 + 8, :]
            rank = jnp.zeros((8, MOBA_BLOCK), f32)
            for m in range(n_blk):
                gm = gt[m:m + 1, :]
                if m < g0:
                    beats = gm >= grp
                elif m >= g0 + 8:
                    beats = gm > grp
                else:
                    beats = (gm > grp) | ((gm == grp) & (row8 > m - g0))
                rank = rank + jnp.where(beats, 1.0, 0.0)
            ranks.append(rank)
        sel = (nrow < tvec) & (jnp.concatenate(ranks, axis=0) < MOBA_TOPK)
        shift = -(slopes[j] * MOBA_BLOCK) * tile.astype(f32)
        c_t = jnp.where(nrow < 2, 1.0, jnp.where(nrow == 2, shift, 0.0))
        return jnp.concatenate([qt * ATT_SCALE, jnp.where(sel, 0.0, NEG), c_t], axis=0).astype(bf16)

    @pl.when(i == 0)
    def _():
        row = _iota((HEAD_DIM, seq), 0)
        pos = _iota((HEAD_DIM, seq), 1)
        blk = jnp.right_shift(pos, MOBA_SHIFT)
        jloc = jnp.bitwise_and(pos, MOBA_BLOCK - 1)
        onehot = jnp.where(row == blk, 1.0, 0.0)
        ones_row = jnp.where(row == 0, 1.0, 0.0).astype(bf16)
        eye_b = (_iota((MOBA_BLOCK, MOBA_BLOCK), 0) == _iota((MOBA_BLOCK, MOBA_BLOCK), 1)).astype(bf16)
        for j in range(MOBA_HEADS):
            kt = kt_ref[0, j]
            e = jnp.where(row == n_blk, slopes[j] * jloc.astype(f32), onehot)
            e = jnp.where(row == n_blk + 1, (slopes[j] * MOBA_BLOCK) * blk.astype(f32), e)
            e = jnp.where(row == n_blk + 2, 1.0, e)
            ka_t = jnp.concatenate([kt.astype(bf16), e.astype(bf16)], axis=0)
            for n in range(n_blk):
                blk_cols = slice(n * MOBA_BLOCK, (n + 1) * MOBA_BLOCK)
                ka_ref[j, blk_cols, :] = _dot(eye_b, ka_t[:, blk_cols], NT).astype(bf16)
            va_ref[j, 0:HEAD_DIM, :] = vt_ref[0, j].astype(bf16)
            va_ref[j, HEAD_DIM:, :] = ones_row
            sums = [jnp.sum(kt[:, n * MOBA_BLOCK:(n + 1) * MOBA_BLOCK], axis=1, keepdims=True)
                    for n in range(n_blk)]
            kmean_ref[j] = (jnp.concatenate(sums, axis=1) * (1.0 / MOBA_BLOCK)).T
        for j in range(MOBA_HEADS):
            qa_ref[j] = augmented_qt(qt_ref[0, j], i, j)

    arow = _iota((2 * HEAD_DIM, MOBA_BLOCK), 0)
    pen_rows = (arow >= HEAD_DIM) & (arow < HEAD_DIM + n_blk)
    causal = _iota((MOBA_BLOCK, MOBA_BLOCK), 0) <= _iota((MOBA_BLOCK, MOBA_BLOCK), 1)
    d0 = pl.multiple_of(i * MOBA_BLOCK, MOBA_BLOCK)
    qa = [qa_ref[j] for j in range(MOBA_HEADS)]
    s_own = []
    for j in range(MOBA_HEADS):
        qo = jnp.where(pen_rows, jnp.zeros_like(qa[j]), qa[j])
        s_own.append(jnp.where(causal, _dot(ka_ref[j, pl.ds(d0, MOBA_BLOCK), :], qo) * LOG2E, NEG))
    half = trip // 2
    rem = i % MOBA_TRIP
    n_full = i // MOBA_TRIP + (rem > MOBA_TRIP // 2).astype(jnp.int32)
    n_half = ((rem > 0) & (rem <= MOBA_TRIP // 2)).astype(jnp.int32)
    half0 = n_full * trip

    def pass1(size, base):
        def body(t, mx):
            r0 = pl.multiple_of(base + t * size, half)
            out = []
            for j in range(MOBA_HEADS):
                s = _dot(ka_ref[j, pl.ds(r0, size), :], qa[j]) * LOG2E
                s_ref[j, pl.ds(r0, size), :] = s
                out.append(jnp.maximum(mx[j], _fold_rows(s, jnp.maximum)))
            return tuple(out)
        return body

    mx = lax.fori_loop(0, n_full, pass1(trip, 0), tuple(_fold_rows(s, jnp.maximum) for s in s_own))
    mx = lax.fori_loop(0, n_half, pass1(half, half0), mx)
    for j in range(MOBA_HEADS):
        qa_ref[j] = augmented_qt(qtn_ref[0, j], i + 1, j)
    m_row = [jnp.max(mx[j], axis=0, keepdims=True) for j in range(MOBA_HEADS)]

    def pass2(size, base):
        def body(t, acc):
            r0 = pl.multiple_of(base + t * size, half)
            return tuple(acc[j] + _dot(va_ref[j, :, pl.ds(r0, size)],
                                       jnp.exp2((s_ref[j, pl.ds(r0, size), :] - m_row[j]).astype(bf16)))
                         for j in range(MOBA_HEADS))
        return body

    acc = lax.fori_loop(0, n_full, pass2(trip, 0),
                        tuple(_dot(va_ref[j, :, pl.ds(d0, MOBA_BLOCK)], jnp.exp2((s_own[j] - m_row[j]).astype(bf16)))
                              for j in range(MOBA_HEADS)))
    acc = lax.fori_loop(0, n_half, pass2(half, half0), acc)
    for j in range(MOBA_HEADS):
        o_ref[0, j] = (acc[j][:HEAD_DIM] / acc[j][HEAD_DIM:HEAD_DIM + 1]).T


def _moba_prompt(qt, kt, vt):
    batch, heads, _, seq = qt.shape
    n_blk = seq // MOBA_BLOCK
    assert n_blk % MOBA_TRIP == 0 and 2 * n_blk == HEAD_DIM and MOBA_BLOCK == 1 << MOBA_SHIFT
    assert heads % MOBA_HEADS == 0
    slopes = jnp.asarray(np.repeat(np.asarray(ALIBI_SLOPES, np.float32)[:, None], LANES, axis=1))
    tile = pl.BlockSpec((1, MOBA_HEADS, HEAD_DIM, MOBA_BLOCK), lambda b, h, i: (b, h, 0, i))
    next_tile = pl.BlockSpec((1, MOBA_HEADS, HEAD_DIM, MOBA_BLOCK),
                             lambda b, h, i: (b, h, 0, jnp.minimum(i + 1, n_blk - 1)))
    whole = pl.BlockSpec((1, MOBA_HEADS, HEAD_DIM, seq), lambda b, h, i: (b, h, 0, 0),
                         pipeline_mode=pl.Buffered(1))
    return pl.pallas_call(
        functools.partial(_moba_prompt_body, seq=seq),
        grid=(batch, heads // MOBA_HEADS, n_blk),
        in_specs=[pl.BlockSpec((H_ATT, LANES), lambda b, h, i: (0, 0)), tile, next_tile, whole, whole],
        out_specs=pl.BlockSpec((1, MOBA_HEADS, MOBA_BLOCK, HEAD_DIM), lambda b, h, i: (b, h, i, 0)),
        out_shape=jax.ShapeDtypeStruct((batch, heads, seq, HEAD_DIM), f32),
        scratch_shapes=[pltpu.VMEM((MOBA_HEADS, seq, 2 * HEAD_DIM), bf16),
                        pltpu.VMEM((MOBA_HEADS, 2 * HEAD_DIM, seq), bf16),
                        pltpu.VMEM((MOBA_HEADS, n_blk, HEAD_DIM), f32),
                        pltpu.VMEM((MOBA_HEADS, seq, MOBA_BLOCK), f32),
                        pltpu.VMEM((MOBA_HEADS, 2 * HEAD_DIM, MOBA_BLOCK), bf16)],
        compiler_params=pltpu.CompilerParams(dimension_semantics=("arbitrary",) * 3,
                                             vmem_limit_bytes=VMEM_LIMIT),
        name="moba_prompt",
    )(slopes, qt, qt, kt, vt)


SAMPLE_SEQS = 2


def _moba_sample_body(pt_ref, q_ref, k_ref, v_ref, *refs, past_len, t_new, spb):
    del pt_ref
    n_pages = past_len // PAGE_SIZE
    kp_refs, vp_refs = refs[:spb * n_pages], refs[spb * n_pages:2 * spb * n_pages]
    o_ref = refs[2 * spb * n_pages]
    own = past_len // MOBA_BLOCK
    n_sel = min(MOBA_TOPK, own)
    tq = _iota((t_new, past_len), 0)
    sk = _iota((t_new, past_len), 1)
    dist = (past_len + tq - sk).astype(f32)
    tq2 = _iota((t_new, t_new), 0)
    sk2 = _iota((t_new, t_new), 1)
    units = [(s, h) for s in range(spb) for h in range(H_ATT)]
    us = range(len(units))
    rows = [slice(s * t_new, (s + 1) * t_new) for s, _ in units]
    cols = [slice(h * HEAD_DIM, (h + 1) * HEAD_DIM) for _, h in units]
    slope = [ALIBI_SLOPES[h] for _, h in units]
    q = [q_ref[rows[u], cols[u]] for u in us]
    qb = [(q[u] * ATT_SCALE).astype(bf16) for u in us]
    kb = [jnp.concatenate([kp_refs[s * n_pages + p][0, h].astype(bf16) for p in range(n_pages)], axis=1)
          for s, h in units]
    s_raw = [_dot(qb[u], kb[u]) for u in us]
    gate = [jnp.concatenate([jnp.sum(s_raw[u][:, n * MOBA_BLOCK:(n + 1) * MOBA_BLOCK], axis=1, keepdims=True)
                             for n in range(own)], axis=1) for u in us]

    def penalty(g, n):
        gn = g[:, n:n + 1]
        rank = jnp.zeros((t_new, 1), f32)
        for m in range(own):
            if m != n:
                gm = g[:, m:m + 1]
                beats = (gm > gn) | ((gm == gn) & (m < n))
                rank = rank + jnp.where(beats, 1.0, 0.0)
        return jnp.broadcast_to(jnp.where(rank < n_sel, 0.0, NEG), (t_new, MOBA_BLOCK))

    pen = [jnp.concatenate([penalty(gate[u], n) for n in range(own)], axis=1) for u in us]
    s_past = [s_raw[u] + pen[u] - slope[u] * dist for u in us]
    s_own = [jnp.where(sk2 <= tq2,
                       _dot(qb[u], k_ref[rows[u], cols[u]].astype(bf16), NT) - slope[u] * (tq2 - sk2).astype(f32),
                       NEG) for u in us]
    m = [jnp.maximum(jnp.max(s_past[u], axis=1, keepdims=True), jnp.max(s_own[u], axis=1, keepdims=True))
         for u in us]
    p_past = [jnp.exp(s_past[u] - m[u]) for u in us]
    p_own = [jnp.exp(s_own[u] - m[u]) for u in us]
    denom = [jnp.sum(p_past[u], axis=1, keepdims=True) + jnp.sum(p_own[u], axis=1, keepdims=True) for u in us]
    vb = [jnp.concatenate([vp_refs[s * n_pages + p][0, h].astype(bf16) for p in range(n_pages)], axis=1)
          for s, h in units]
    o = [(_dot(p_own[u].astype(bf16), v_ref[rows[u], cols[u]].astype(bf16))
          + _dot(p_past[u].astype(bf16), vb[u], NT)) / denom[u] for u in us]
    for s in range(spb):
        o_ref[s * t_new:(s + 1) * t_new, :] = jnp.concatenate(o[s * H_ATT:(s + 1) * H_ATT], axis=-1)


def _moba_sample(q, k, v, cache_kt, cache_vt, page_table, *, t_new):
    n_seq, n_pages = page_table.shape
    past_len = n_pages * PAGE_SIZE
    assert past_len % MOBA_BLOCK == 0 and past_len // MOBA_BLOCK >= 1
    spb = SAMPLE_SEQS
    assert n_seq % spb == 0
    rows = pl.BlockSpec((spb * t_new, D_ATT), lambda b, pt: (b, 0))

    def page_spec(s, p):
        return pl.BlockSpec((1, H_ATT, HEAD_DIM, PAGE_SIZE),
                            lambda b, pt: (pt[(b * spb + s) * n_pages + p], 0, 0, 0))

    pages = [page_spec(s, p) for s in range(spb) for p in range(n_pages)]
    grid_spec = pltpu.PrefetchScalarGridSpec(
        num_scalar_prefetch=1,
        grid=(n_seq // spb,),
        in_specs=[rows, rows, rows] + pages * 2,
        out_specs=rows,
    )
    return pl.pallas_call(
        functools.partial(_moba_sample_body, past_len=past_len, t_new=t_new, spb=spb),
        grid_spec=grid_spec,
        out_shape=jax.ShapeDtypeStruct(q.shape, f32),
        compiler_params=pltpu.CompilerParams(dimension_semantics=("arbitrary",),
                                             vmem_limit_bytes=VMEM_LIMIT),
        name="moba_sample",
    )(page_table.reshape(-1), q, k, v, *([cache_kt] * (spb * n_pages)), *([cache_vt] * (spb * n_pages)))


def _conv_silu(cur, prev8, w):
    t = cur.shape[0]
    row8 = _iota((8, cur.shape[1]), 0)
    y = None
    for i in range(CONV_W):
        d = CONV_W - 1 - i
        if d == 0:
            term = cur
        else:
            sh = pltpu.roll(cur, d, 0)
            top = jnp.where(row8 < d, pltpu.roll(prev8, d, 0), sh[0:8])
            term = top if t == 8 else jnp.concatenate([top, sh[8:]], axis=0)
        term = term * w[i:i + 1, :]
        y = term if y is None else y + term
    return _silu(y)


def _gdn_body(uq_ref, uk_ref, uv_ref, pq_ref, pk_ref, pv_ref, cq_ref, ck_ref, cv_ref,
              wq_ref, wk_ref, wv_ref, ba_ref, gate_ref, alog_ref, dtb_ref, gg_ref, m0_ref,
              o_ref, m_ref, *, bb, tb, c):
    t = pl.program_id(1)

    @pl.when(t == 0)
    def _():
        m_ref[...] = m0_ref[...]

    n_chunks = tb // c
    lc = int(np.log2(c))
    assert 1 << lc == c
    ri = _iota((tb, tb), 0)
    ci = _iota((tb, tb), 1)
    same = jnp.right_shift(ri, lc) == jnp.right_shift(ci, lc)
    eye = ri == ci
    keep = same & (ri >= ci)
    cum = jnp.where(keep, 1.0, 0.0)
    tot = jnp.where(same, 1.0, 0.0)
    sb = min(tb, GDN_SOLVE)
    assert tb % sb == 0 and sb % c == 0
    rs_i = _iota((sb, sb), 0)
    cs_i = _iota((sb, sb), 1)
    eye_s = rs_i == cs_i
    keep_s = (jnp.right_shift(rs_i, lc) == jnp.right_shift(cs_i, lc)) & (rs_i >= cs_i)
    ident_s = jnp.where(eye_s, 1.0, 0.0)
    split = lambda z: [z[:, h * HEAD_DIM:(h + 1) * HEAD_DIM] for h in range(H_GDN)]
    if tb >= LANES:
        head_ones = jnp.where(_iota((D_GDN, D_GDN), 0) // HEAD_DIM == _iota((D_GDN, D_GDN), 1) // HEAD_DIM,
                              1.0, 0.0)
        l2_heads = lambda z: split(z * lax.rsqrt(_bdot(z * z, head_ones) + L2_EPS))
        rms_cat = lambda o: o * lax.rsqrt(_bdot(o * o, head_ones) * (1.0 / HEAD_DIM) + RMS_EPS)
        rms_heads = lambda parts: rms_cat(jnp.concatenate(parts, axis=-1))
    else:
        l2_heads = lambda z: [zh * lax.rsqrt(jnp.sum(zh * zh, axis=-1, keepdims=True) + L2_EPS) for zh in split(z)]
        rms_heads = lambda parts: jnp.concatenate([_rms(p) for p in parts], axis=-1)
    first = jnp.full((8, D_GDN), t, jnp.int32) == 0

    for s in range(bb):
        q = _conv_silu(uq_ref[s], jnp.where(first, cq_ref[s], pq_ref[s]), wq_ref[...])
        k = _conv_silu(uk_ref[s], jnp.where(first, ck_ref[s], pk_ref[s]), wk_ref[...])
        v = _conv_silu(uv_ref[s], jnp.where(first, cv_ref[s], pv_ref[s]), wv_ref[...])
        ba = ba_ref[s]
        beta_all = jax.nn.sigmoid(ba)
        xa = ba + dtb_ref[...]
        softplus = jnp.maximum(xa, 0.0) + jnp.log1p(jnp.exp(-jnp.abs(xa)))
        g_all = -jnp.exp(alog_ref[...]) * softplus
        gcum_all = _dot(cum, g_all, NN, HI)
        gtot_all = _dot(tot, g_all, NN, HI)
        gcum_t = gcum_all.T if tb % LANES == 0 else None
        hs = range(H_GDN)
        beta = [beta_all[:, h:h + 1] for h in hs]
        gcum = [gcum_all[:, H_GDN + h:H_GDN + h + 1] for h in hs]
        gtot = [gtot_all[:, H_GDN + h:H_GDN + h + 1] for h in hs]
        if gcum_t is None:
            grow = [jnp.sum(jnp.where(eye, gcum[h], 0.0), axis=0, keepdims=True) for h in hs]
        else:
            grow = [gcum_t[H_GDN + h:H_GDN + h + 1, :] for h in hs]
        qh = [z * (HEAD_DIM ** -0.5) for z in l2_heads(q)]
        kh = l2_heads(k)
        vh = [v[:, h * HEAD_DIM:(h + 1) * HEAD_DIM] for h in hs]
        eg = [jnp.exp(gcum[h]) for h in hs]
        rhs = [jnp.concatenate([beta[h] * vh[h], (beta[h] * eg[h]) * kh[h]], axis=-1) for h in hs]
        units = [(h, slice(r0, r0 + sb)) for h in hs for r0 in range(0, tb, sb)]
        decay = [jnp.exp(jnp.where(keep_s, gcum[h][rs] - grow[h][:, rs], NEG)) for h, rs in units]
        pw = [beta[h][rs] * _bdot(kh[h][rs], kh[h][rs], NT) * jnp.where(eye_s, 0.0, decay[u])
              for u, (h, rs) in enumerate(units)]
        x = [ident_s - a for a in pw]
        for _ in range(lc - 1):
            pw = [_bdot(a, a) for a in pw]
            x = [xu + _bdot(xu, a) for xu, a in zip(x, pw)]
        sol = [_bdot(x[u], rhs[h][rs]) for u, (h, rs) in enumerate(units)]
        aqk = [_bdot(qh[h][rs], kh[h][rs], NT) * decay[u] for u, (h, rs) in enumerate(units)]
        qg = [qh[h] * eg[h] for h in hs]
        kd = [kh[h] * jnp.exp(gtot[h] - gcum[h]) for h in hs]
        gdec = [jnp.exp(gtot[h]) for h in hs]
        m = [m_ref[s, h] for h in hs]
        n_sub = tb // sb
        deltas = [[] for _ in hs]
        oqs = [[] for _ in hs]
        for ch in range(n_chunks):
            sl = slice(ch * c, (ch + 1) * c)
            lsl = slice((ch * c) % sb, (ch * c) % sb + c)
            su = [sol[h * n_sub + (ch * c) // sb] for h in hs]
            both = [_bdot(jnp.concatenate([su[h][lsl, HEAD_DIM:], qg[h][sl]], axis=0), m[h]) for h in hs]
            for h in hs:
                deltas[h].append(su[h][lsl, :HEAD_DIM] - both[h][:c])
                oqs[h].append(both[h][c:])
            m = [gdec[h][ch * c:ch * c + 1, :] * m[h] + _dot(kd[h][sl], deltas[h][ch], TN) for h in hs]
        cps = sb // c
        cat = lambda parts: parts[0] if len(parts) == 1 else jnp.concatenate(parts, axis=0)
        outs = []
        for h in hs:
            m_ref[s, h] = m[h]
            intra = cat([_bdot(aqk[h * n_sub + r], cat(deltas[h][r * cps:(r + 1) * cps])) for r in range(n_sub)])
            outs.append(cat(oqs[h]) + intra)
        o_ref[s] = rms_heads(outs) * gg_ref[...] * _silu(gate_ref[s])


def _gdn(u, conv_pad, conv_w, ba, gate, alog_pad, dtb_pad, gg8, m0, *, bb, tb):
    batch, seq, _ = u.shape
    c = min(GDN_CHUNK, seq)
    assert batch % bb == 0 and seq % tb == 0 and tb % c == 0 and tb % 8 == 0
    cur = lambda g: pl.BlockSpec((bb, tb, D_GDN), lambda b, t: (b, t, g))
    prev = lambda g: pl.BlockSpec((bb, 8, D_GDN), lambda b, t: (b, jnp.maximum(t * (tb // 8) - 1, 0), g))
    cbuf = lambda g: pl.BlockSpec((bb, 8, D_GDN), lambda b, t: (b, 0, g))
    cw = lambda g: pl.BlockSpec((CONV_W, D_GDN), lambda b, t: (0, g))
    row = lambda width: pl.BlockSpec((1, width), lambda b, t: (0, 0))
    state = pl.BlockSpec((bb, H_GDN, HEAD_DIM, HEAD_DIM), lambda b, t: (b, 0, 0, 0))
    return pl.pallas_call(
        functools.partial(_gdn_body, bb=bb, tb=tb, c=c),
        grid=(batch // bb, seq // tb),
        in_specs=[cur(0), cur(1), cur(2), prev(0), prev(1), prev(2), cbuf(0), cbuf(1), cbuf(2),
                  cw(0), cw(1), cw(2),
                  pl.BlockSpec((bb, tb, LANES), lambda b, t: (b, t, 0)),
                  pl.BlockSpec((bb, tb, D_GDN), lambda b, t: (b, t, 0)),
                  row(LANES), row(LANES), row(D_GDN), state],
        out_specs=[pl.BlockSpec((bb, tb, D_GDN), lambda b, t: (b, t, 0)), state],
        out_shape=[jax.ShapeDtypeStruct((batch, seq, D_GDN), f32),
                   jax.ShapeDtypeStruct((batch, H_GDN, HEAD_DIM, HEAD_DIM), f32)],
        compiler_params=pltpu.CompilerParams(dimension_semantics=("arbitrary",) * 2,
                                             vmem_limit_bytes=VMEM_LIMIT),
        name="gdn",
    )(u, u, u, u, u, u, conv_pad, conv_pad, conv_pad, conv_w, conv_w, conv_w,
      ba, gate, alog_pad, dtb_pad, gg8, m0)


FF_CHUNK = 256


def _out_ffn_body(x_ref, oa_ref, og_ref, ag_ref, wo_ref, nf_ref, wg_ref, wu_ref, wd_ref, y_ref):
    oa = jnp.concatenate([oa_ref[0, h] for h in range(H_ATT)], axis=-1)
    oa = _rms(oa) * ag_ref[...]
    mix = jnp.concatenate([oa, og_ref[...]], axis=-1).astype(bf16)
    hid = x_ref[...] + jnp.dot(mix, wo_ref[...], preferred_element_type=f32)
    ub = (_rms(hid) * nf_ref[...]).astype(bf16)
    d_ff = wg_ref.shape[1]
    acc = hid
    for cidx in range(d_ff // FF_CHUNK):
        sl = slice(cidx * FF_CHUNK, (cidx + 1) * FF_CHUNK)
        a = jnp.dot(ub, wg_ref[:, sl], preferred_element_type=f32)
        b = jnp.dot(ub, wu_ref[:, sl], preferred_element_type=f32)
        acc = acc + jnp.dot((_silu(a) * b).astype(bf16), wd_ref[sl, :], preferred_element_type=f32)
    y_ref[...] = acc


def _out_ffn(x2d, o_att, o_gdn, ag, wo, nf, wg, wu, wd, *, tm):
    n = x2d.shape[0]
    batch, _, seq, _ = o_att.shape
    assert n == batch * seq and seq % tm == 0 and wg.shape[1] % FF_CHUNK == 0
    spb = seq // tm
    full = lambda shape: _resident(shape)
    rows = lambda width: pl.BlockSpec((tm, width), lambda i: (i, 0))
    return pl.pallas_call(
        _out_ffn_body,
        grid=(n // tm,),
        in_specs=[rows(D_MODEL),
                  pl.BlockSpec((1, H_ATT, tm, HEAD_DIM), lambda i: (i // spb, 0, i % spb, 0)),
                  rows(D_GDN), full((1, D_ATT)), full(wo.shape), full((1, D_MODEL)),
                  full(wg.shape), full(wu.shape), full(wd.shape)],
        out_specs=rows(D_MODEL),
        out_shape=jax.ShapeDtypeStruct((n, D_MODEL), f32),
        compiler_params=pltpu.CompilerParams(dimension_semantics=("arbitrary",),
                                             vmem_limit_bytes=VMEM_LIMIT),
        name="out_ffn",
    )(x2d, o_att, o_gdn, ag, wo, nf, wg, wu, wd)


def kernel(x_prompt, x_sample, cache_k, cache_v, state_ssm, state_conv, page_table, norm_mix, w_in,
           q_norm_g, k_norm_g, attn_out_g, conv_w, a_log, dt_bias, gdn_out_g, w_out, norm_ffn,
           w_gate, w_up, w_down):
    depth = w_in.shape[0]
    assert depth == 1
    batch, seq, _ = x_prompt.shape
    n_seq, t_new, _ = x_sample.shape

    w = w_in[0]
    c0, c1, c2 = 3 * D_ATT, 3 * D_ATT + 3 * D_GDN, 3 * D_ATT + 4 * D_GDN
    wqkv = w[:, :c0].astype(bf16)
    wqkvt = w.T[:c0].astype(bf16)
    wg = w[:, c0:c1].astype(bf16)
    wgate = w[:, c1:c2].astype(bf16)
    wba = jnp.pad(w[:, c2:], ((0, 0), (0, LANES - 2 * H_GDN))).astype(bf16)
    nm = norm_mix[0].reshape(1, D_MODEL)
    qg = q_norm_g[0].reshape(1, HEAD_DIM)
    qgc = q_norm_g[0].reshape(HEAD_DIM, 1)
    kg = k_norm_g[0].reshape(1, HEAD_DIM)
    kgc = k_norm_g[0].reshape(HEAD_DIM, 1)
    ag = attn_out_g[0].reshape(1, D_ATT)
    nf = norm_ffn[0].reshape(1, D_MODEL)
    cw = conv_w[0]
    pad_lo = lambda vec: jnp.pad(vec.astype(f32), (H_GDN, LANES - 2 * H_GDN)).reshape(1, LANES)
    alog_pad = pad_lo(a_log[0])
    dtb_pad = pad_lo(dt_bias[0])
    gg8 = jnp.tile(gdn_out_g[0].astype(f32), H_GDN).reshape(1, D_GDN)
    wo = w_out[0].astype(bf16)
    wgt = w_gate[0].astype(bf16)
    wup = w_up[0].astype(bf16)
    wdn = w_down[0].astype(bf16)

    xp = x_prompt.reshape(batch * seq, D_MODEL)
    qtp, ktp, vtp, gp, gatep, bap = _in_proj_prompt(xp, nm, wqkvt, wg, wgate, wba, qgc, kgc,
                                                    batch=batch, seq=seq, tm=512)
    oap = _moba_prompt(qtp, ktp, vtp)
    conv0 = jnp.zeros((batch, 8, 3 * D_GDN), f32)
    ssm0 = jnp.zeros((batch, H_GDN, HEAD_DIM, HEAD_DIM), f32)
    ogp, ssm_p = _gdn(gp.reshape(batch, seq, 3 * D_GDN), conv0, cw, bap.reshape(batch, seq, LANES),
                      gatep.reshape(batch, seq, D_GDN), alog_pad, dtb_pad, gg8, ssm0, bb=1, tb=256)
    yp = _out_ffn(xp, oap, ogp.reshape(batch * seq, D_GDN), ag, wo, nf, wgt, wup, wdn, tm=512)

    ns = n_seq * t_new
    xs = x_sample.reshape(ns, D_MODEL)
    qs, ks, vs, gs, gates, bas = _in_proj_sample(xs, nm, wqkv, wg, wgate, wba, qg, kg, tm=256)
    cache_kt = jnp.swapaxes(cache_k[0], -1, -2)
    cache_vt = jnp.swapaxes(cache_v[0], -1, -2)
    oas = _moba_sample(qs, ks, vs, cache_kt, cache_vt, page_table, t_new=t_new)
    conv_s = jnp.pad(state_conv[0], ((0, 0), (8 - (CONV_W - 1), 0), (0, 0)))
    ogs, ssm_s = _gdn(gs.reshape(n_seq, t_new, 3 * D_GDN), conv_s, cw, bas.reshape(n_seq, t_new, LANES),
                      gates.reshape(n_seq, t_new, D_GDN), alog_pad, dtb_pad, gg8, state_ssm[0],
                      bb=4, tb=t_new)
    to_heads = lambda a: a.reshape(n_seq, t_new, H_ATT, HEAD_DIM).transpose(0, 2, 1, 3)
    oas_hm = oas.reshape(1, ns, H_ATT, HEAD_DIM).transpose(0, 2, 1, 3)
    ys = _out_ffn(xs, oas_hm, ogs.reshape(ns, D_GDN), ag, wo, nf, wgt, wup, wdn, tm=512)

    gp3 = gp.reshape(batch, seq, 3 * D_GDN)
    gs3 = gs.reshape(n_seq, t_new, 3 * D_GDN)
    return (yp.reshape(batch, seq, D_MODEL), ys.reshape(n_seq, t_new, D_MODEL),
            jnp.swapaxes(ktp, -1, -2)[None], jnp.swapaxes(vtp, -1, -2)[None],
            to_heads(ks)[None], to_heads(vs)[None],
            ssm_p[None], ssm_s[None],
            gp3[:, seq - (CONV_W - 1):][None], gs3[:, t_new - (CONV_W - 1):][None])
```

```python
import functools

import numpy as np
import jax
import jax.numpy as jnp
from jax import lax
from jax.experimental import pallas as pl
from jax.experimental.pallas import tpu as pltpu

f32 = jnp.float32
bf16 = jnp.bfloat16

D_MODEL = 1024
HEAD_DIM = 64
H_ATT = 8
H_GDN = 8
D_ATT = H_ATT * HEAD_DIM
D_GDN = H_GDN * HEAD_DIM
MOBA_BLOCK = 256
MOBA_SHIFT = 8
MOBA_TOPK = 3
GDN_CHUNK = 64
GDN_SOLVE = 128
CONV_W = 4
PAGE_SIZE = 128
RMS_EPS = 1e-6
L2_EPS = 1e-6
ATT_SCALE = HEAD_DIM ** -0.5
LOG2E = float(np.log2(np.e))
NEG = -1e30
LANES = 128
VMEM_LIMIT = 56 * 1024 * 1024

HI = lax.Precision.HIGHEST
NN = (((1,), (0,)), ((), ()))
NT = (((1,), (1,)), ((), ()))
TN = (((0,), (0,)), ((), ()))

ALIBI_SLOPES = [2.0 ** (-8.0 * (i + 1) / H_ATT) for i in range(H_ATT)]


def _dot(a, b, dims=NN, precision=None):
    return lax.dot_general(a, b, dims, precision=precision, preferred_element_type=f32)


def _bdot(a, b, dims=NN):
    return lax.dot_general(a.astype(bf16), b.astype(bf16), dims, preferred_element_type=f32)


def _rms(x, axis=-1):
    return x * lax.rsqrt(jnp.mean(x * x, axis=axis, keepdims=True) + RMS_EPS)


def _silu(x):
    return x * jax.nn.sigmoid(x)


def _iota(shape, dim):
    return lax.broadcasted_iota(jnp.int32, shape, dim)


def _resident(shape):
    return pl.BlockSpec(shape, lambda i: (0,) * len(shape), pipeline_mode=pl.Buffered(1))


def _in_proj_prompt_body(x_ref, nm_ref, wqkvt_ref, wg_ref, wgate_ref, wba_ref, qgc_ref, kgc_ref,
                         qt_ref, kt_ref, vt_ref, g_ref, gate_ref, ba_ref):
    x = x_ref[...]
    xb = (_rms(x) * nm_ref[...]).astype(bf16)
    z = _dot(wqkvt_ref[...], xb, NT)
    for h in range(H_ATT):
        lo, hi = h * HEAD_DIM, (h + 1) * HEAD_DIM
        qt_ref[0, h] = _rms(z[lo:hi, :], axis=0) * qgc_ref[...]
        kt_ref[0, h] = _rms(z[D_ATT + lo:D_ATT + hi, :], axis=0) * kgc_ref[...]
        vt_ref[0, h] = z[2 * D_ATT + lo:2 * D_ATT + hi, :]
    g_ref[...] = jnp.dot(xb, wg_ref[...], preferred_element_type=f32)
    gate_ref[...] = jnp.dot(xb, wgate_ref[...], preferred_element_type=f32)
    ba_ref[...] = jnp.dot(xb, wba_ref[...], preferred_element_type=f32)


def _in_proj_prompt(x2d, nm, wqkvt, wg, wgate, wba, qgc, kgc, *, batch, seq, tm):
    n = x2d.shape[0]
    assert n == batch * seq and seq % tm == 0
    spb = seq // tm
    full = lambda shape: _resident(shape)
    rows = lambda width: pl.BlockSpec((tm, width), lambda i: (i, 0))
    t_spec = pl.BlockSpec((1, H_ATT, HEAD_DIM, tm), lambda i: (i // spb, 0, 0, i % spb))
    t_shape = jax.ShapeDtypeStruct((batch, H_ATT, HEAD_DIM, seq), f32)
    return pl.pallas_call(
        _in_proj_prompt_body,
        grid=(n // tm,),
        in_specs=[rows(D_MODEL), full((1, D_MODEL)), full(wqkvt.shape), full(wg.shape),
                  full(wgate.shape), full(wba.shape), full((HEAD_DIM, 1)), full((HEAD_DIM, 1))],
        out_specs=[t_spec, t_spec, t_spec, rows(3 * D_GDN), rows(D_GDN), rows(LANES)],
        out_shape=[t_shape, t_shape, t_shape,
                   jax.ShapeDtypeStruct((n, 3 * D_GDN), f32),
                   jax.ShapeDtypeStruct((n, D_GDN), f32),
                   jax.ShapeDtypeStruct((n, LANES), f32)],
        compiler_params=pltpu.CompilerParams(dimension_semantics=("arbitrary",),
                                             vmem_limit_bytes=VMEM_LIMIT),
        name="in_proj_prompt",
    )(x2d, nm, wqkvt, wg, wgate, wba, qgc, kgc)


def _in_proj_sample_body(x_ref, nm_ref, wqkv_ref, wg_ref, wgate_ref, wba_ref, qg_ref, kg_ref,
                         q_ref, k_ref, v_ref, g_ref, gate_ref, ba_ref):
    x = x_ref[...]
    xb = (_rms(x) * nm_ref[...]).astype(bf16)
    z = jnp.dot(xb, wqkv_ref[...], preferred_element_type=f32)
    qs, ks = [], []
    for h in range(H_ATT):
        lo, hi = h * HEAD_DIM, (h + 1) * HEAD_DIM
        qs.append(_rms(z[:, lo:hi]) * qg_ref[...])
        ks.append(_rms(z[:, D_ATT + lo:D_ATT + hi]) * kg_ref[...])
    q_ref[...] = jnp.concatenate(qs, axis=-1)
    k_ref[...] = jnp.concatenate(ks, axis=-1)
    v_ref[...] = z[:, 2 * D_ATT:]
    g_ref[...] = jnp.dot(xb, wg_ref[...], preferred_element_type=f32)
    gate_ref[...] = jnp.dot(xb, wgate_ref[...], preferred_element_type=f32)
    ba_ref[...] = jnp.dot(xb, wba_ref[...], preferred_element_type=f32)


def _in_proj_sample(x2d, nm, wqkv, wg, wgate, wba, qg, kg, *, tm):
    n = x2d.shape[0]
    assert n % tm == 0
    full = lambda shape: _resident(shape)
    rows = lambda width: pl.BlockSpec((tm, width), lambda i: (i, 0))
    tok = jax.ShapeDtypeStruct((n, D_ATT), f32)
    return pl.pallas_call(
        _in_proj_sample_body,
        grid=(n // tm,),
        in_specs=[rows(D_MODEL), full((1, D_MODEL)), full(wqkv.shape), full(wg.shape),
                  full(wgate.shape), full(wba.shape), full((1, HEAD_DIM)), full((1, HEAD_DIM))],
        out_specs=[rows(D_ATT), rows(D_ATT), rows(D_ATT), rows(3 * D_GDN), rows(D_GDN), rows(LANES)],
        out_shape=[tok, tok, tok,
                   jax.ShapeDtypeStruct((n, 3 * D_GDN), f32),
                   jax.ShapeDtypeStruct((n, D_GDN), f32),
                   jax.ShapeDtypeStruct((n, LANES), f32)],
        compiler_params=pltpu.CompilerParams(dimension_semantics=("arbitrary",),
                                             vmem_limit_bytes=VMEM_LIMIT),
        name="in_proj_sample",
    )(x2d, nm, wqkv, wg, wgate, wba, qg, kg)


def _fold_rows(s, op):
    parts = [s[r0:r0 + 8] for r0 in range(0, s.shape[0], 8)]
    while len(parts) > 1:
        parts = [op(parts[k], parts[k + 1]) for k in range(0, len(parts), 2)]
    return parts[0]


MOBA_HEADS = 2
MOBA_TRIP = 8


def _moba_prompt_body(slope_ref, qt_ref, qtn_ref, kt_ref, vt_ref, o_ref, ka_ref, va_ref, kmean_ref, s_ref,
                      qa_ref, *, seq):
    hg = pl.program_id(1)
    i = pl.program_id(2)
    n_blk = seq // MOBA_BLOCK
    trip = MOBA_TRIP * MOBA_BLOCK
    slopes = [slope_ref[pl.ds(hg * MOBA_HEADS + j, 1), 0:1] for j in range(MOBA_HEADS)]
    nrow = _iota((n_blk, MOBA_BLOCK), 0)
    nrow_f = nrow.astype(f32)

    def augmented_qt(qt, tile, j):
        tvec = jnp.full((n_blk, MOBA_BLOCK), tile, jnp.int32)
        gt = _dot(kmean_ref[j], qt, NN, HI)
        gt = jnp.where(nrow < tvec, gt, -jnp.inf)
        picked = jnp.zeros((n_blk, MOBA_BLOCK), f32)
        for _ in range(MOBA_TOPK):
            best = jnp.max(gt, axis=0, keepdims=True)
            first_best = jnp.min(jnp.where(gt == best, nrow_f, float(n_blk)), axis=0, keepdims=True)
            chosen = nrow_f == first_best
            picked = jnp.where(chosen, 1.0, picked)
            gt = jnp.where(chosen, -jnp.inf, gt)
        sel = (nrow < tvec) & (picked > 0.0)
        shift = -(slopes[j] * MOBA_BLOCK) * tile.astype(f32)
        c_t = jnp.where(nrow < 2, 1.0, jnp.where(nrow == 2, shift, 0.0))
        return jnp.concatenate([qt * ATT_SCALE, jnp.where(sel, 0.0, NEG), c_t], axis=0).astype(bf16)

    @pl.when(i == 0)
    def _():
        row = _iota((HEAD_DIM, seq), 0)
        pos = _iota((HEAD_DIM, seq), 1)
        blk = jnp.right_shift(pos, MOBA_SHIFT)
        jloc = jnp.bitwise_and(pos, MOBA_BLOCK - 1)
        onehot = jnp.where(row == blk, 1.0, 0.0)
        ones_row = jnp.where(row == 0, 1.0, 0.0).astype(bf16)
        eye_b = (_iota((MOBA_BLOCK, MOBA_BLOCK), 0) == _iota((MOBA_BLOCK, MOBA_BLOCK), 1)).astype(bf16)
        for j in range(MOBA_HEADS):
            kt = kt_ref[0, j]
            e = jnp.where(row == n_blk, slopes[j] * jloc.astype(f32), onehot)
            e = jnp.where(row == n_blk + 1, (slopes[j] * MOBA_BLOCK) * blk.astype(f32), e)
            e = jnp.where(row == n_blk + 2, 1.0, e)
            ka_t = jnp.concatenate([kt.astype(bf16), e.astype(bf16)], axis=0)
            for n in range(n_blk):
                blk_cols = slice(n * MOBA_BLOCK, (n + 1) * MOBA_BLOCK)
                ka_ref[j, blk_cols, :] = _dot(eye_b, ka_t[:, blk_cols], NT).astype(bf16)
            va_ref[j, 0:HEAD_DIM, :] = vt_ref[0, j].astype(bf16)
            va_ref[j, HEAD_DIM:, :] = ones_row
            sums = [jnp.sum(kt[:, n * MOBA_BLOCK:(n + 1) * MOBA_BLOCK], axis=1, keepdims=True)
                    for n in range(n_blk)]
            kmean_ref[j] = (jnp.concatenate(sums, axis=1) * (1.0 / MOBA_BLOCK)).T
        for j in range(MOBA_HEADS):
            qa_ref[j] = augmented_qt(qt_ref[0, j], i, j)

    arow = _iota((2 * HEAD_DIM, MOBA_BLOCK), 0)
    pen_rows = (arow >= HEAD_DIM) & (arow < HEAD_DIM + n_blk)
    causal = _iota((MOBA_BLOCK, MOBA_BLOCK), 0) <= _iota((MOBA_BLOCK, MOBA_BLOCK), 1)
    d0 = pl.multiple_of(i * MOBA_BLOCK, MOBA_BLOCK)
    qa = [qa_ref[j] for j in range(MOBA_HEADS)]
    s_own = []
    for j in range(MOBA_HEADS):
        qo = jnp.where(pen_rows, jnp.zeros_like(qa[j]), qa[j])
        s_own.append(jnp.where(causal, _dot(ka_ref[j, pl.ds(d0, MOBA_BLOCK), :], qo) * LOG2E, NEG))
    half = trip // 2
    rem = i % MOBA_TRIP
    n_full = i // MOBA_TRIP + (rem > MOBA_TRIP // 2).astype(jnp.int32)
    n_half = ((rem > 0) & (rem <= MOBA_TRIP // 2)).astype(jnp.int32)
    half0 = n_full * trip

    def pass1(size, base):
        def body(t, mx):
            r0 = pl.multiple_of(base + t * size, half)
            out = []
            for j in range(MOBA_HEADS):
                s = _dot(ka_ref[j, pl.ds(r0, size), :], qa[j]) * LOG2E
                s_ref[j, pl.ds(r0, size), :] = s
                out.append(jnp.maximum(mx[j], _fold_rows(s, jnp.maximum)))
            return tuple(out)
        return body

    mx = lax.fori_loop(0, n_full, pass1(trip, 0), tuple(_fold_rows(s, jnp.maximum) for s in s_own))
    mx = lax.fori_loop(0, n_half, pass1(half, half0), mx)
    for j in range(MOBA_HEADS):
        qa_ref[j] = augmented_qt(qtn_ref[0, j], i + 1, j)
    m_row = [jnp.max(mx[j], axis=0, keepdims=True) for j in range(MOBA_HEADS)]

    def pass2(size, base):
        def body(t, acc):
            r0 = pl.multiple_of(base + t * size, half)
            return tuple(acc[j] + _dot(va_ref[j, :, pl.ds(r0, size)],
                                       jnp.exp2((s_ref[j, pl.ds(r0, size), :] - m_row[j]).astype(bf16)))
                         for j in range(MOBA_HEADS))
        return body

    acc = lax.fori_loop(0, n_full, pass2(trip, 0),
                        tuple(_dot(va_ref[j, :, pl.ds(d0, MOBA_BLOCK)], jnp.exp2((s_own[j] - m_row[j]).astype(bf16)))
                              for j in range(MOBA_HEADS)))
    acc = lax.fori_loop(0, n_half, pass2(half, half0), acc)
    for j in range(MOBA_HEADS):
        o_ref[0, j] = (acc[j][:HEAD_DIM] / acc[j][HEAD_DIM:HEAD_DIM + 1]).T


def _moba_prompt(qt, kt, vt):
    batch, heads, _, seq = qt.shape
    n_blk = seq // MOBA_BLOCK
    assert n_blk % MOBA_TRIP == 0 and 2 * n_blk == HEAD_DIM and MOBA_BLOCK == 1 << MOBA_SHIFT
    assert heads % MOBA_HEADS == 0
    slopes = jnp.asarray(np.repeat(np.asarray(ALIBI_SLOPES, np.float32)[:, None], LANES, axis=1))
    tile = pl.BlockSpec((1, MOBA_HEADS, HEAD_DIM, MOBA_BLOCK), lambda b, h, i: (b, h, 0, i))
    next_tile = pl.BlockSpec((1, MOBA_HEADS, HEAD_DIM, MOBA_BLOCK),
                             lambda b, h, i: (b, h, 0, jnp.minimum(i + 1, n_blk - 1)))
    whole = pl.BlockSpec((1, MOBA_HEADS, HEAD_DIM, seq), lambda b, h, i: (b, h, 0, 0),
                         pipeline_mode=pl.Buffered(1))
    return pl.pallas_call(
        functools.partial(_moba_prompt_body, seq=seq),
        grid=(batch, heads // MOBA_HEADS, n_blk),
        in_specs=[pl.BlockSpec((H_ATT, LANES), lambda b, h, i: (0, 0)), tile, next_tile, whole, whole],
        out_specs=pl.BlockSpec((1, MOBA_HEADS, MOBA_BLOCK, HEAD_DIM), lambda b, h, i: (b, h, i, 0)),
        out_shape=jax.ShapeDtypeStruct((batch, heads, seq, HEAD_DIM), f32),
        scratch_shapes=[pltpu.VMEM((MOBA_HEADS, seq, 2 * HEAD_DIM), bf16),
                        pltpu.VMEM((MOBA_HEADS, 2 * HEAD_DIM, seq), bf16),
                        pltpu.VMEM((MOBA_HEADS, n_blk, HEAD_DIM), f32),
                        pltpu.VMEM((MOBA_HEADS, seq, MOBA_BLOCK), f32),
                        pltpu.VMEM((MOBA_HEADS, 2 * HEAD_DIM, MOBA_BLOCK), bf16)],
        compiler_params=pltpu.CompilerParams(dimension_semantics=("arbitrary",) * 3,
                                             vmem_limit_bytes=VMEM_LIMIT),
        name="moba_prompt",
    )(slopes, qt, qt, kt, vt)


SAMPLE_SEQS = 2


def _moba_sample_body(pt_ref, q_ref, k_ref, v_ref, *refs, past_len, t_new, spb):
    del pt_ref
    n_pages = past_len // PAGE_SIZE
    kp_refs, vp_refs = refs[:spb * n_pages], refs[spb * n_pages:2 * spb * n_pages]
    o_ref = refs[2 * spb * n_pages]
    own = past_len // MOBA_BLOCK
    n_sel = min(MOBA_TOPK, own)
    tq = _iota((t_new, past_len), 0)
    sk = _iota((t_new, past_len), 1)
    dist = (past_len + tq - sk).astype(f32)
    tq2 = _iota((t_new, t_new), 0)
    sk2 = _iota((t_new, t_new), 1)
    units = [(s, h) for s in range(spb) for h in range(H_ATT)]
    us = range(len(units))
    rows = [slice(s * t_new, (s + 1) * t_new) for s, _ in units]
    cols = [slice(h * HEAD_DIM, (h + 1) * HEAD_DIM) for _, h in units]
    slope = [ALIBI_SLOPES[h] for _, h in units]
    q = [q_ref[rows[u], cols[u]] for u in us]
    qb = [(q[u] * ATT_SCALE).astype(bf16) for u in us]
    kb = [jnp.concatenate([kp_refs[s * n_pages + p][0, h].astype(bf16) for p in range(n_pages)], axis=1)
          for s, h in units]
    s_raw = [_dot(qb[u], kb[u]) for u in us]
    gate = [jnp.concatenate([jnp.sum(s_raw[u][:, n * MOBA_BLOCK:(n + 1) * MOBA_BLOCK], axis=1, keepdims=True)
                             for n in range(own)], axis=1) for u in us]

    def penalty(g, n):
        gn = g[:, n:n + 1]
        rank = jnp.zeros((t_new, 1), f32)
        for m in range(own):
            if m != n:
                gm = g[:, m:m + 1]
                beats = (gm > gn) | ((gm == gn) & (m < n))
                rank = rank + jnp.where(beats, 1.0, 0.0)
        return jnp.broadcast_to(jnp.where(rank < n_sel, 0.0, NEG), (t_new, MOBA_BLOCK))

    pen = [jnp.concatenate([penalty(gate[u], n) for n in range(own)], axis=1) for u in us]
    s_past = [s_raw[u] + pen[u] - slope[u] * dist for u in us]
    s_own = [jnp.where(sk2 <= tq2,
                       _dot(qb[u], k_ref[rows[u], cols[u]].astype(bf16), NT) - slope[u] * (tq2 - sk2).astype(f32),
                       NEG) for u in us]
    m = [jnp.maximum(jnp.max(s_past[u], axis=1, keepdims=True), jnp.max(s_own[u], axis=1, keepdims=True))
         for u in us]
    p_past = [jnp.exp(s_past[u] - m[u]) for u in us]
    p_own = [jnp.exp(s_own[u] - m[u]) for u in us]
    denom = [jnp.sum(p_past[u], axis=1, keepdims=True) + jnp.sum(p_own[u], axis=1, keepdims=True) for u in us]
    vb = [jnp.concatenate([vp_refs[s * n_pages + p][0, h].astype(bf16) for p in range(n_pages)], axis=1)
          for s, h in units]
    o = [(_dot(p_own[u].astype(bf16), v_ref[rows[u], cols[u]].astype(bf16))
          + _dot(p_past[u].astype(bf16), vb[u], NT)) / denom[u] for u in us]
    for s in range(spb):
        o_ref[s * t_new:(s + 1) * t_new, :] = jnp.concatenate(o[s * H_ATT:(s + 1) * H_ATT], axis=-1)


def _moba_sample(q, k, v, cache_kt, cache_vt, page_table, *, t_new):
    n_seq, n_pages = page_table.shape
    past_len = n_pages * PAGE_SIZE
    assert past_len % MOBA_BLOCK == 0 and past_len // MOBA_BLOCK >= 1
    spb = SAMPLE_SEQS
    assert n_seq % spb == 0
    rows = pl.BlockSpec((spb * t_new, D_ATT), lambda b, pt: (b, 0))

    def page_spec(s, p):
        return pl.BlockSpec((1, H_ATT, HEAD_DIM, PAGE_SIZE),
                            lambda b, pt: (pt[(b * spb + s) * n_pages + p], 0, 0, 0))

    pages = [page_spec(s, p) for s in range(spb) for p in range(n_pages)]
    grid_spec = pltpu.PrefetchScalarGridSpec(
        num_scalar_prefetch=1,
        grid=(n_seq // spb,),
        in_specs=[rows, rows, rows] + pages * 2,
        out_specs=rows,
    )
    return pl.pallas_call(
        functools.partial(_moba_sample_body, past_len=past_len, t_new=t_new, spb=spb),
        grid_spec=grid_spec,
        out_shape=jax.ShapeDtypeStruct(q.shape, f32),
        compiler_params=pltpu.CompilerParams(dimension_semantics=("arbitrary",),
                                             vmem_limit_bytes=VMEM_LIMIT),
        name="moba_sample",
    )(page_table.reshape(-1), q, k, v, *([cache_kt] * (spb * n_pages)), *([cache_vt] * (spb * n_pages)))


def _conv_silu(cur, prev8, w):
    t = cur.shape[0]
    row8 = _iota((8, cur.shape[1]), 0)
    y = None
    for i in range(CONV_W):
        d = CONV_W - 1 - i
        if d == 0:
            term = cur
        else:
            sh = pltpu.roll(cur, d, 0)
            top = jnp.where(row8 < d, pltpu.roll(prev8, d, 0), sh[0:8])
            term = top if t == 8 else jnp.concatenate([top, sh[8:]], axis=0)
        term = term * w[i:i + 1, :]
        y = term if y is None else y + term
    return _silu(y)


def _conv_silu_stacked(cur, prev, w):
    n = cur.shape[0]
    pos = jnp.bitwise_and(_iota(cur.shape, 0), 7)
    y = None
    for i in range(CONV_W):
        d = CONV_W - 1 - i
        if d == 0:
            term = cur
        else:
            term = jnp.where(pos < d, pltpu.roll(prev, (n - 8 + d) % n, 0), pltpu.roll(cur, d, 0))
        term = term * w[i:i + 1, :]
        y = term if y is None else y + term
    return _silu(y)


def _gdn_body(uq_ref, uk_ref, uv_ref, pq_ref, pk_ref, pv_ref, cq_ref, ck_ref, cv_ref,
              wq_ref, wk_ref, wv_ref, ba_ref, gate_ref, alog_ref, dtb_ref, gg_ref, m0_ref,
              o_ref, m_ref, *, bb, tb, c, flat):
    t = pl.program_id(1)

    @pl.when(t == 0)
    def _():
        m_ref[...] = m0_ref[...]

    seqs = range(1) if flat else range(bb)
    stacked = bb * tb
    blk = (lambda ref, s: ref[...].reshape(stacked, ref.shape[-1])) if flat else (lambda ref, s: ref[s])
    tb = stacked if flat else tb
    n_chunks = tb // c
    lc = int(np.log2(c))
    assert 1 << lc == c
    ri = _iota((tb, tb), 0)
    ci = _iota((tb, tb), 1)
    same = jnp.right_shift(ri, lc) == jnp.right_shift(ci, lc)
    eye = ri == ci
    keep = same & (ri >= ci)
    cum = jnp.where(keep, 1.0, 0.0)
    tot = jnp.where(same, 1.0, 0.0)
    sb = min(tb, GDN_SOLVE)
    assert tb % sb == 0 and sb % c == 0
    rs_i = _iota((sb, sb), 0)
    cs_i = _iota((sb, sb), 1)
    eye_s = rs_i == cs_i
    keep_s = (jnp.right_shift(rs_i, lc) == jnp.right_shift(cs_i, lc)) & (rs_i >= cs_i)
    ident_s = jnp.where(eye_s, 1.0, 0.0)
    split = lambda z: [z[:, h * HEAD_DIM:(h + 1) * HEAD_DIM] for h in range(H_GDN)]
    if tb >= LANES:
        head_ones = jnp.where(_iota((D_GDN, D_GDN), 0) // HEAD_DIM == _iota((D_GDN, D_GDN), 1) // HEAD_DIM,
                              1.0, 0.0)
        l2_heads = lambda z: split(z * lax.rsqrt(_bdot(z * z, head_ones) + L2_EPS))
        rms_cat = lambda o: o * lax.rsqrt(_bdot(o * o, head_ones) * (1.0 / HEAD_DIM) + RMS_EPS)
        rms_heads = lambda parts: rms_cat(jnp.concatenate(parts, axis=-1))
    else:
        l2_heads = lambda z: [zh * lax.rsqrt(jnp.sum(zh * zh, axis=-1, keepdims=True) + L2_EPS) for zh in split(z)]
        rms_heads = lambda parts: jnp.concatenate([_rms(p) for p in parts], axis=-1)
    first = jnp.full((8, D_GDN), t, jnp.int32) == 0

    for s in seqs:
        if flat:
            q = _conv_silu_stacked(blk(uq_ref, s), blk(cq_ref, s), wq_ref[...])
            k = _conv_silu_stacked(blk(uk_ref, s), blk(ck_ref, s), wk_ref[...])
            v = _conv_silu_stacked(blk(uv_ref, s), blk(cv_ref, s), wv_ref[...])
        else:
            q = _conv_silu(uq_ref[s], jnp.where(first, cq_ref[s], pq_ref[s]), wq_ref[...])
            k = _conv_silu(uk_ref[s], jnp.where(first, ck_ref[s], pk_ref[s]), wk_ref[...])
            v = _conv_silu(uv_ref[s], jnp.where(first, cv_ref[s], pv_ref[s]), wv_ref[...])
        ba = blk(ba_ref, s)
        beta_all = jax.nn.sigmoid(ba)
        xa = ba + dtb_ref[...]
        softplus = jnp.maximum(xa, 0.0) + jnp.log1p(jnp.exp(-jnp.abs(xa)))
        g_all = -jnp.exp(alog_ref[...]) * softplus
        gcum_all = _dot(cum, g_all, NN, HI)
        gtot_all = _dot(tot, g_all, NN, HI)
        gcum_t = gcum_all.T if tb % LANES == 0 else None
        hs = range(H_GDN)
        beta = [beta_all[:, h:h + 1] for h in hs]
        gcum = [gcum_all[:, H_GDN + h:H_GDN + h + 1] for h in hs]
        gtot = [gtot_all[:, H_GDN + h:H_GDN + h + 1] for h in hs]
        if gcum_t is None:
            grow = [jnp.sum(jnp.where(eye, gcum[h], 0.0), axis=0, keepdims=True) for h in hs]
        else:
            grow = [gcum_t[H_GDN + h:H_GDN + h + 1, :] for h in hs]
        qh = [z * (HEAD_DIM ** -0.5) for z in l2_heads(q)]
        kh = l2_heads(k)
        vh = [v[:, h * HEAD_DIM:(h + 1) * HEAD_DIM] for h in hs]
        eg = [jnp.exp(gcum[h]) for h in hs]
        rhs = [jnp.concatenate([beta[h] * vh[h], (beta[h] * eg[h]) * kh[h]], axis=-1) for h in hs]
        units = [(h, slice(r0, r0 + sb)) for h in hs for r0 in range(0, tb, sb)]
        decay = [jnp.exp(jnp.where(keep_s, gcum[h][rs] - grow[h][:, rs], NEG)) for h, rs in units]
        pw = [beta[h][rs] * _bdot(kh[h][rs], kh[h][rs], NT) * jnp.where(eye_s, 0.0, decay[u])
              for u, (h, rs) in enumerate(units)]
        x = [ident_s - a for a in pw]
        for _ in range(lc - 1):
            pw = [_bdot(a, a) for a in pw]
            x = [xu + _bdot(xu, a) for xu, a in zip(x, pw)]
        sol = [_bdot(x[u], rhs[h][rs]) for u, (h, rs) in enumerate(units)]
        aqk = [_bdot(qh[h][rs], kh[h][rs], NT) * decay[u] for u, (h, rs) in enumerate(units)]
        qg = [qh[h] * eg[h] for h in hs]
        kd = [kh[h] * jnp.exp(gtot[h] - gcum[h]) for h in hs]
        gdec = [jnp.exp(gtot[h]) for h in hs]
        m = None if flat else [m_ref[s, h] for h in hs]
        n_sub = tb // sb
        deltas = [[] for _ in hs]
        oqs = [[] for _ in hs]
        for ch in range(n_chunks):
            if flat:
                m = [m_ref[ch, h] for h in hs]
            sl = slice(ch * c, (ch + 1) * c)
            lsl = slice((ch * c) % sb, (ch * c) % sb + c)
            su = [sol[h * n_sub + (ch * c) // sb] for h in hs]
            both = [_bdot(jnp.concatenate([su[h][lsl, HEAD_DIM:], qg[h][sl]], axis=0), m[h]) for h in hs]
            for h in hs:
                deltas[h].append(su[h][lsl, :HEAD_DIM] - both[h][:c])
                oqs[h].append(both[h][c:])
            m = [gdec[h][ch * c:ch * c + 1, :] * m[h] + _dot(kd[h][sl], deltas[h][ch], TN) for h in hs]
            if flat:
                for h in hs:
                    m_ref[ch, h] = m[h]
        cps = sb // c
        cat = lambda parts: parts[0] if len(parts) == 1 else jnp.concatenate(parts, axis=0)
        outs = []
        for h in hs:
            if not flat:
                m_ref[s, h] = m[h]
            intra = cat([_bdot(aqk[h * n_sub + r], cat(deltas[h][r * cps:(r + 1) * cps])) for r in range(n_sub)])
            outs.append(cat(oqs[h]) + intra)
        o = rms_heads(outs) * gg_ref[...] * _silu(blk(gate_ref, s))
        if flat:
            o_ref[...] = o.reshape(o_ref.shape)
        else:
            o_ref[s] = o


def _gdn(u, conv_pad, conv_w, ba, gate, alog_pad, dtb_pad, gg8, m0, *, bb, tb):
    batch, seq, _ = u.shape
    c = min(GDN_CHUNK, seq)
    assert batch % bb == 0 and seq % tb == 0 and tb % c == 0 and tb % 8 == 0
    flat = bb > 1
    assert not flat or (seq == tb == c == 8)
    cur = lambda g: pl.BlockSpec((bb, tb, D_GDN), lambda b, t: (b, t, g))
    prev = lambda g: pl.BlockSpec((bb, 8, D_GDN), lambda b, t: (b, jnp.maximum(t * (tb // 8) - 1, 0), g))
    cbuf = lambda g: pl.BlockSpec((bb, 8, D_GDN), lambda b, t: (b, 0, g))
    cw = lambda g: pl.BlockSpec((CONV_W, D_GDN), lambda b, t: (0, g))
    row = lambda width: pl.BlockSpec((1, width), lambda b, t: (0, 0))
    state = pl.BlockSpec((bb, H_GDN, HEAD_DIM, HEAD_DIM), lambda b, t: (b, 0, 0, 0))
    return pl.pallas_call(
        functools.partial(_gdn_body, bb=bb, tb=tb, c=c, flat=flat),
        grid=(batch // bb, seq // tb),
        in_specs=[cur(0), cur(1), cur(2), prev(0), prev(1), prev(2), cbuf(0), cbuf(1), cbuf(2),
                  cw(0), cw(1), cw(2),
                  pl.BlockSpec((bb, tb, LANES), lambda b, t: (b, t, 0)),
                  pl.BlockSpec((bb, tb, D_GDN), lambda b, t: (b, t, 0)),
                  row(LANES), row(LANES), row(D_GDN), state],
        out_specs=[pl.BlockSpec((bb, tb, D_GDN), lambda b, t: (b, t, 0)), state],
        out_shape=[jax.ShapeDtypeStruct((batch, seq, D_GDN), f32),
                   jax.ShapeDtypeStruct((batch, H_GDN, HEAD_DIM, HEAD_DIM), f32)],
        compiler_params=pltpu.CompilerParams(dimension_semantics=("arbitrary",) * 2,
                                             vmem_limit_bytes=VMEM_LIMIT),
        name="gdn",
    )(u, u, u, u, u, u, conv_pad, conv_pad, conv_pad, conv_w, conv_w, conv_w,
      ba, gate, alog_pad, dtb_pad, gg8, m0)


FF_CHUNK = 256


def _out_ffn_body(x_ref, oa_ref, og_ref, ag_ref, wo_ref, nf_ref, wg_ref, wu_ref, wd_ref, y_ref):
    oa = jnp.concatenate([oa_ref[0, h] for h in range(H_ATT)], axis=-1)
    oa = _rms(oa) * ag_ref[...]
    mix = jnp.concatenate([oa, og_ref[...]], axis=-1).astype(bf16)
    hid = x_ref[...] + jnp.dot(mix, wo_ref[...], preferred_element_type=f32)
    ub = (_rms(hid) * nf_ref[...]).astype(bf16)
    d_ff = wg_ref.shape[1]
    acc = hid
    for cidx in range(d_ff // FF_CHUNK):
        sl = slice(cidx * FF_CHUNK, (cidx + 1) * FF_CHUNK)
        a = jnp.dot(ub, wg_ref[:, sl], preferred_element_type=f32)
        b = jnp.dot(ub, wu_ref[:, sl], preferred_element_type=f32)
        acc = acc + jnp.dot((_silu(a) * b).astype(bf16), wd_ref[sl, :], preferred_element_type=f32)
    y_ref[...] = acc


def _out_ffn(x2d, o_att, o_gdn, ag, wo, nf, wg, wu, wd, *, tm):
    n = x2d.shape[0]
    batch, _, seq, _ = o_att.shape
    assert n == batch * seq and seq % tm == 0 and wg.shape[1] % FF_CHUNK == 0
    spb = seq // tm
    full = lambda shape: _resident(shape)
    rows = lambda width: pl.BlockSpec((tm, width), lambda i: (i, 0))
    return pl.pallas_call(
        _out_ffn_body,
        grid=(n // tm,),
        in_specs=[rows(D_MODEL),
                  pl.BlockSpec((1, H_ATT, tm, HEAD_DIM), lambda i: (i // spb, 0, i % spb, 0)),
                  rows(D_GDN), full((1, D_ATT)), full(wo.shape), full((1, D_MODEL)),
                  full(wg.shape), full(wu.shape), full(wd.shape)],
        out_specs=rows(D_MODEL),
        out_shape=jax.ShapeDtypeStruct((n, D_MODEL), f32),
        compiler_params=pltpu.CompilerParams(dimension_semantics=("arbitrary",),
                                             vmem_limit_bytes=VMEM_LIMIT),
        name="out_ffn",
    )(x2d, o_att, o_gdn, ag, wo, nf, wg, wu, wd)


def kernel(x_prompt, x_sample, cache_k, cache_v, state_ssm, state_conv, page_table, norm_mix, w_in,
           q_norm_g, k_norm_g, attn_out_g, conv_w, a_log, dt_bias, gdn_out_g, w_out, norm_ffn,
           w_gate, w_up, w_down):
    depth = w_in.shape[0]
    assert depth == 1
    batch, seq, _ = x_prompt.shape
    n_seq, t_new, _ = x_sample.shape

    w = w_in[0]
    c0, c1, c2 = 3 * D_ATT, 3 * D_ATT + 3 * D_GDN, 3 * D_ATT + 4 * D_GDN
    wqkv = w[:, :c0].astype(bf16)
    wqkvt = w.T[:c0].astype(bf16)
    wg = w[:, c0:c1].astype(bf16)
    wgate = w[:, c1:c2].astype(bf16)
    wba = jnp.pad(w[:, c2:], ((0, 0), (0, LANES - 2 * H_GDN))).astype(bf16)
    nm = norm_mix[0].reshape(1, D_MODEL)
    qg = q_norm_g[0].reshape(1, HEAD_DIM)
    qgc = q_norm_g[0].reshape(HEAD_DIM, 1)
    kg = k_norm_g[0].reshape(1, HEAD_DIM)
    kgc = k_norm_g[0].reshape(HEAD_DIM, 1)
    ag = attn_out_g[0].reshape(1, D_ATT)
    nf = norm_ffn[0].reshape(1, D_MODEL)
    cw = conv_w[0]
    pad_lo = lambda vec: jnp.pad(vec.astype(f32), (H_GDN, LANES - 2 * H_GDN)).reshape(1, LANES)
    alog_pad = pad_lo(a_log[0])
    dtb_pad = pad_lo(dt_bias[0])
    gg8 = jnp.tile(gdn_out_g[0].astype(f32), H_GDN).reshape(1, D_GDN)
    wo = w_out[0].astype(bf16)
    wgt = w_gate[0].astype(bf16)
    wup = w_up[0].astype(bf16)
    wdn = w_down[0].astype(bf16)

    xp = x_prompt.reshape(batch * seq, D_MODEL)
    qtp, ktp, vtp, gp, gatep, bap = _in_proj_prompt(xp, nm, wqkvt, wg, wgate, wba, qgc, kgc,
                                                    batch=batch, seq=seq, tm=512)
    oap = _moba_prompt(qtp, ktp, vtp)
    conv0 = jnp.zeros((batch, 8, 3 * D_GDN), f32)
    ssm0 = jnp.zeros((batch, H_GDN, HEAD_DIM, HEAD_DIM), f32)
    ogp, ssm_p = _gdn(gp.reshape(batch, seq, 3 * D_GDN), conv0, cw, bap.reshape(batch, seq, LANES),
                      gatep.reshape(batch, seq, D_GDN), alog_pad, dtb_pad, gg8, ssm0, bb=1, tb=256)
    yp = _out_ffn(xp, oap, ogp.reshape(batch * seq, D_GDN), ag, wo, nf, wgt, wup, wdn, tm=512)

    ns = n_seq * t_new
    xs = x_sample.reshape(ns, D_MODEL)
    qs, ks, vs, gs, gates, bas = _in_proj_sample(xs, nm, wqkv, wg, wgate, wba, qg, kg, tm=256)
    cache_kt = jnp.swapaxes(cache_k[0], -1, -2)
    cache_vt = jnp.swapaxes(cache_v[0], -1, -2)
    oas = _moba_sample(qs, ks, vs, cache_kt, cache_vt, page_table, t_new=t_new)
    conv_s = jnp.pad(state_conv[0], ((0, 0), (8 - (CONV_W - 1), 0), (0, 0)))
    ogs, ssm_s = _gdn(gs.reshape(n_seq, t_new, 3 * D_GDN), conv_s, cw, bas.reshape(n_seq, t_new, LANES),
                      gates.reshape(n_seq, t_new, D_GDN), alog_pad, dtb_pad, gg8, state_ssm[0],
                      bb=16, tb=t_new)
    to_heads = lambda a: a.reshape(n_seq, t_new, H_ATT, HEAD_DIM).transpose(0, 2, 1, 3)
    oas_hm = oas.reshape(1, ns, H_ATT, HEAD_DIM).transpose(0, 2, 1, 3)
    ys = _out_ffn(xs, oas_hm, ogs.reshape(ns, D_GDN), ag, wo, nf, wgt, wup, wdn, tm=512)

    gp3 = gp.reshape(batch, seq, 3 * D_GDN)
    gs3 = gs.reshape(n_seq, t_new, 3 * D_GDN)
    return (yp.reshape(batch, seq, D_MODEL), ys.reshape(n_seq, t_new, D_MODEL),
            jnp.swapaxes(ktp, -1, -2)[None], jnp.swapaxes(vtp, -1, -2)[None],
            to_heads(ks)[None], to_heads(vs)[None],
            ssm_p[None], ssm_s[None],
            gp3[:, seq - (CONV_W - 1):][None], gs3[:, t_new - (CONV_W - 1):][None])
```

```python
import functools

import numpy as np
import jax
import jax.numpy as jnp
from jax import lax
from jax.experimental import pallas as pl
from jax.experimental.pallas import tpu as pltpu

f32 = jnp.float32
bf16 = jnp.bfloat16

D_MODEL = 1024
HEAD_DIM = 64
H_ATT = 8
H_GDN = 8
D_ATT = H_ATT * HEAD_DIM
D_GDN = H_GDN * HEAD_DIM
MOBA_BLOCK = 256
MOBA_SHIFT = 8
MOBA_TOPK = 3
GDN_CHUNK = 64
GDN_SOLVE = 128
CONV_W = 4
PAGE_SIZE = 128
RMS_EPS = 1e-6
L2_EPS = 1e-6
ATT_SCALE = HEAD_DIM ** -0.5
LOG2E = float(np.log2(np.e))
NEG = -1e30
LANES = 128
VMEM_LIMIT = 56 * 1024 * 1024

HI = lax.Precision.HIGHEST
NN = (((1,), (0,)), ((), ()))
NT = (((1,), (1,)), ((), ()))
TN = (((0,), (0,)), ((), ()))

ALIBI_SLOPES = [2.0 ** (-8.0 * (i + 1) / H_ATT) for i in range(H_ATT)]


def _dot(a, b, dims=NN, precision=None):
    return lax.dot_general(a, b, dims, precision=precision, preferred_element_type=f32)


def _bdot(a, b, dims=NN):
    return lax.dot_general(a.astype(bf16), b.astype(bf16), dims, preferred_element_type=f32)


def _rms(x, axis=-1):
    return x * lax.rsqrt(jnp.mean(x * x, axis=axis, keepdims=True) + RMS_EPS)


def _silu(x):
    return x * jax.nn.sigmoid(x)


def _iota(shape, dim):
    return lax.broadcasted_iota(jnp.int32, shape, dim)


def _resident(shape):
    return pl.BlockSpec(shape, lambda i: (0,) * len(shape), pipeline_mode=pl.Buffered(1))


def _in_proj_prompt_body(x_ref, nm_ref, wqkvt_ref, wg_ref, wgate_ref, wba_ref, qgc_ref, kgc_ref,
                         qt_ref, kt_ref, vt_ref, g_ref, gate_ref, ba_ref):
    x = x_ref[...]
    xb = (_rms(x) * nm_ref[...]).astype(bf16)
    z = _dot(wqkvt_ref[...], xb, NT)
    for h in range(H_ATT):
        lo, hi = h * HEAD_DIM, (h + 1) * HEAD_DIM
        qt_ref[0, h] = _rms(z[lo:hi, :], axis=0) * qgc_ref[...]
        kt_ref[0, h] = _rms(z[D_ATT + lo:D_ATT + hi, :], axis=0) * kgc_ref[...]
        vt_ref[0, h] = z[2 * D_ATT + lo:2 * D_ATT + hi, :]
    g_ref[...] = jnp.dot(xb, wg_ref[...], preferred_element_type=f32)
    gate_ref[...] = jnp.dot(xb, wgate_ref[...], preferred_element_type=f32)
    ba_ref[...] = jnp.dot(xb, wba_ref[...], preferred_element_type=f32)


def _in_proj_prompt(x2d, nm, wqkvt, wg, wgate, wba, qgc, kgc, *, batch, seq, tm):
    n = x2d.shape[0]
    assert n == batch * seq and seq % tm == 0
    spb = seq // tm
    full = lambda shape: _resident(shape)
    rows = lambda width: pl.BlockSpec((tm, width), lambda i: (i, 0))
    t_spec = pl.BlockSpec((1, H_ATT, HEAD_DIM, tm), lambda i: (i // spb, 0, 0, i % spb))
    t_shape = jax.ShapeDtypeStruct((batch, H_ATT, HEAD_DIM, seq), f32)
    return pl.pallas_call(
        _in_proj_prompt_body,
        grid=(n // tm,),
        in_specs=[rows(D_MODEL), full((1, D_MODEL)), full(wqkvt.shape), full(wg.shape),
                  full(wgate.shape), full(wba.shape), full((HEAD_DIM, 1)), full((HEAD_DIM, 1))],
        out_specs=[t_spec, t_spec, t_spec, rows(3 * D_GDN), rows(D_GDN), rows(LANES)],
        out_shape=[t_shape, t_shape, t_shape,
                   jax.ShapeDtypeStruct((n, 3 * D_GDN), f32),
                   jax.ShapeDtypeStruct((n, D_GDN), f32),
                   jax.ShapeDtypeStruct((n, LANES), f32)],
        compiler_params=pltpu.CompilerParams(dimension_semantics=("arbitrary",),
                                             vmem_limit_bytes=VMEM_LIMIT),
        name="in_proj_prompt",
    )(x2d, nm, wqkvt, wg, wgate, wba, qgc, kgc)


def _in_proj_sample_body(x_ref, nm_ref, wqkv_ref, wg_ref, wgate_ref, wba_ref, qg_ref, kg_ref,
                         q_ref, k_ref, v_ref, g_ref, gate_ref, ba_ref):
    x = x_ref[...]
    xb = (_rms(x) * nm_ref[...]).astype(bf16)
    z = jnp.dot(xb, wqkv_ref[...], preferred_element_type=f32)
    qs, ks = [], []
    for h in range(H_ATT):
        lo, hi = h * HEAD_DIM, (h + 1) * HEAD_DIM
        qs.append(_rms(z[:, lo:hi]) * qg_ref[...])
        ks.append(_rms(z[:, D_ATT + lo:D_ATT + hi]) * kg_ref[...])
    q_ref[...] = jnp.concatenate(qs, axis=-1)
    k_ref[...] = jnp.concatenate(ks, axis=-1)
    v_ref[...] = z[:, 2 * D_ATT:]
    g_ref[...] = jnp.dot(xb, wg_ref[...], preferred_element_type=f32)
    gate_ref[...] = jnp.dot(xb, wgate_ref[...], preferred_element_type=f32)
    ba_ref[...] = jnp.dot(xb, wba_ref[...], preferred_element_type=f32)


def _in_proj_sample(x2d, nm, wqkv, wg, wgate, wba, qg, kg, *, tm):
    n = x2d.shape[0]
    assert n % tm == 0
    full = lambda shape: _resident(shape)
    rows = lambda width: pl.BlockSpec((tm, width), lambda i: (i, 0))
    tok = jax.ShapeDtypeStruct((n, D_ATT), f32)
    return pl.pallas_call(
        _in_proj_sample_body,
        grid=(n // tm,),
        in_specs=[rows(D_MODEL), full((1, D_MODEL)), full(wqkv.shape), full(wg.shape),
                  full(wgate.shape), full(wba.shape), full((1, HEAD_DIM)), full((1, HEAD_DIM))],
        out_specs=[rows(D_ATT), rows(D_ATT), rows(D_ATT), rows(3 * D_GDN), rows(D_GDN), rows(LANES)],
        out_shape=[tok, tok, tok,
                   jax.ShapeDtypeStruct((n, 3 * D_GDN), f32),
                   jax.ShapeDtypeStruct((n, D_GDN), f32),
                   jax.ShapeDtypeStruct((n, LANES), f32)],
        compiler_params=pltpu.CompilerParams(dimension_semantics=("arbitrary",),
                                             vmem_limit_bytes=VMEM_LIMIT),
        name="in_proj_sample",
    )(x2d, nm, wqkv, wg, wgate, wba, qg, kg)


def _fold_rows(s, op):
    parts = [s[r0:r0 + 8] for r0 in range(0, s.shape[0], 8)]
    while len(parts) > 1:
        parts = [op(parts[k], parts[k + 1]) for k in range(0, len(parts), 2)]
    return parts[0]


MOBA_HEADS = 2
MOBA_TRIP = 16


def _moba_prompt_body(slope_ref, qt_ref, qtn_ref, kt_ref, vt_ref, o_ref, ka_ref, va_ref, kmean_ref, s_ref,
                      qa_ref, *, seq):
    hg = pl.program_id(1)
    i = pl.program_id(2)
    n_blk = seq // MOBA_BLOCK
    trip = MOBA_TRIP * MOBA_BLOCK
    slopes = [slope_ref[pl.ds(hg * MOBA_HEADS + j, 1), 0:1] for j in range(MOBA_HEADS)]
    nrow = _iota((n_blk, MOBA_BLOCK), 0)
    nrow_f = nrow.astype(f32)

    def augmented_qt(qt, tile, j):
        tvec = jnp.full((n_blk, MOBA_BLOCK), tile, jnp.int32)
        gt = _dot(kmean_ref[j], qt, NN, HI)
        gt = jnp.where(nrow < tvec, gt, -jnp.inf)
        picked = jnp.zeros((n_blk, MOBA_BLOCK), f32)
        for _ in range(MOBA_TOPK):
            best = jnp.max(gt, axis=0, keepdims=True)
            first_best = jnp.min(jnp.where(gt == best, nrow_f, float(n_blk)), axis=0, keepdims=True)
            chosen = nrow_f == first_best
            picked = jnp.where(chosen, 1.0, picked)
            gt = jnp.where(chosen, -jnp.inf, gt)
        sel = (nrow < tvec) & (picked > 0.0)
        shift = -(slopes[j] * MOBA_BLOCK) * tile.astype(f32)
        c_t = jnp.where(nrow < 2, 1.0, jnp.where(nrow == 2, shift, 0.0))
        return jnp.concatenate([qt * ATT_SCALE, jnp.where(sel, 0.0, NEG), c_t], axis=0).astype(bf16)

    @pl.when(i == 0)
    def _():
        row = _iota((HEAD_DIM, seq), 0)
        pos = _iota((HEAD_DIM, seq), 1)
        blk = jnp.right_shift(pos, MOBA_SHIFT)
        jloc = jnp.bitwise_and(pos, MOBA_BLOCK - 1)
        onehot = jnp.where(row == blk, 1.0, 0.0)
        ones_row = jnp.where(row == 0, 1.0, 0.0).astype(bf16)
        eye_b = (_iota((MOBA_BLOCK, MOBA_BLOCK), 0) == _iota((MOBA_BLOCK, MOBA_BLOCK), 1)).astype(bf16)
        for j in range(MOBA_HEADS):
            kt = kt_ref[0, j]
            e = jnp.where(row == n_blk, slopes[j] * jloc.astype(f32), onehot)
            e = jnp.where(row == n_blk + 1, (slopes[j] * MOBA_BLOCK) * blk.astype(f32), e)
            e = jnp.where(row == n_blk + 2, 1.0, e)
            ka_t = jnp.concatenate([kt.astype(bf16), e.astype(bf16)], axis=0)
            for n in range(n_blk):
                blk_cols = slice(n * MOBA_BLOCK, (n + 1) * MOBA_BLOCK)
                ka_ref[j, blk_cols, :] = _dot(eye_b, ka_t[:, blk_cols], NT).astype(bf16)
            va_ref[j, 0:HEAD_DIM, :] = vt_ref[0, j].astype(bf16)
            va_ref[j, HEAD_DIM:, :] = ones_row
            sums = [jnp.sum(kt[:, n * MOBA_BLOCK:(n + 1) * MOBA_BLOCK], axis=1, keepdims=True)
                    for n in range(n_blk)]
            kmean_ref[j] = (jnp.concatenate(sums, axis=1) * (1.0 / MOBA_BLOCK)).T
        for j in range(MOBA_HEADS):
            qa_ref[j] = augmented_qt(qt_ref[0, j], i, j)

    arow = _iota((2 * HEAD_DIM, MOBA_BLOCK), 0)
    pen_rows = (arow >= HEAD_DIM) & (arow < HEAD_DIM + n_blk)
    causal = _iota((MOBA_BLOCK, MOBA_BLOCK), 0) <= _iota((MOBA_BLOCK, MOBA_BLOCK), 1)
    d0 = pl.multiple_of(i * MOBA_BLOCK, MOBA_BLOCK)
    qa = [qa_ref[j] for j in range(MOBA_HEADS)]
    s_own = []
    for j in range(MOBA_HEADS):
        qo = jnp.where(pen_rows, jnp.zeros_like(qa[j]), qa[j])
        s_own.append(jnp.where(causal, _dot(ka_ref[j, pl.ds(d0, MOBA_BLOCK), :], qo) * LOG2E, NEG))
    half, quarter = trip // 2, trip // 4
    qb = MOBA_TRIP // 4
    rem = i % MOBA_TRIP
    as_int = lambda cond: cond.astype(jnp.int32)
    n_full = i // MOBA_TRIP + as_int(rem > 3 * qb)
    n_half = as_int((rem > qb) & (rem <= 3 * qb))
    n_quarter = as_int(((rem > 0) & (rem <= qb)) | ((rem > 2 * qb) & (rem <= 3 * qb)))
    half0 = n_full * trip
    quarter0 = half0 + n_half * half

    def pass1(size, base):
        def body(t, mx):
            r0 = pl.multiple_of(base + t * size, quarter)
            out = []
            for j in range(MOBA_HEADS):
                s = _dot(ka_ref[j, pl.ds(r0, size), :], qa[j]) * LOG2E
                s_ref[j, pl.ds(r0, size), :] = s
                out.append(jnp.maximum(mx[j], _fold_rows(s, jnp.maximum)))
            return tuple(out)
        return body

    mx = lax.fori_loop(0, n_full, pass1(trip, 0), tuple(_fold_rows(s, jnp.maximum) for s in s_own))
    mx = lax.fori_loop(0, n_half, pass1(half, half0), mx)
    mx = lax.fori_loop(0, n_quarter, pass1(quarter, quarter0), mx)
    for j in range(MOBA_HEADS):
        qa_ref[j] = augmented_qt(qtn_ref[0, j], i + 1, j)
    m_row = [jnp.max(mx[j], axis=0, keepdims=True) for j in range(MOBA_HEADS)]

    def pass2(size, base):
        def body(t, acc):
            r0 = pl.multiple_of(base + t * size, quarter)
            return tuple(acc[j] + _dot(va_ref[j, :, pl.ds(r0, size)],
                                       jnp.exp2((s_ref[j, pl.ds(r0, size), :] - m_row[j]).astype(bf16)))
                         for j in range(MOBA_HEADS))
        return body

    acc = lax.fori_loop(0, n_full, pass2(trip, 0),
                        tuple(_dot(va_ref[j, :, pl.ds(d0, MOBA_BLOCK)], jnp.exp2((s_own[j] - m_row[j]).astype(bf16)))
                              for j in range(MOBA_HEADS)))
    acc = lax.fori_loop(0, n_half, pass2(half, half0), acc)
    acc = lax.fori_loop(0, n_quarter, pass2(quarter, quarter0), acc)
    for j in range(MOBA_HEADS):
        o_ref[0, j] = (acc[j][:HEAD_DIM] / acc[j][HEAD_DIM:HEAD_DIM + 1]).T


def _moba_prompt(qt, kt, vt):
    batch, heads, _, seq = qt.shape
    n_blk = seq // MOBA_BLOCK
    assert n_blk % MOBA_TRIP == 0 and 2 * n_blk == HEAD_DIM and MOBA_BLOCK == 1 << MOBA_SHIFT
    assert heads % MOBA_HEADS == 0
    slopes = jnp.asarray(np.repeat(np.asarray(ALIBI_SLOPES, np.float32)[:, None], LANES, axis=1))
    tile = pl.BlockSpec((1, MOBA_HEADS, HEAD_DIM, MOBA_BLOCK), lambda b, h, i: (b, h, 0, i))
    next_tile = pl.BlockSpec((1, MOBA_HEADS, HEAD_DIM, MOBA_BLOCK),
                             lambda b, h, i: (b, h, 0, jnp.minimum(i + 1, n_blk - 1)))
    whole = pl.BlockSpec((1, MOBA_HEADS, HEAD_DIM, seq), lambda b, h, i: (b, h, 0, 0),
                         pipeline_mode=pl.Buffered(1))
    return pl.pallas_call(
        functools.partial(_moba_prompt_body, seq=seq),
        grid=(batch, heads // MOBA_HEADS, n_blk),
        in_specs=[pl.BlockSpec((H_ATT, LANES), lambda b, h, i: (0, 0)), tile, next_tile, whole, whole],
        out_specs=pl.BlockSpec((1, MOBA_HEADS, MOBA_BLOCK, HEAD_DIM), lambda b, h, i: (b, h, i, 0)),
        out_shape=jax.ShapeDtypeStruct((batch, heads, seq, HEAD_DIM), f32),
        scratch_shapes=[pltpu.VMEM((MOBA_HEADS, seq, 2 * HEAD_DIM), bf16),
                        pltpu.VMEM((MOBA_HEADS, 2 * HEAD_DIM, seq), bf16),
                        pltpu.VMEM((MOBA_HEADS, n_blk, HEAD_DIM), f32),
                        pltpu.VMEM((MOBA_HEADS, seq, MOBA_BLOCK), f32),
                        pltpu.VMEM((MOBA_HEADS, 2 * HEAD_DIM, MOBA_BLOCK), bf16)],
        compiler_params=pltpu.CompilerParams(dimension_semantics=("arbitrary",) * 3,
                                             vmem_limit_bytes=VMEM_LIMIT),
        name="moba_prompt",
    )(slopes, qt, qt, kt, vt)


SAMPLE_SEQS = 2


def _moba_sample_body(pt_ref, q_ref, k_ref, v_ref, *refs, past_len, t_new, spb):
    del pt_ref
    n_pages = past_len // PAGE_SIZE
    kp_refs, vp_refs = refs[:spb * n_pages], refs[spb * n_pages:2 * spb * n_pages]
    o_ref = refs[2 * spb * n_pages]
    own = past_len // MOBA_BLOCK
    n_sel = min(MOBA_TOPK, own)
    tq = _iota((t_new, past_len), 0)
    sk = _iota((t_new, past_len), 1)
    dist = (past_len + tq - sk).astype(f32)
    tq2 = _iota((t_new, t_new), 0)
    sk2 = _iota((t_new, t_new), 1)
    units = [(s, h) for s in range(spb) for h in range(H_ATT)]
    us = range(len(units))
    rows = [slice(s * t_new, (s + 1) * t_new) for s, _ in units]
    cols = [slice(h * HEAD_DIM, (h + 1) * HEAD_DIM) for _, h in units]
    slope = [ALIBI_SLOPES[h] for _, h in units]
    q = [q_ref[rows[u], cols[u]] for u in us]
    qb = [(q[u] * ATT_SCALE).astype(bf16) for u in us]
    kb = [jnp.concatenate([kp_refs[s * n_pages + p][0, h].astype(bf16) for p in range(n_pages)], axis=1)
          for s, h in units]
    s_raw = [_dot(qb[u], kb[u]) for u in us]
    gate = [jnp.concatenate([jnp.sum(s_raw[u][:, n * MOBA_BLOCK:(n + 1) * MOBA_BLOCK], axis=1, keepdims=True)
                             for n in range(own)], axis=1) for u in us]

    def penalty(g, n):
        gn = g[:, n:n + 1]
        rank = jnp.zeros((t_new, 1), f32)
        for m in range(own):
            if m != n:
                gm = g[:, m:m + 1]
                beats = (gm > gn) | ((gm == gn) & (m < n))
                rank = rank + jnp.where(beats, 1.0, 0.0)
        return jnp.broadcast_to(jnp.where(rank < n_sel, 0.0, NEG), (t_new, MOBA_BLOCK))

    pen = [jnp.concatenate([penalty(gate[u], n) for n in range(own)], axis=1) for u in us]
    s_past = [s_raw[u] + pen[u] - slope[u] * dist for u in us]
    s_own = [jnp.where(sk2 <= tq2,
                       _dot(qb[u], k_ref[rows[u], cols[u]].astype(bf16), NT) - slope[u] * (tq2 - sk2).astype(f32),
                       NEG) for u in us]
    m = [jnp.maximum(jnp.max(s_past[u], axis=1, keepdims=True), jnp.max(s_own[u], axis=1, keepdims=True))
         for u in us]
    p_past = [jnp.exp(s_past[u] - m[u]) for u in us]
    p_own = [jnp.exp(s_own[u] - m[u]) for u in us]
    denom = [jnp.sum(p_past[u], axis=1, keepdims=True) + jnp.sum(p_own[u], axis=1, keepdims=True) for u in us]
    vb = [jnp.concatenate([vp_refs[s * n_pages + p][0, h].astype(bf16) for p in range(n_pages)], axis=1)
          for s, h in units]
    o = [(_dot(p_own[u].astype(bf16), v_ref[rows[u], cols[u]].astype(bf16))
          + _dot(p_past[u].astype(bf16), vb[u], NT)) / denom[u] for u in us]
    for s in range(spb):
        o_ref[s * t_new:(s + 1) * t_new, :] = jnp.concatenate(o[s * H_ATT:(s + 1) * H_ATT], axis=-1)


def _moba_sample(q, k, v, cache_kt, cache_vt, page_table, *, t_new):
    n_seq, n_pages = page_table.shape
    past_len = n_pages * PAGE_SIZE
    assert past_len % MOBA_BLOCK == 0 and past_len // MOBA_BLOCK >= 1
    spb = SAMPLE_SEQS
    assert n_seq % spb == 0
    rows = pl.BlockSpec((spb * t_new, D_ATT), lambda b, pt: (b, 0))

    def page_spec(s, p):
        return pl.BlockSpec((1, H_ATT, HEAD_DIM, PAGE_SIZE),
                            lambda b, pt: (pt[(b * spb + s) * n_pages + p], 0, 0, 0))

    pages = [page_spec(s, p) for s in range(spb) for p in range(n_pages)]
    grid_spec = pltpu.PrefetchScalarGridSpec(
        num_scalar_prefetch=1,
        grid=(n_seq // spb,),
        in_specs=[rows, rows, rows] + pages * 2,
        out_specs=rows,
    )
    return pl.pallas_call(
        functools.partial(_moba_sample_body, past_len=past_len, t_new=t_new, spb=spb),
        grid_spec=grid_spec,
        out_shape=jax.ShapeDtypeStruct(q.shape, f32),
        compiler_params=pltpu.CompilerParams(dimension_semantics=("arbitrary",),
                                             vmem_limit_bytes=VMEM_LIMIT),
        name="moba_sample",
    )(page_table.reshape(-1), q, k, v, *([cache_kt] * (spb * n_pages)), *([cache_vt] * (spb * n_pages)))


def _conv_silu(cur, prev8, w):
    t = cur.shape[0]
    row8 = _iota((8, cur.shape[1]), 0)
    y = None
    for i in range(CONV_W):
        d = CONV_W - 1 - i
        if d == 0:
            term = cur
        else:
            sh = pltpu.roll(cur, d, 0)
            top = jnp.where(row8 < d, pltpu.roll(prev8, d, 0), sh[0:8])
            term = top if t == 8 else jnp.concatenate([top, sh[8:]], axis=0)
        term = term * w[i:i + 1, :]
        y = term if y is None else y + term
    return _silu(y)


def _conv_silu_stacked(cur, prev, w):
    n = cur.shape[0]
    pos = jnp.bitwise_and(_iota(cur.shape, 0), 7)
    y = None
    for i in range(CONV_W):
        d = CONV_W - 1 - i
        if d == 0:
            term = cur
        else:
            term = jnp.where(pos < d, pltpu.roll(prev, (n - 8 + d) % n, 0), pltpu.roll(cur, d, 0))
        term = term * w[i:i + 1, :]
        y = term if y is None else y + term
    return _silu(y)


def _gdn_body(uq_ref, uk_ref, uv_ref, pq_ref, pk_ref, pv_ref, cq_ref, ck_ref, cv_ref,
              wq_ref, wk_ref, wv_ref, ba_ref, gate_ref, alog_ref, dtb_ref, gg_ref, m0_ref,
              o_ref, m_ref, *, bb, tb, c, flat):
    t = pl.program_id(1)

    @pl.when(t == 0)
    def _():
        m_ref[...] = m0_ref[...]

    seqs = range(1) if flat else range(bb)
    stacked = bb * tb
    blk = (lambda ref, s: ref[...].reshape(stacked, ref.shape[-1])) if flat else (lambda ref, s: ref[s])
    tb = stacked if flat else tb
    n_chunks = tb // c
    lc = int(np.log2(c))
    assert 1 << lc == c
    ri = _iota((tb, tb), 0)
    ci = _iota((tb, tb), 1)
    same = jnp.right_shift(ri, lc) == jnp.right_shift(ci, lc)
    eye = ri == ci
    keep = same & (ri >= ci)
    cum = jnp.where(keep, 1.0, 0.0)
    tot = jnp.where(same, 1.0, 0.0)
    sb = min(tb, GDN_SOLVE)
    assert tb % sb == 0 and sb % c == 0
    rs_i = _iota((sb, sb), 0)
    cs_i = _iota((sb, sb), 1)
    eye_s = rs_i == cs_i
    keep_s = (jnp.right_shift(rs_i, lc) == jnp.right_shift(cs_i, lc)) & (rs_i >= cs_i)
    ident_s = jnp.where(eye_s, 1.0, 0.0)
    split = lambda z: [z[:, h * HEAD_DIM:(h + 1) * HEAD_DIM] for h in range(H_GDN)]
    if tb >= LANES:
        head_ones = jnp.where(_iota((D_GDN, D_GDN), 0) // HEAD_DIM == _iota((D_GDN, D_GDN), 1) // HEAD_DIM,
                              1.0, 0.0)
        l2_heads = lambda z: split(z * lax.rsqrt(_bdot(z * z, head_ones) + L2_EPS))
        rms_cat = lambda o: o * lax.rsqrt(_bdot(o * o, head_ones) * (1.0 / HEAD_DIM) + RMS_EPS)
        rms_heads = lambda parts: rms_cat(jnp.concatenate(parts, axis=-1))
    else:
        l2_heads = lambda z: [zh * lax.rsqrt(jnp.sum(zh * zh, axis=-1, keepdims=True) + L2_EPS) for zh in split(z)]
        rms_heads = lambda parts: jnp.concatenate([_rms(p) for p in parts], axis=-1)
    first = jnp.full((8, D_GDN), t, jnp.int32) == 0

    for s in seqs:
        if flat:
            q = _conv_silu_stacked(blk(uq_ref, s), blk(cq_ref, s), wq_ref[...])
            k = _conv_silu_stacked(blk(uk_ref, s), blk(ck_ref, s), wk_ref[...])
            v = _conv_silu_stacked(blk(uv_ref, s), blk(cv_ref, s), wv_ref[...])
        else:
            q = _conv_silu(uq_ref[s], jnp.where(first, cq_ref[s], pq_ref[s]), wq_ref[...])
            k = _conv_silu(uk_ref[s], jnp.where(first, ck_ref[s], pk_ref[s]), wk_ref[...])
            v = _conv_silu(uv_ref[s], jnp.where(first, cv_ref[s], pv_ref[s]), wv_ref[...])
        ba = blk(ba_ref, s)
        beta_all = jax.nn.sigmoid(ba)
        xa = ba + dtb_ref[...]
        softplus = jnp.maximum(xa, 0.0) + jnp.log1p(jnp.exp(-jnp.abs(xa)))
        g_all = -jnp.exp(alog_ref[...]) * softplus
        gcum_all = _dot(cum, g_all, NN, HI)
        gtot_all = _dot(tot, g_all, NN, HI)
        gcum_t = gcum_all.T if tb % LANES == 0 else None
        hs = range(H_GDN)
        beta = [beta_all[:, h:h + 1] for h in hs]
        gcum = [gcum_all[:, H_GDN + h:H_GDN + h + 1] for h in hs]
        gtot = [gtot_all[:, H_GDN + h:H_GDN + h + 1] for h in hs]
        if gcum_t is None:
            grow = [jnp.sum(jnp.where(eye, gcum[h], 0.0), axis=0, keepdims=True) for h in hs]
        else:
            grow = [gcum_t[H_GDN + h:H_GDN + h + 1, :] for h in hs]
        qh = [z * (HEAD_DIM ** -0.5) for z in l2_heads(q)]
        kh = l2_heads(k)
        vh = [v[:, h * HEAD_DIM:(h + 1) * HEAD_DIM] for h in hs]
        eg = [jnp.exp(gcum[h]) for h in hs]
        rhs = [jnp.concatenate([beta[h] * vh[h], (beta[h] * eg[h]) * kh[h]], axis=-1) for h in hs]
        units = [(h, slice(r0, r0 + sb)) for h in hs for r0 in range(0, tb, sb)]
        decay = [jnp.exp(jnp.where(keep_s, gcum[h][rs] - grow[h][:, rs], NEG)) for h, rs in units]
        pw = [beta[h][rs] * _bdot(kh[h][rs], kh[h][rs], NT) * jnp.where(eye_s, 0.0, decay[u])
              for u, (h, rs) in enumerate(units)]
        x = [ident_s - a for a in pw]
        for _ in range(lc - 1):
            pw = [_bdot(a, a) for a in pw]
            x = [xu + _bdot(xu, a) for xu, a in zip(x, pw)]
        sol = [_bdot(x[u], rhs[h][rs]) for u, (h, rs) in enumerate(units)]
        aqk = [_bdot(qh[h][rs], kh[h][rs], NT) * decay[u] for u, (h, rs) in enumerate(units)]
        qg = [qh[h] * eg[h] for h in hs]
        kd = [kh[h] * jnp.exp(gtot[h] - gcum[h]) for h in hs]
        gdec = [jnp.exp(gtot[h]) for h in hs]
        m = None if flat else [m_ref[s, h] for h in hs]
        n_sub = tb // sb
        deltas = [[] for _ in hs]
        oqs = [[] for _ in hs]
        for ch in range(n_chunks):
            if flat:
                m = [m_ref[ch, h] for h in hs]
            sl = slice(ch * c, (ch + 1) * c)
            lsl = slice((ch * c) % sb, (ch * c) % sb + c)
            su = [sol[h * n_sub + (ch * c) // sb] for h in hs]
            both = [_bdot(jnp.concatenate([su[h][lsl, HEAD_DIM:], qg[h][sl]], axis=0), m[h]) for h in hs]
            for h in hs:
                deltas[h].append(su[h][lsl, :HEAD_DIM] - both[h][:c])
                oqs[h].append(both[h][c:])
            m = [gdec[h][ch * c:ch * c + 1, :] * m[h] + _dot(kd[h][sl], deltas[h][ch], TN) for h in hs]
            if flat:
                for h in hs:
                    m_ref[ch, h] = m[h]
        cps = sb // c
        cat = lambda parts: parts[0] if len(parts) == 1 else jnp.concatenate(parts, axis=0)
        outs = []
        for h in hs:
            if not flat:
                m_ref[s, h] = m[h]
            intra = cat([_bdot(aqk[h * n_sub + r], cat(deltas[h][r * cps:(r + 1) * cps])) for r in range(n_sub)])
            outs.append(cat(oqs[h]) + intra)
        o = rms_heads(outs) * gg_ref[...] * _silu(blk(gate_ref, s))
        if flat:
            o_ref[...] = o.reshape(o_ref.shape)
        else:
            o_ref[s] = o


def _gdn(u, conv_pad, conv_w, ba, gate, alog_pad, dtb_pad, gg8, m0, *, bb, tb):
    batch, seq, _ = u.shape
    c = min(GDN_CHUNK, seq)
    assert batch % bb == 0 and seq % tb == 0 and tb % c == 0 and tb % 8 == 0
    flat = bb > 1
    assert not flat or (seq == tb == c == 8)
    cur = lambda g: pl.BlockSpec((bb, tb, D_GDN), lambda b, t: (b, t, g))
    prev = lambda g: pl.BlockSpec((bb, 8, D_GDN), lambda b, t: (b, jnp.maximum(t * (tb // 8) - 1, 0), g))
    cbuf = lambda g: pl.BlockSpec((bb, 8, D_GDN), lambda b, t: (b, 0, g))
    cw = lambda g: pl.BlockSpec((CONV_W, D_GDN), lambda b, t: (0, g))
    row = lambda width: pl.BlockSpec((1, width), lambda b, t: (0, 0))
    state = pl.BlockSpec((bb, H_GDN, HEAD_DIM, HEAD_DIM), lambda b, t: (b, 0, 0, 0))
    return pl.pallas_call(
        functools.partial(_gdn_body, bb=bb, tb=tb, c=c, flat=flat),
        grid=(batch // bb, seq // tb),
        in_specs=[cur(0), cur(1), cur(2), prev(0), prev(1), prev(2), cbuf(0), cbuf(1), cbuf(2),
                  cw(0), cw(1), cw(2),
                  pl.BlockSpec((bb, tb, LANES), lambda b, t: (b, t, 0)),
                  pl.BlockSpec((bb, tb, D_GDN), lambda b, t: (b, t, 0)),
                  row(LANES), row(LANES), row(D_GDN), state],
        out_specs=[pl.BlockSpec((bb, tb, D_GDN), lambda b, t: (b, t, 0)), state],
        out_shape=[jax.ShapeDtypeStruct((batch, seq, D_GDN), f32),
                   jax.ShapeDtypeStruct((batch, H_GDN, HEAD_DIM, HEAD_DIM), f32)],
        compiler_params=pltpu.CompilerParams(dimension_semantics=("arbitrary",) * 2,
                                             vmem_limit_bytes=VMEM_LIMIT),
        name="gdn",
    )(u, u, u, u, u, u, conv_pad, conv_pad, conv_pad, conv_w, conv_w, conv_w,
      ba, gate, alog_pad, dtb_pad, gg8, m0)


FF_CHUNK = 256


def _out_ffn_body(x_ref, oa_ref, og_ref, ag_ref, wo_ref, nf_ref, wg_ref, wu_ref, wd_ref, y_ref):
    oa = jnp.concatenate([oa_ref[0, h] for h in range(H_ATT)], axis=-1)
    oa = _rms(oa) * ag_ref[...]
    mix = jnp.concatenate([oa, og_ref[...]], axis=-1).astype(bf16)
    hid = x_ref[...] + jnp.dot(mix, wo_ref[...], preferred_element_type=f32)
    ub = (_rms(hid) * nf_ref[...]).astype(bf16)
    d_ff = wg_ref.shape[1]
    acc = hid
    for cidx in range(d_ff // FF_CHUNK):
        sl = slice(cidx * FF_CHUNK, (cidx + 1) * FF_CHUNK)
        a = jnp.dot(ub, wg_ref[:, sl], preferred_element_type=f32)
        b = jnp.dot(ub, wu_ref[:, sl], preferred_element_type=f32)
        acc = acc + jnp.dot((_silu(a) * b).astype(bf16), wd_ref[sl, :], preferred_element_type=f32)
    y_ref[...] = acc


def _out_ffn(x2d, o_att, o_gdn, ag, wo, nf, wg, wu, wd, *, tm):
    n = x2d.shape[0]
    batch, _, seq, _ = o_att.shape
    assert n == batch * seq and seq % tm == 0 and wg.shape[1] % FF_CHUNK == 0
    spb = seq // tm
    full = lambda shape: _resident(shape)
    rows = lambda width: pl.BlockSpec((tm, width), lambda i: (i, 0))
    return pl.pallas_call(
        _out_ffn_body,
        grid=(n // tm,),
        in_specs=[rows(D_MODEL),
                  pl.BlockSpec((1, H_ATT, tm, HEAD_DIM), lambda i: (i // spb, 0, i % spb, 0)),
                  rows(D_GDN), full((1, D_ATT)), full(wo.shape), full((1, D_MODEL)),
                  full(wg.shape), full(wu.shape), full(wd.shape)],
        out_specs=rows(D_MODEL),
        out_shape=jax.ShapeDtypeStruct((n, D_MODEL), f32),
        compiler_params=pltpu.CompilerParams(dimension_semantics=("arbitrary",),
                                             vmem_limit_bytes=VMEM_LIMIT),
        name="out_ffn",
    )(x2d, o_att, o_gdn, ag, wo, nf, wg, wu, wd)


def kernel(x_prompt, x_sample, cache_k, cache_v, state_ssm, state_conv, page_table, norm_mix, w_in,
           q_norm_g, k_norm_g, attn_out_g, conv_w, a_log, dt_bias, gdn_out_g, w_out, norm_ffn,
           w_gate, w_up, w_down):
    depth = w_in.shape[0]
    assert depth == 1
    batch, seq, _ = x_prompt.shape
    n_seq, t_new, _ = x_sample.shape

    w = w_in[0]
    c0, c1, c2 = 3 * D_ATT, 3 * D_ATT + 3 * D_GDN, 3 * D_ATT + 4 * D_GDN
    wqkv = w[:, :c0].astype(bf16)
    wqkvt = w.T[:c0].astype(bf16)
    wg = w[:, c0:c1].astype(bf16)
    wgate = w[:, c1:c2].astype(bf16)
    wba = jnp.pad(w[:, c2:], ((0, 0), (0, LANES - 2 * H_GDN))).astype(bf16)
    nm = norm_mix[0].reshape(1, D_MODEL)
    qg = q_norm_g[0].reshape(1, HEAD_DIM)
    qgc = q_norm_g[0].reshape(HEAD_DIM, 1)
    kg = k_norm_g[0].reshape(1, HEAD_DIM)
    kgc = k_norm_g[0].reshape(HEAD_DIM, 1)
    ag = attn_out_g[0].reshape(1, D_ATT)
    nf = norm_ffn[0].reshape(1, D_MODEL)
    cw = conv_w[0]
    pad_lo = lambda vec: jnp.pad(vec.astype(f32), (H_GDN, LANES - 2 * H_GDN)).reshape(1, LANES)
    alog_pad = pad_lo(a_log[0])
    dtb_pad = pad_lo(dt_bias[0])
    gg8 = jnp.tile(gdn_out_g[0].astype(f32), H_GDN).reshape(1, D_GDN)
    wo = w_out[0].astype(bf16)
    wgt = w_gate[0].astype(bf16)
    wup = w_up[0].astype(bf16)
    wdn = w_down[0].astype(bf16)

    xp = x_prompt.reshape(batch * seq, D_MODEL)
    qtp, ktp, vtp, gp, gatep, bap = _in_proj_prompt(xp, nm, wqkvt, wg, wgate, wba, qgc, kgc,
                                                    batch=batch, seq=seq, tm=512)
    oap = _moba_prompt(qtp, ktp, vtp)
    conv0 = jnp.zeros((batch, 8, 3 * D_GDN), f32)
    ssm0 = jnp.zeros((batch, H_GDN, HEAD_DIM, HEAD_DIM), f32)
    ogp, ssm_p = _gdn(gp.reshape(batch, seq, 3 * D_GDN), conv0, cw, bap.reshape(batch, seq, LANES),
                      gatep.reshape(batch, seq, D_GDN), alog_pad, dtb_pad, gg8, ssm0, bb=1, tb=256)
    yp = _out_ffn(xp, oap, ogp.reshape(batch * seq, D_GDN), ag, wo, nf, wgt, wup, wdn, tm=512)

    ns = n_seq * t_new
    xs = x_sample.reshape(ns, D_MODEL)
    qs, ks, vs, gs, gates, bas = _in_proj_sample(xs, nm, wqkv, wg, wgate, wba, qg, kg, tm=256)
    cache_kt = jnp.swapaxes(cache_k[0], -1, -2)
    cache_vt = jnp.swapaxes(cache_v[0], -1, -2)
    oas = _moba_sample(qs, ks, vs, cache_kt, cache_vt, page_table, t_new=t_new)
    conv_s = jnp.pad(state_conv[0], ((0, 0), (8 - (CONV_W - 1), 0), (0, 0)))
    ogs, ssm_s = _gdn(gs.reshape(n_seq, t_new, 3 * D_GDN), conv_s, cw, bas.reshape(n_seq, t_new, LANES),
                      gates.reshape(n_seq, t_new, D_GDN), alog_pad, dtb_pad, gg8, state_ssm[0],
                      bb=16, tb=t_new)
    to_heads = lambda a: a.reshape(n_seq, t_new, H_ATT, HEAD_DIM).transpose(0, 2, 1, 3)
    oas_hm = oas.reshape(1, ns, H_ATT, HEAD_DIM).transpose(0, 2, 1, 3)
    ys = _out_ffn(xs, oas_hm, ogs.reshape(ns, D_GDN), ag, wo, nf, wgt, wup, wdn, tm=512)

    gp3 = gp.reshape(batch, seq, 3 * D_GDN)
    gs3 = gs.reshape(n_seq, t_new, 3 * D_GDN)
    return (yp.reshape(batch, seq, D_MODEL), ys.reshape(n_seq, t_new, D_MODEL),
            jnp.swapaxes(ktp, -1, -2)[None], jnp.swapaxes(vtp, -1, -2)[None],
            to_heads(ks)[None], to_heads(vs)[None],
            ssm_p[None], ssm_s[None],
            gp3[:, seq - (CONV_W - 1):][None], gs3[:, t_new - (CONV_W - 1):][None])
```

```python
import functools

import numpy as np
import jax
import jax.numpy as jnp
from jax import lax
from jax.experimental import pallas as pl
from jax.experimental.pallas import tpu as pltpu

f32 = jnp.float32
bf16 = jnp.bfloat16

D_MODEL = 1024
HEAD_DIM = 64
H_ATT = 8
H_GDN = 8
D_ATT = H_ATT * HEAD_DIM
D_GDN = H_GDN * HEAD_DIM
MOBA_BLOCK = 256
MOBA_SHIFT = 8
MOBA_TOPK = 3
GDN_CHUNK = 64
GDN_SOLVE = 128
CONV_W = 4
PAGE_SIZE = 128
RMS_EPS = 1e-6
L2_EPS = 1e-6
ATT_SCALE = HEAD_DIM ** -0.5
LOG2E = float(np.log2(np.e))
NEG = -1e30
LANES = 128
SUBLANES = 8
VMEM_LIMIT = 56 * 1024 * 1024

HI = lax.Precision.HIGHEST
NN = (((1,), (0,)), ((), ()))
NT = (((1,), (1,)), ((), ()))
TN = (((0,), (0,)), ((), ()))

ALIBI_SLOPES = [2.0 ** (-8.0 * (i + 1) / H_ATT) for i in range(H_ATT)]


def _dot(a, b, dims=NN, precision=None):
    return lax.dot_general(a, b, dims, precision=precision, preferred_element_type=f32)


def _bdot(a, b, dims=NN):
    return lax.dot_general(a.astype(bf16), b.astype(bf16), dims, preferred_element_type=f32)


def _rms(x, axis=-1):
    return x * lax.rsqrt(jnp.mean(x * x, axis=axis, keepdims=True) + RMS_EPS)


def _silu(x):
    return x * jax.nn.sigmoid(x)


def _iota(shape, dim):
    return lax.broadcasted_iota(jnp.int32, shape, dim)


def _resident(shape):
    return pl.BlockSpec(shape, lambda i: (0,) * len(shape), pipeline_mode=pl.Buffered(1))


def _in_proj_prompt_body(x_ref, nm_ref, wqkvt_ref, wg_ref, wgate_ref, wba_ref, qgc_ref, kgc_ref,
                         qt_ref, kt_ref, vt_ref, g_ref, gate_ref, ba_ref):
    x = x_ref[...]
    xb = (_rms(x) * nm_ref[...]).astype(bf16)
    z = _dot(wqkvt_ref[...], xb, NT)
    for h in range(H_ATT):
        lo, hi = h * HEAD_DIM, (h + 1) * HEAD_DIM
        qt_ref[0, h] = _rms(z[lo:hi, :], axis=0) * qgc_ref[...]
        kt_ref[0, h] = _rms(z[D_ATT + lo:D_ATT + hi, :], axis=0) * kgc_ref[...]
        vt_ref[0, h] = z[2 * D_ATT + lo:2 * D_ATT + hi, :]
    g_ref[...] = jnp.dot(xb, wg_ref[...], preferred_element_type=f32)
    gate_ref[...] = jnp.dot(xb, wgate_ref[...], preferred_element_type=f32)
    ba_ref[...] = jnp.dot(xb, wba_ref[...], preferred_element_type=f32)


def _in_proj_prompt(x2d, nm, wqkvt, wg, wgate, wba, qgc, kgc, *, batch, seq, tm):
    n = x2d.shape[0]
    assert n == batch * seq and seq % tm == 0
    spb = seq // tm
    full = lambda shape: _resident(shape)
    rows = lambda width: pl.BlockSpec((tm, width), lambda i: (i, 0))
    t_spec = pl.BlockSpec((1, H_ATT, HEAD_DIM, tm), lambda i: (i // spb, 0, 0, i % spb))
    t_shape = jax.ShapeDtypeStruct((batch, H_ATT, HEAD_DIM, seq), f32)
    return pl.pallas_call(
        _in_proj_prompt_body,
        grid=(n // tm,),
        in_specs=[rows(D_MODEL), full((1, D_MODEL)), full(wqkvt.shape), full(wg.shape),
                  full(wgate.shape), full(wba.shape), full((HEAD_DIM, 1)), full((HEAD_DIM, 1))],
        out_specs=[t_spec, t_spec, t_spec, rows(3 * D_GDN), rows(D_GDN), rows(LANES)],
        out_shape=[t_shape, t_shape, t_shape,
                   jax.ShapeDtypeStruct((n, 3 * D_GDN), f32),
                   jax.ShapeDtypeStruct((n, D_GDN), f32),
                   jax.ShapeDtypeStruct((n, LANES), f32)],
        compiler_params=pltpu.CompilerParams(dimension_semantics=("arbitrary",),
                                             vmem_limit_bytes=VMEM_LIMIT),
        name="in_proj_prompt",
    )(x2d, nm, wqkvt, wg, wgate, wba, qgc, kgc)


def _in_proj_sample_body(x_ref, nm_ref, wqkv_ref, wg_ref, wgate_ref, wba_ref, qg_ref, kg_ref,
                         q_ref, k_ref, v_ref, g_ref, gate_ref, ba_ref):
    x = x_ref[...]
    xb = (_rms(x) * nm_ref[...]).astype(bf16)
    z = jnp.dot(xb, wqkv_ref[...], preferred_element_type=f32)
    qs, ks = [], []
    for h in range(H_ATT):
        lo, hi = h * HEAD_DIM, (h + 1) * HEAD_DIM
        qs.append(_rms(z[:, lo:hi]) * qg_ref[...])
        ks.append(_rms(z[:, D_ATT + lo:D_ATT + hi]) * kg_ref[...])
    q_ref[...] = jnp.concatenate(qs, axis=-1)
    k_ref[...] = jnp.concatenate(ks, axis=-1)
    v_ref[...] = z[:, 2 * D_ATT:]
    g_ref[...] = jnp.dot(xb, wg_ref[...], preferred_element_type=f32)
    gate_ref[...] = jnp.dot(xb, wgate_ref[...], preferred_element_type=f32)
    ba_ref[...] = jnp.dot(xb, wba_ref[...], preferred_element_type=f32)


def _in_proj_sample(x2d, nm, wqkv, wg, wgate, wba, qg, kg, *, tm):
    n = x2d.shape[0]
    assert n % tm == 0
    full = lambda shape: _resident(shape)
    rows = lambda width: pl.BlockSpec((tm, width), lambda i: (i, 0))
    tok = jax.ShapeDtypeStruct((n, D_ATT), f32)
    return pl.pallas_call(
        _in_proj_sample_body,
        grid=(n // tm,),
        in_specs=[rows(D_MODEL), full((1, D_MODEL)), full(wqkv.shape), full(wg.shape),
                  full(wgate.shape), full(wba.shape), full((1, HEAD_DIM)), full((1, HEAD_DIM))],
        out_specs=[rows(D_ATT), rows(D_ATT), rows(D_ATT), rows(3 * D_GDN), rows(D_GDN), rows(LANES)],
        out_shape=[tok, tok, tok,
                   jax.ShapeDtypeStruct((n, 3 * D_GDN), f32),
                   jax.ShapeDtypeStruct((n, D_GDN), f32),
                   jax.ShapeDtypeStruct((n, LANES), f32)],
        compiler_params=pltpu.CompilerParams(dimension_semantics=("arbitrary",),
                                             vmem_limit_bytes=VMEM_LIMIT),
        name="in_proj_sample",
    )(x2d, nm, wqkv, wg, wgate, wba, qg, kg)


def _fold_rows(s, op):
    parts = [s[r0:r0 + SUBLANES] for r0 in range(0, s.shape[0], SUBLANES)]
    while len(parts) > 1:
        parts = [op(parts[k], parts[k + 1]) for k in range(0, len(parts), 2)]
    return parts[0]


MOBA_HEADS = 2
MOBA_TRIP = 16


def _moba_prompt_body(slope_ref, qt_ref, qtn_ref, kt_ref, vt_ref, o_ref, ka_ref, va_ref, kmean_ref, s_ref,
                      qa_ref, *, seq):
    hg = pl.program_id(1)
    i = pl.program_id(2)
    n_blk = seq // MOBA_BLOCK
    trip = MOBA_TRIP * MOBA_BLOCK
    slopes = [slope_ref[pl.ds(hg * MOBA_HEADS + j, 1), 0:1] for j in range(MOBA_HEADS)]
    nrow = _iota((n_blk, MOBA_BLOCK), 0)
    nrow_f = nrow.astype(f32)

    def augmented_qt(qt, tile, j):
        tvec = jnp.full((n_blk, MOBA_BLOCK), tile, jnp.int32)
        gt = _dot(kmean_ref[j], qt, NN, HI)
        gt = jnp.where(nrow < tvec, gt, -jnp.inf)
        picked = jnp.zeros((n_blk, MOBA_BLOCK), f32)
        for _ in range(MOBA_TOPK):
            best = jnp.max(gt, axis=0, keepdims=True)
            first_best = jnp.min(jnp.where(gt == best, nrow_f, float(n_blk)), axis=0, keepdims=True)
            chosen = nrow_f == first_best
            picked = jnp.where(chosen, 1.0, picked)
            gt = jnp.where(chosen, -jnp.inf, gt)
        sel = (nrow < tvec) & (picked > 0.0)
        shift = -(slopes[j] * MOBA_BLOCK) * tile.astype(f32)
        c_t = jnp.where(nrow < 2, 1.0, jnp.where(nrow == 2, shift, 0.0))
        return jnp.concatenate([qt * ATT_SCALE, jnp.where(sel, 0.0, NEG), c_t], axis=0).astype(bf16)

    @pl.when(i == 0)
    def _():
        row = _iota((HEAD_DIM, seq), 0)
        pos = _iota((HEAD_DIM, seq), 1)
        blk = jnp.right_shift(pos, MOBA_SHIFT)
        jloc = jnp.bitwise_and(pos, MOBA_BLOCK - 1)
        onehot = jnp.where(row == blk, 1.0, 0.0)
        ones_row = jnp.where(row == 0, 1.0, 0.0).astype(bf16)
        eye_b = (_iota((MOBA_BLOCK, MOBA_BLOCK), 0) == _iota((MOBA_BLOCK, MOBA_BLOCK), 1)).astype(bf16)
        for j in range(MOBA_HEADS):
            kt = kt_ref[0, j]
            e = jnp.where(row == n_blk, slopes[j] * jloc.astype(f32), onehot)
            e = jnp.where(row == n_blk + 1, (slopes[j] * MOBA_BLOCK) * blk.astype(f32), e)
            e = jnp.where(row == n_blk + 2, 1.0, e)
            ka_t = jnp.concatenate([kt.astype(bf16), e.astype(bf16)], axis=0)
            for n in range(n_blk):
                blk_cols = slice(n * MOBA_BLOCK, (n + 1) * MOBA_BLOCK)
                ka_ref[j, blk_cols, :] = _dot(eye_b, ka_t[:, blk_cols], NT).astype(bf16)
            va_ref[j, 0:HEAD_DIM, :] = vt_ref[0, j].astype(bf16)
            va_ref[j, HEAD_DIM:, :] = ones_row
            sums = [jnp.sum(kt[:, n * MOBA_BLOCK:(n + 1) * MOBA_BLOCK], axis=1, keepdims=True)
                    for n in range(n_blk)]
            kmean_ref[j] = (jnp.concatenate(sums, axis=1) * (1.0 / MOBA_BLOCK)).T
        for j in range(MOBA_HEADS):
            qa_ref[j] = augmented_qt(qt_ref[0, j], i, j)

    arow = _iota((2 * HEAD_DIM, MOBA_BLOCK), 0)
    pen_rows = (arow >= HEAD_DIM) & (arow < HEAD_DIM + n_blk)
    causal = _iota((MOBA_BLOCK, MOBA_BLOCK), 0) <= _iota((MOBA_BLOCK, MOBA_BLOCK), 1)
    d0 = pl.multiple_of(i * MOBA_BLOCK, MOBA_BLOCK)
    qa = [qa_ref[j] for j in range(MOBA_HEADS)]
    s_own = []
    for j in range(MOBA_HEADS):
        qo = jnp.where(pen_rows, jnp.zeros_like(qa[j]), qa[j])
        s_own.append(jnp.where(causal, _dot(ka_ref[j, pl.ds(d0, MOBA_BLOCK), :], qo) * LOG2E, NEG))
    half, quarter = trip // 2, trip // 4
    qb = MOBA_TRIP // 4
    rem = i % MOBA_TRIP
    as_int = lambda cond: cond.astype(jnp.int32)
    n_full = i // MOBA_TRIP + as_int(rem > 3 * qb)
    n_half = as_int((rem > qb) & (rem <= 3 * qb))
    n_quarter = as_int(((rem > 0) & (rem <= qb)) | ((rem > 2 * qb) & (rem <= 3 * qb)))
    half0 = n_full * trip
    quarter0 = half0 + n_half * half

    def pass1(size, base):
        def body(t, mx):
            r0 = pl.multiple_of(base + t * size, quarter)
            out = []
            for j in range(MOBA_HEADS):
                s = _dot(ka_ref[j, pl.ds(r0, size), :], qa[j]) * LOG2E
                s_ref[j, pl.ds(r0, size), :] = s
                out.append(jnp.maximum(mx[j], _fold_rows(s, jnp.maximum)))
            return tuple(out)
        return body

    mx = lax.fori_loop(0, n_full, pass1(trip, 0), tuple(_fold_rows(s, jnp.maximum) for s in s_own))
    mx = lax.fori_loop(0, n_half, pass1(half, half0), mx)
    mx = lax.fori_loop(0, n_quarter, pass1(quarter, quarter0), mx)
    for j in range(MOBA_HEADS):
        qa_ref[j] = augmented_qt(qtn_ref[0, j], i + 1, j)
    m_row = [jnp.max(mx[j], axis=0, keepdims=True) for j in range(MOBA_HEADS)]

    def pass2(size, base):
        def body(t, acc):
            r0 = pl.multiple_of(base + t * size, quarter)
            return tuple(acc[j] + _dot(va_ref[j, :, pl.ds(r0, size)],
                                       jnp.exp2((s_ref[j, pl.ds(r0, size), :] - m_row[j]).astype(bf16)))
                         for j in range(MOBA_HEADS))
        return body

    acc = lax.fori_loop(0, n_full, pass2(trip, 0),
                        tuple(_dot(va_ref[j, :, pl.ds(d0, MOBA_BLOCK)], jnp.exp2((s_own[j] - m_row[j]).astype(bf16)))
                              for j in range(MOBA_HEADS)))
    acc = lax.fori_loop(0, n_half, pass2(half, half0), acc)
    acc = lax.fori_loop(0, n_quarter, pass2(quarter, quarter0), acc)
    for j in range(MOBA_HEADS):
        o_ref[0, j] = (acc[j][:HEAD_DIM] / acc[j][HEAD_DIM:HEAD_DIM + 1]).T


def _moba_prompt(qt, kt, vt):
    batch, heads, _, seq = qt.shape
    n_blk = seq // MOBA_BLOCK
    assert n_blk % MOBA_TRIP == 0 and 2 * n_blk == HEAD_DIM and MOBA_BLOCK == 1 << MOBA_SHIFT
    assert heads % MOBA_HEADS == 0
    slopes = jnp.asarray(np.repeat(np.asarray(ALIBI_SLOPES, np.float32)[:, None], LANES, axis=1))
    tile = pl.BlockSpec((1, MOBA_HEADS, HEAD_DIM, MOBA_BLOCK), lambda b, h, i: (b, h, 0, i))
    next_tile = pl.BlockSpec((1, MOBA_HEADS, HEAD_DIM, MOBA_BLOCK),
                             lambda b, h, i: (b, h, 0, jnp.minimum(i + 1, n_blk - 1)))
    whole = pl.BlockSpec((1, MOBA_HEADS, HEAD_DIM, seq), lambda b, h, i: (b, h, 0, 0),
                         pipeline_mode=pl.Buffered(1))
    return pl.pallas_call(
        functools.partial(_moba_prompt_body, seq=seq),
        grid=(batch, heads // MOBA_HEADS, n_blk),
        in_specs=[pl.BlockSpec((H_ATT, LANES), lambda b, h, i: (0, 0)), tile, next_tile, whole, whole],
        out_specs=pl.BlockSpec((1, MOBA_HEADS, MOBA_BLOCK, HEAD_DIM), lambda b, h, i: (b, h, i, 0)),
        out_shape=jax.ShapeDtypeStruct((batch, heads, seq, HEAD_DIM), f32),
        scratch_shapes=[pltpu.VMEM((MOBA_HEADS, seq, 2 * HEAD_DIM), bf16),
                        pltpu.VMEM((MOBA_HEADS, 2 * HEAD_DIM, seq), bf16),
                        pltpu.VMEM((MOBA_HEADS, n_blk, HEAD_DIM), f32),
                        pltpu.VMEM((MOBA_HEADS, seq, MOBA_BLOCK), f32),
                        pltpu.VMEM((MOBA_HEADS, 2 * HEAD_DIM, MOBA_BLOCK), bf16)],
        compiler_params=pltpu.CompilerParams(dimension_semantics=("arbitrary",) * 3,
                                             vmem_limit_bytes=VMEM_LIMIT),
        name="moba_prompt",
    )(slopes, qt, qt, kt, vt)


SAMPLE_SEQS = 2


def _moba_sample_body(pt_ref, q_ref, k_ref, v_ref, *refs, past_len, t_new, spb):
    del pt_ref
    n_pages = past_len // PAGE_SIZE
    kp_refs, vp_refs = refs[:spb * n_pages], refs[spb * n_pages:2 * spb * n_pages]
    o_ref = refs[2 * spb * n_pages]
    own = past_len // MOBA_BLOCK
    n_sel = min(MOBA_TOPK, own)
    tq = _iota((t_new, past_len), 0)
    sk = _iota((t_new, past_len), 1)
    dist = (past_len + tq - sk).astype(f32)
    tq2 = _iota((t_new, t_new), 0)
    sk2 = _iota((t_new, t_new), 1)
    units = [(s, h) for s in range(spb) for h in range(H_ATT)]
    us = range(len(units))
    rows = [slice(s * t_new, (s + 1) * t_new) for s, _ in units]
    cols = [slice(h * HEAD_DIM, (h + 1) * HEAD_DIM) for _, h in units]
    slope = [ALIBI_SLOPES[h] for _, h in units]
    q = [q_ref[rows[u], cols[u]] for u in us]
    qb = [(q[u] * ATT_SCALE).astype(bf16) for u in us]
    kb = [jnp.concatenate([kp_refs[s * n_pages + p][0, h].astype(bf16) for p in range(n_pages)], axis=1)
          for s, h in units]
    s_raw = [_dot(qb[u], kb[u]) for u in us]
    gate = [jnp.concatenate([jnp.sum(s_raw[u][:, n * MOBA_BLOCK:(n + 1) * MOBA_BLOCK], axis=1, keepdims=True)
                             for n in range(own)], axis=1) for u in us]

    def penalty(g, n):
        gn = g[:, n:n + 1]
        rank = jnp.zeros((t_new, 1), f32)
        for m in range(own):
            if m != n:
                gm = g[:, m:m + 1]
                beats = (gm > gn) | ((gm == gn) & (m < n))
                rank = rank + jnp.where(beats, 1.0, 0.0)
        return jnp.broadcast_to(jnp.where(rank < n_sel, 0.0, NEG), (t_new, MOBA_BLOCK))

    pen = [jnp.concatenate([penalty(gate[u], n) for n in range(own)], axis=1) for u in us]
    s_past = [s_raw[u] + pen[u] - slope[u] * dist for u in us]
    s_own = [jnp.where(sk2 <= tq2,
                       _dot(qb[u], k_ref[rows[u], cols[u]].astype(bf16), NT) - slope[u] * (tq2 - sk2).astype(f32),
                       NEG) for u in us]
    m = [jnp.maximum(jnp.max(s_past[u], axis=1, keepdims=True), jnp.max(s_own[u], axis=1, keepdims=True))
         for u in us]
    p_past = [jnp.exp(s_past[u] - m[u]) for u in us]
    p_own = [jnp.exp(s_own[u] - m[u]) for u in us]
    denom = [jnp.sum(p_past[u], axis=1, keepdims=True) + jnp.sum(p_own[u], axis=1, keepdims=True) for u in us]
    vb = [jnp.concatenate([vp_refs[s * n_pages + p][0, h].astype(bf16) for p in range(n_pages)], axis=1)
          for s, h in units]
    o = [(_dot(p_own[u].astype(bf16), v_ref[rows[u], cols[u]].astype(bf16))
          + _dot(p_past[u].astype(bf16), vb[u], NT)) / denom[u] for u in us]
    for s in range(spb):
        o_ref[s * t_new:(s + 1) * t_new, :] = jnp.concatenate(o[s * H_ATT:(s + 1) * H_ATT], axis=-1)


def _moba_sample(q, k, v, cache_kt, cache_vt, page_table, *, t_new):
    n_seq, n_pages = page_table.shape
    past_len = n_pages * PAGE_SIZE
    assert past_len % MOBA_BLOCK == 0 and past_len // MOBA_BLOCK >= 1
    spb = SAMPLE_SEQS
    assert n_seq % spb == 0
    rows = pl.BlockSpec((spb * t_new, D_ATT), lambda b, pt: (b, 0))

    def page_spec(s, p):
        return pl.BlockSpec((1, H_ATT, HEAD_DIM, PAGE_SIZE),
                            lambda b, pt: (pt[(b * spb + s) * n_pages + p], 0, 0, 0))

    pages = [page_spec(s, p) for s in range(spb) for p in range(n_pages)]
    grid_spec = pltpu.PrefetchScalarGridSpec(
        num_scalar_prefetch=1,
        grid=(n_seq // spb,),
        in_specs=[rows, rows, rows] + pages * 2,
        out_specs=rows,
    )
    return pl.pallas_call(
        functools.partial(_moba_sample_body, past_len=past_len, t_new=t_new, spb=spb),
        grid_spec=grid_spec,
        out_shape=jax.ShapeDtypeStruct(q.shape, f32),
        compiler_params=pltpu.CompilerParams(dimension_semantics=("arbitrary",),
                                             vmem_limit_bytes=VMEM_LIMIT),
        name="moba_sample",
    )(page_table.reshape(-1), q, k, v, *([cache_kt] * (spb * n_pages)), *([cache_vt] * (spb * n_pages)))


def _conv_silu(cur, prev8, w):
    t = cur.shape[0]
    row8 = _iota((SUBLANES, cur.shape[1]), 0)
    y = None
    for i in range(CONV_W):
        d = CONV_W - 1 - i
        if d == 0:
            term = cur
        else:
            sh = pltpu.roll(cur, d, 0)
            top = jnp.where(row8 < d, pltpu.roll(prev8, d, 0), sh[0:SUBLANES])
            term = top if t == SUBLANES else jnp.concatenate([top, sh[SUBLANES:]], axis=0)
        term = term * w[i:i + 1, :]
        y = term if y is None else y + term
    return _silu(y)


def _conv_silu_stacked(cur, prev, w):
    n = cur.shape[0]
    pos = jnp.bitwise_and(_iota(cur.shape, 0), SUBLANES - 1)
    y = None
    for i in range(CONV_W):
        d = CONV_W - 1 - i
        if d == 0:
            term = cur
        else:
            term = jnp.where(pos < d, pltpu.roll(prev, (n - SUBLANES + d) % n, 0), pltpu.roll(cur, d, 0))
        term = term * w[i:i + 1, :]
        y = term if y is None else y + term
    return _silu(y)


def _gdn_body(uq_ref, uk_ref, uv_ref, pq_ref, pk_ref, pv_ref, cq_ref, ck_ref, cv_ref,
              wq_ref, wk_ref, wv_ref, ba_ref, gate_ref, alog_ref, dtb_ref, gg_ref, m0_ref,
              o_ref, m_ref, *, bb, tb, c, flat):
    t = pl.program_id(1)

    @pl.when(t == 0)
    def _():
        m_ref[...] = m0_ref[...]

    seqs = range(1) if flat else range(bb)
    stacked = bb * tb
    blk = (lambda ref, s: ref[...].reshape(stacked, ref.shape[-1])) if flat else (lambda ref, s: ref[s])
    tb = stacked if flat else tb
    n_chunks = tb // c
    lc = int(np.log2(c))
    assert 1 << lc == c
    ri = _iota((tb, tb), 0)
    ci = _iota((tb, tb), 1)
    same = jnp.right_shift(ri, lc) == jnp.right_shift(ci, lc)
    eye = ri == ci
    keep = same & (ri >= ci)
    cum = jnp.where(keep, 1.0, 0.0)
    tot = jnp.where(same, 1.0, 0.0)
    sb = min(tb, GDN_SOLVE)
    assert tb % sb == 0 and sb % c == 0
    rs_i = _iota((sb, sb), 0)
    cs_i = _iota((sb, sb), 1)
    eye_s = rs_i == cs_i
    keep_s = (jnp.right_shift(rs_i, lc) == jnp.right_shift(cs_i, lc)) & (rs_i >= cs_i)
    ident_s = jnp.where(eye_s, 1.0, 0.0)
    split = lambda z: [z[:, h * HEAD_DIM:(h + 1) * HEAD_DIM] for h in range(H_GDN)]
    if tb >= LANES:
        head_ones = jnp.where(_iota((D_GDN, D_GDN), 0) // HEAD_DIM == _iota((D_GDN, D_GDN), 1) // HEAD_DIM,
                              1.0, 0.0)
        l2_heads = lambda z: split(z * lax.rsqrt(_bdot(z * z, head_ones) + L2_EPS))
        rms_cat = lambda o: o * lax.rsqrt(_bdot(o * o, head_ones) * (1.0 / HEAD_DIM) + RMS_EPS)
        rms_heads = lambda parts: rms_cat(jnp.concatenate(parts, axis=-1))
    else:
        l2_heads = lambda z: [zh * lax.rsqrt(jnp.sum(zh * zh, axis=-1, keepdims=True) + L2_EPS) for zh in split(z)]
        rms_heads = lambda parts: jnp.concatenate([_rms(p) for p in parts], axis=-1)
    first = jnp.full((SUBLANES, D_GDN), t, jnp.int32) == 0

    for s in seqs:
        if flat:
            q = _conv_silu_stacked(blk(uq_ref, s), blk(cq_ref, s), wq_ref[...])
            k = _conv_silu_stacked(blk(uk_ref, s), blk(ck_ref, s), wk_ref[...])
            v = _conv_silu_stacked(blk(uv_ref, s), blk(cv_ref, s), wv_ref[...])
        else:
            q = _conv_silu(uq_ref[s], jnp.where(first, cq_ref[s], pq_ref[s]), wq_ref[...])
            k = _conv_silu(uk_ref[s], jnp.where(first, ck_ref[s], pk_ref[s]), wk_ref[...])
            v = _conv_silu(uv_ref[s], jnp.where(first, cv_ref[s], pv_ref[s]), wv_ref[...])
        ba = blk(ba_ref, s)
        beta_all = jax.nn.sigmoid(ba)
        xa = ba + dtb_ref[...]
        softplus = jnp.maximum(xa, 0.0) + jnp.log1p(jnp.exp(-jnp.abs(xa)))
        g_all = -jnp.exp(alog_ref[...]) * softplus
        gcum_all = _dot(cum, g_all, NN, HI)
        gtot_all = _dot(tot, g_all, NN, HI)
        gcum_t = gcum_all.T if tb % LANES == 0 else None
        hs = range(H_GDN)
        beta = [beta_all[:, h:h + 1] for h in hs]
        gcum = [gcum_all[:, H_GDN + h:H_GDN + h + 1] for h in hs]
        gtot = [gtot_all[:, H_GDN + h:H_GDN + h + 1] for h in hs]
        if gcum_t is None:
            grow = [jnp.sum(jnp.where(eye, gcum[h], 0.0), axis=0, keepdims=True) for h in hs]
        else:
            grow = [gcum_t[H_GDN + h:H_GDN + h + 1, :] for h in hs]
        qh = [z * (HEAD_DIM ** -0.5) for z in l2_heads(q)]
        kh = l2_heads(k)
        vh = [v[:, h * HEAD_DIM:(h + 1) * HEAD_DIM] for h in hs]
        eg = [jnp.exp(gcum[h]) for h in hs]
        rhs = [jnp.concatenate([beta[h] * vh[h], (beta[h] * eg[h]) * kh[h]], axis=-1) for h in hs]
        units = [(h, slice(r0, r0 + sb)) for h in hs for r0 in range(0, tb, sb)]
        decay = [jnp.exp(jnp.where(keep_s, gcum[h][rs] - grow[h][:, rs], NEG)) for h, rs in units]
        pw = [beta[h][rs] * _bdot(kh[h][rs], kh[h][rs], NT) * jnp.where(eye_s, 0.0, decay[u])
              for u, (h, rs) in enumerate(units)]
        x = [ident_s - a for a in pw]
        for _ in range(lc - 1):
            pw = [_bdot(a, a) for a in pw]
            x = [xu + _bdot(xu, a) for xu, a in zip(x, pw)]
        sol = [_bdot(x[u], rhs[h][rs]) for u, (h, rs) in enumerate(units)]
        aqk = [_bdot(qh[h][rs], kh[h][rs], NT) * decay[u] for u, (h, rs) in enumerate(units)]
        qg = [qh[h] * eg[h] for h in hs]
        kd = [kh[h] * jnp.exp(gtot[h] - gcum[h]) for h in hs]
        gdec = [jnp.exp(gtot[h]) for h in hs]
        m = None if flat else [m_ref[s, h] for h in hs]
        n_sub = tb // sb
        deltas = [[] for _ in hs]
        oqs = [[] for _ in hs]
        for ch in range(n_chunks):
            if flat:
                m = [m_ref[ch, h] for h in hs]
            sl = slice(ch * c, (ch + 1) * c)
            lsl = slice((ch * c) % sb, (ch * c) % sb + c)
            su = [sol[h * n_sub + (ch * c) // sb] for h in hs]
            both = [_bdot(jnp.concatenate([su[h][lsl, HEAD_DIM:], qg[h][sl]], axis=0), m[h]) for h in hs]
            for h in hs:
                deltas[h].append(su[h][lsl, :HEAD_DIM] - both[h][:c])
                oqs[h].append(both[h][c:])
            m = [gdec[h][ch * c:ch * c + 1, :] * m[h] + _dot(kd[h][sl], deltas[h][ch], TN) for h in hs]
            if flat:
                for h in hs:
                    m_ref[ch, h] = m[h]
        cps = sb // c
        cat = lambda parts: parts[0] if len(parts) == 1 else jnp.concatenate(parts, axis=0)
        outs = []
        for h in hs:
            if not flat:
                m_ref[s, h] = m[h]
            intra = cat([_bdot(aqk[h * n_sub + r], cat(deltas[h][r * cps:(r + 1) * cps])) for r in range(n_sub)])
            outs.append(cat(oqs[h]) + intra)
        o = rms_heads(outs) * gg_ref[...] * _silu(blk(gate_ref, s))
        if flat:
            o_ref[...] = o.reshape(o_ref.shape)
        else:
            o_ref[s] = o


def _gdn(u, conv_pad, conv_w, ba, gate, alog_pad, dtb_pad, gg8, m0, *, bb, tb):
    batch, seq, _ = u.shape
    c = min(GDN_CHUNK, seq)
    assert batch % bb == 0 and seq % tb == 0 and tb % c == 0 and tb % SUBLANES == 0
    flat = bb > 1
    assert not flat or (seq == tb == c == SUBLANES)
    cur = lambda g: pl.BlockSpec((bb, tb, D_GDN), lambda b, t: (b, t, g))
    prev = lambda g: pl.BlockSpec((bb, SUBLANES, D_GDN),
                                  lambda b, t: (b, jnp.maximum(t * (tb // SUBLANES) - 1, 0), g))
    cbuf = lambda g: pl.BlockSpec((bb, SUBLANES, D_GDN), lambda b, t: (b, 0, g))
    cw = lambda g: pl.BlockSpec((CONV_W, D_GDN), lambda b, t: (0, g))
    row = lambda width: pl.BlockSpec((1, width), lambda b, t: (0, 0))
    state = pl.BlockSpec((bb, H_GDN, HEAD_DIM, HEAD_DIM), lambda b, t: (b, 0, 0, 0))
    return pl.pallas_call(
        functools.partial(_gdn_body, bb=bb, tb=tb, c=c, flat=flat),
        grid=(batch // bb, seq // tb),
        in_specs=[cur(0), cur(1), cur(2), prev(0), prev(1), prev(2), cbuf(0), cbuf(1), cbuf(2),
                  cw(0), cw(1), cw(2),
                  pl.BlockSpec((bb, tb, LANES), lambda b, t: (b, t, 0)),
                  pl.BlockSpec((bb, tb, D_GDN), lambda b, t: (b, t, 0)),
                  row(LANES), row(LANES), row(D_GDN), state],
        out_specs=[pl.BlockSpec((bb, tb, D_GDN), lambda b, t: (b, t, 0)), state],
        out_shape=[jax.ShapeDtypeStruct((batch, seq, D_GDN), f32),
                   jax.ShapeDtypeStruct((batch, H_GDN, HEAD_DIM, HEAD_DIM), f32)],
        compiler_params=pltpu.CompilerParams(dimension_semantics=("arbitrary",) * 2,
                                             vmem_limit_bytes=VMEM_LIMIT),
        name="gdn",
    )(u, u, u, u, u, u, conv_pad, conv_pad, conv_pad, conv_w, conv_w, conv_w,
      ba, gate, alog_pad, dtb_pad, gg8, m0)


FF_CHUNK = 256


def _out_ffn_body(x_ref, oa_ref, og_ref, ag_ref, wo_ref, nf_ref, wg_ref, wu_ref, wd_ref, y_ref):
    oa = jnp.concatenate([oa_ref[0, h] for h in range(H_ATT)], axis=-1)
    oa = _rms(oa) * ag_ref[...]
    mix = jnp.concatenate([oa, og_ref[...]], axis=-1).astype(bf16)
    hid = x_ref[...] + jnp.dot(mix, wo_ref[...], preferred_element_type=f32)
    ub = (_rms(hid) * nf_ref[...]).astype(bf16)
    d_ff = wg_ref.shape[1]
    acc = hid
    for cidx in range(d_ff // FF_CHUNK):
        sl = slice(cidx * FF_CHUNK, (cidx + 1) * FF_CHUNK)
        a = jnp.dot(ub, wg_ref[:, sl], preferred_element_type=f32)
        b = jnp.dot(ub, wu_ref[:, sl], preferred_element_type=f32)
        acc = acc + jnp.dot((_silu(a) * b).astype(bf16), wd_ref[sl, :], preferred_element_type=f32)
    y_ref[...] = acc


def _out_ffn(x2d, o_att, o_gdn, ag, wo, nf, wg, wu, wd, *, tm):
    n = x2d.shape[0]
    batch, _, seq, _ = o_att.shape
    assert n == batch * seq and seq % tm == 0 and wg.shape[1] % FF_CHUNK == 0
    spb = seq // tm
    full = lambda shape: _resident(shape)
    rows = lambda width: pl.BlockSpec((tm, width), lambda i: (i, 0))
    return pl.pallas_call(
        _out_ffn_body,
        grid=(n // tm,),
        in_specs=[rows(D_MODEL),
                  pl.BlockSpec((1, H_ATT, tm, HEAD_DIM), lambda i: (i // spb, 0, i % spb, 0)),
                  rows(D_GDN), full((1, D_ATT)), full(wo.shape), full((1, D_MODEL)),
                  full(wg.shape), full(wu.shape), full(wd.shape)],
        out_specs=rows(D_MODEL),
        out_shape=jax.ShapeDtypeStruct((n, D_MODEL), f32),
        compiler_params=pltpu.CompilerParams(dimension_semantics=("arbitrary",),
                                             vmem_limit_bytes=VMEM_LIMIT),
        name="out_ffn",
    )(x2d, o_att, o_gdn, ag, wo, nf, wg, wu, wd)


def kernel(x_prompt, x_sample, cache_k, cache_v, state_ssm, state_conv, page_table, norm_mix, w_in,
           q_norm_g, k_norm_g, attn_out_g, conv_w, a_log, dt_bias, gdn_out_g, w_out, norm_ffn,
           w_gate, w_up, w_down):
    depth = w_in.shape[0]
    assert depth == 1
    batch, seq, _ = x_prompt.shape
    n_seq, t_new, _ = x_sample.shape

    w = w_in[0]
    c0, c1, c2 = 3 * D_ATT, 3 * D_ATT + 3 * D_GDN, 3 * D_ATT + 4 * D_GDN
    wqkv = w[:, :c0].astype(bf16)
    wqkvt = w.T[:c0].astype(bf16)
    wg = w[:, c0:c1].astype(bf16)
    wgate = w[:, c1:c2].astype(bf16)
    wba = jnp.pad(w[:, c2:], ((0, 0), (0, LANES - 2 * H_GDN))).astype(bf16)
    nm = norm_mix[0].reshape(1, D_MODEL)
    qg = q_norm_g[0].reshape(1, HEAD_DIM)
    qgc = q_norm_g[0].reshape(HEAD_DIM, 1)
    kg = k_norm_g[0].reshape(1, HEAD_DIM)
    kgc = k_norm_g[0].reshape(HEAD_DIM, 1)
    ag = attn_out_g[0].reshape(1, D_ATT)
    nf = norm_ffn[0].reshape(1, D_MODEL)
    cw = conv_w[0]
    pad_lo = lambda vec: jnp.pad(vec.astype(f32), (H_GDN, LANES - 2 * H_GDN)).reshape(1, LANES)
    alog_pad = pad_lo(a_log[0])
    dtb_pad = pad_lo(dt_bias[0])
    gg8 = jnp.tile(gdn_out_g[0].astype(f32), H_GDN).reshape(1, D_GDN)
    wo = w_out[0].astype(bf16)
    wgt = w_gate[0].astype(bf16)
    wup = w_up[0].astype(bf16)
    wdn = w_down[0].astype(bf16)

    xp = x_prompt.reshape(batch * seq, D_MODEL)
    qtp, ktp, vtp, gp, gatep, bap = _in_proj_prompt(xp, nm, wqkvt, wg, wgate, wba, qgc, kgc,
                                                    batch=batch, seq=seq, tm=1024)
    oap = _moba_prompt(qtp, ktp, vtp)
    conv0 = jnp.zeros((batch, SUBLANES, 3 * D_GDN), f32)
    ssm0 = jnp.zeros((batch, H_GDN, HEAD_DIM, HEAD_DIM), f32)
    ogp, ssm_p = _gdn(gp.reshape(batch, seq, 3 * D_GDN), conv0, cw, bap.reshape(batch, seq, LANES),
                      gatep.reshape(batch, seq, D_GDN), alog_pad, dtb_pad, gg8, ssm0, bb=1, tb=256)
    yp = _out_ffn(xp, oap, ogp.reshape(batch * seq, D_GDN), ag, wo, nf, wgt, wup, wdn, tm=512)

    ns = n_seq * t_new
    xs = x_sample.reshape(ns, D_MODEL)
    qs, ks, vs, gs, gates, bas = _in_proj_sample(xs, nm, wqkv, wg, wgate, wba, qg, kg, tm=256)
    cache_kt = jnp.swapaxes(cache_k[0], -1, -2)
    cache_vt = jnp.swapaxes(cache_v[0], -1, -2)
    oas = _moba_sample(qs, ks, vs, cache_kt, cache_vt, page_table, t_new=t_new)
    conv_s = jnp.pad(state_conv[0], ((0, 0), (SUBLANES - (CONV_W - 1), 0), (0, 0)))
    ogs, ssm_s = _gdn(gs.reshape(n_seq, t_new, 3 * D_GDN), conv_s, cw, bas.reshape(n_seq, t_new, LANES),
                      gates.reshape(n_seq, t_new, D_GDN), alog_pad, dtb_pad, gg8, state_ssm[0],
                      bb=16, tb=t_new)
    to_heads = lambda a: a.reshape(n_seq, t_new, H_ATT, HEAD_DIM).transpose(0, 2, 1, 3)
    oas_hm = oas.reshape(1, ns, H_ATT, HEAD_DIM).transpose(0, 2, 1, 3)
    ys = _out_ffn(xs, oas_hm, ogs.reshape(ns, D_GDN), ag, wo, nf, wgt, wup, wdn, tm=512)

    gp3 = gp.reshape(batch, seq, 3 * D_GDN)
    gs3 = gs.reshape(n_seq, t_new, 3 * D_GDN)
    return (yp.reshape(batch, seq, D_MODEL), ys.reshape(n_seq, t_new, D_MODEL),
            jnp.swapaxes(ktp, -1, -2)[None], jnp.swapaxes(vtp, -1, -2)[None],
            to_heads(ks)[None], to_heads(vs)[None],
            ssm_p[None], ssm_s[None],
            gp3[:, seq - (CONV_W - 1):][None], gs3[:, t_new - (CONV_W - 1):][None])
```

```python
import functools

import numpy as np
import jax
import jax.numpy as jnp
from jax import lax
from jax.experimental import pallas as pl
from jax.experimental.pallas import tpu as pltpu

f32 = jnp.float32
bf16 = jnp.bfloat16

D_MODEL = 1024
HEAD_DIM = 64
H_ATT = 8
H_GDN = 8
D_ATT = H_ATT * HEAD_DIM
D_GDN = H_GDN * HEAD_DIM
MOBA_BLOCK = 256
MOBA_SHIFT = 8
MOBA_TOPK = 3
GDN_CHUNK = 64
GDN_SOLVE = 128
CONV_W = 4
PAGE_SIZE = 128
RMS_EPS = 1e-6
L2_EPS = 1e-6
ATT_SCALE = HEAD_DIM ** -0.5
LOG2E = float(np.log2(np.e))
NEG = -1e30
LANES = 128
SUBLANES = 8
VMEM_LIMIT = 56 * 1024 * 1024

HI = lax.Precision.HIGHEST
NN = (((1,), (0,)), ((), ()))
NT = (((1,), (1,)), ((), ()))
TN = (((0,), (0,)), ((), ()))

ALIBI_SLOPES = [2.0 ** (-8.0 * (i + 1) / H_ATT) for i in range(H_ATT)]


def _dot(a, b, dims=NN, precision=None):
    return lax.dot_general(a, b, dims, precision=precision, preferred_element_type=f32)


def _bdot(a, b, dims=NN):
    return lax.dot_general(a.astype(bf16), b.astype(bf16), dims, preferred_element_type=f32)


def _rms(x, axis=-1):
    return x * lax.rsqrt(jnp.mean(x * x, axis=axis, keepdims=True) + RMS_EPS)


def _silu(x):
    return x * jax.nn.sigmoid(x)


def _iota(shape, dim):
    return lax.broadcasted_iota(jnp.int32, shape, dim)


def _resident(shape):
    return pl.BlockSpec(shape, lambda i: (0,) * len(shape), pipeline_mode=pl.Buffered(1))


def _in_proj_prompt_body(x_ref, nm_ref, wqkvt_ref, wg_ref, wgate_ref, wba_ref, qgc_ref, kgc_ref,
                         qt_ref, kt_ref, vt_ref, g_ref, gate_ref, ba_ref):
    x = x_ref[...]
    xb = (_rms(x) * nm_ref[...]).astype(bf16)
    z = _dot(wqkvt_ref[...], xb, NT)
    for h in range(H_ATT):
        lo, hi = h * HEAD_DIM, (h + 1) * HEAD_DIM
        qt_ref[0, h] = _rms(z[lo:hi, :], axis=0) * qgc_ref[...]
        kt_ref[0, h] = _rms(z[D_ATT + lo:D_ATT + hi, :], axis=0) * kgc_ref[...]
        vt_ref[0, h] = z[2 * D_ATT + lo:2 * D_ATT + hi, :]
    g_ref[...] = jnp.dot(xb, wg_ref[...], preferred_element_type=f32)
    gate_ref[...] = jnp.dot(xb, wgate_ref[...], preferred_element_type=f32)
    ba_ref[...] = jnp.dot(xb, wba_ref[...], preferred_element_type=f32)


def _in_proj_prompt(x2d, nm, wqkvt, wg, wgate, wba, qgc, kgc, *, batch, seq, tm):
    n = x2d.shape[0]
    assert n == batch * seq and seq % tm == 0
    spb = seq // tm
    full = lambda shape: _resident(shape)
    rows = lambda width: pl.BlockSpec((tm, width), lambda i: (i, 0))
    t_spec = pl.BlockSpec((1, H_ATT, HEAD_DIM, tm), lambda i: (i // spb, 0, 0, i % spb))
    t_shape = jax.ShapeDtypeStruct((batch, H_ATT, HEAD_DIM, seq), f32)
    return pl.pallas_call(
        _in_proj_prompt_body,
        grid=(n // tm,),
        in_specs=[rows(D_MODEL), full((1, D_MODEL)), full(wqkvt.shape), full(wg.shape),
                  full(wgate.shape), full(wba.shape), full((HEAD_DIM, 1)), full((HEAD_DIM, 1))],
        out_specs=[t_spec, t_spec, t_spec, rows(3 * D_GDN), rows(D_GDN), rows(LANES)],
        out_shape=[t_shape, t_shape, t_shape,
                   jax.ShapeDtypeStruct((n, 3 * D_GDN), f32),
                   jax.ShapeDtypeStruct((n, D_GDN), f32),
                   jax.ShapeDtypeStruct((n, LANES), f32)],
        compiler_params=pltpu.CompilerParams(dimension_semantics=("arbitrary",),
                                             vmem_limit_bytes=VMEM_LIMIT),
        name="in_proj_prompt",
    )(x2d, nm, wqkvt, wg, wgate, wba, qgc, kgc)


def _in_proj_sample_body(x_ref, nm_ref, wqkv_ref, wg_ref, wgate_ref, wba_ref, qg_ref, kg_ref,
                         q_ref, k_ref, v_ref, g_ref, gate_ref, ba_ref):
    x = x_ref[...]
    xb = (_rms(x) * nm_ref[...]).astype(bf16)
    z = jnp.dot(xb, wqkv_ref[...], preferred_element_type=f32)
    qs, ks = [], []
    for h in range(H_ATT):
        lo, hi = h * HEAD_DIM, (h + 1) * HEAD_DIM
        qs.append(_rms(z[:, lo:hi]) * qg_ref[...])
        ks.append(_rms(z[:, D_ATT + lo:D_ATT + hi]) * kg_ref[...])
    q_ref[...] = jnp.concatenate(qs, axis=-1)
    k_ref[...] = jnp.concatenate(ks, axis=-1)
    v_ref[...] = z[:, 2 * D_ATT:]
    g_ref[...] = jnp.dot(xb, wg_ref[...], preferred_element_type=f32)
    gate_ref[...] = jnp.dot(xb, wgate_ref[...], preferred_element_type=f32)
    ba_ref[...] = jnp.dot(xb, wba_ref[...], preferred_element_type=f32)


def _in_proj_sample(x2d, nm, wqkv, wg, wgate, wba, qg, kg, *, tm):
    n = x2d.shape[0]
    assert n % tm == 0
    full = lambda shape: _resident(shape)
    rows = lambda width: pl.BlockSpec((tm, width), lambda i: (i, 0))
    tok = jax.ShapeDtypeStruct((n, D_ATT), f32)
    return pl.pallas_call(
        _in_proj_sample_body,
        grid=(n // tm,),
        in_specs=[rows(D_MODEL), full((1, D_MODEL)), full(wqkv.shape), full(wg.shape),
                  full(wgate.shape), full(wba.shape), full((1, HEAD_DIM)), full((1, HEAD_DIM))],
        out_specs=[rows(D_ATT), rows(D_ATT), rows(D_ATT), rows(3 * D_GDN), rows(D_GDN), rows(LANES)],
        out_shape=[tok, tok, tok,
                   jax.ShapeDtypeStruct((n, 3 * D_GDN), f32),
                   jax.ShapeDtypeStruct((n, D_GDN), f32),
                   jax.ShapeDtypeStruct((n, LANES), f32)],
        compiler_params=pltpu.CompilerParams(dimension_semantics=("arbitrary",),
                                             vmem_limit_bytes=VMEM_LIMIT),
        name="in_proj_sample",
    )(x2d, nm, wqkv, wg, wgate, wba, qg, kg)


def _fold_rows(s, op):
    parts = [s[r0:r0 + SUBLANES] for r0 in range(0, s.shape[0], SUBLANES)]
    while len(parts) > 1:
        parts = [op(parts[k], parts[k + 1]) for k in range(0, len(parts), 2)]
    return parts[0]


MOBA_HEADS = 2
MOBA_TRIP = 16


def _moba_prompt_body(slope_ref, qt_ref, qtn_ref, kt_ref, vt_ref, o_ref, ka_ref, va_ref, kmean_ref, s_ref,
                      qa_ref, *, seq):
    hg = pl.program_id(1)
    i = pl.program_id(2)
    n_blk = seq // MOBA_BLOCK
    trip = MOBA_TRIP * MOBA_BLOCK
    slopes = [slope_ref[pl.ds(hg * MOBA_HEADS + j, 1), 0:1] for j in range(MOBA_HEADS)]
    nrow = _iota((n_blk, MOBA_BLOCK), 0)
    nrow_f = nrow.astype(f32)

    def augmented_qt(qt, tile, j):
        tvec = jnp.full((n_blk, MOBA_BLOCK), tile, jnp.int32)
        gt = _dot(kmean_ref[j], qt, NN, HI)
        gt = jnp.where(nrow < tvec, gt, -jnp.inf)
        picked = jnp.zeros((n_blk, MOBA_BLOCK), f32)
        for _ in range(MOBA_TOPK):
            best = jnp.max(gt, axis=0, keepdims=True)
            first_best = jnp.min(jnp.where(gt == best, nrow_f, float(n_blk)), axis=0, keepdims=True)
            chosen = nrow_f == first_best
            picked = jnp.where(chosen, 1.0, picked)
            gt = jnp.where(chosen, -jnp.inf, gt)
        sel = (nrow < tvec) & (picked > 0.0)
        shift = -(slopes[j] * MOBA_BLOCK) * tile.astype(f32)
        c_t = jnp.where(nrow < 2, 1.0, jnp.where(nrow == 2, shift, 0.0))
        return jnp.concatenate([qt * ATT_SCALE, jnp.where(sel, 0.0, NEG), c_t], axis=0).astype(bf16)

    @pl.when(i == 0)
    def _():
        row = _iota((HEAD_DIM, seq), 0)
        pos = _iota((HEAD_DIM, seq), 1)
        blk = jnp.right_shift(pos, MOBA_SHIFT)
        jloc = jnp.bitwise_and(pos, MOBA_BLOCK - 1)
        onehot = jnp.where(row == blk, 1.0, 0.0)
        ones_row = jnp.where(row == 0, 1.0, 0.0).astype(bf16)
        eye_b = (_iota((MOBA_BLOCK, MOBA_BLOCK), 0) == _iota((MOBA_BLOCK, MOBA_BLOCK), 1)).astype(bf16)
        for j in range(MOBA_HEADS):
            kt = kt_ref[0, j]
            e = jnp.where(row == n_blk, slopes[j] * jloc.astype(f32), onehot)
            e = jnp.where(row == n_blk + 1, (slopes[j] * MOBA_BLOCK) * blk.astype(f32), e)
            e = jnp.where(row == n_blk + 2, 1.0, e)
            ka_t = jnp.concatenate([kt.astype(bf16), e.astype(bf16)], axis=0)
            for n in range(n_blk):
                blk_cols = slice(n * MOBA_BLOCK, (n + 1) * MOBA_BLOCK)
                ka_ref[j, blk_cols, :] = _dot(eye_b, ka_t[:, blk_cols], NT).astype(bf16)
            va_ref[j, 0:HEAD_DIM, :] = vt_ref[0, j].astype(bf16)
            va_ref[j, HEAD_DIM:, :] = ones_row
            sums = [jnp.sum(kt[:, n * MOBA_BLOCK:(n + 1) * MOBA_BLOCK], axis=1, keepdims=True)
                    for n in range(n_blk)]
            kmean_ref[j] = (jnp.concatenate(sums, axis=1) * (1.0 / MOBA_BLOCK)).T
        for j in range(MOBA_HEADS):
            qa_ref[j] = augmented_qt(qt_ref[0, j], i, j)

    arow = _iota((2 * HEAD_DIM, MOBA_BLOCK), 0)
    pen_rows = (arow >= HEAD_DIM) & (arow < HEAD_DIM + n_blk)
    causal = _iota((MOBA_BLOCK, MOBA_BLOCK), 0) <= _iota((MOBA_BLOCK, MOBA_BLOCK), 1)
    d0 = pl.multiple_of(i * MOBA_BLOCK, MOBA_BLOCK)
    qa = [qa_ref[j] for j in range(MOBA_HEADS)]
    s_own = []
    for j in range(MOBA_HEADS):
        qo = jnp.where(pen_rows, jnp.zeros_like(qa[j]), qa[j])
        s_own.append(jnp.where(causal, _dot(ka_ref[j, pl.ds(d0, MOBA_BLOCK), :], qo) * LOG2E, NEG))
    half, quarter = trip // 2, trip // 4
    qb = MOBA_TRIP // 4
    rem = i % MOBA_TRIP
    as_int = lambda cond: cond.astype(jnp.int32)
    n_full = i // MOBA_TRIP + as_int(rem > 3 * qb)
    n_half = as_int((rem > qb) & (rem <= 3 * qb))
    n_quarter = as_int(((rem > 0) & (rem <= qb)) | ((rem > 2 * qb) & (rem <= 3 * qb)))
    half0 = n_full * trip
    quarter0 = half0 + n_half * half

    def pass1(size, base):
        def body(t, mx):
            r0 = pl.multiple_of(base + t * size, quarter)
            out = []
            for j in range(MOBA_HEADS):
                s = _dot(ka_ref[j, pl.ds(r0, size), :], qa[j]) * LOG2E
                s_ref[j, pl.ds(r0, size), :] = s
                out.append(jnp.maximum(mx[j], _fold_rows(s, jnp.maximum)))
            return tuple(out)
        return body

    mx = lax.fori_loop(0, n_full, pass1(trip, 0), tuple(_fold_rows(s, jnp.maximum) for s in s_own))
    mx = lax.fori_loop(0, n_half, pass1(half, half0), mx)
    mx = lax.fori_loop(0, n_quarter, pass1(quarter, quarter0), mx)
    for j in range(MOBA_HEADS):
        qa_ref[j] = augmented_qt(qtn_ref[0, j], i + 1, j)
    m_row = [jnp.max(mx[j], axis=0, keepdims=True) for j in range(MOBA_HEADS)]

    def pass2(size, base):
        def body(t, acc):
            r0 = pl.multiple_of(base + t * size, quarter)
            return tuple(acc[j] + _dot(va_ref[j, :, pl.ds(r0, size)],
                                       jnp.exp2((s_ref[j, pl.ds(r0, size), :] - m_row[j]).astype(bf16)))
                         for j in range(MOBA_HEADS))
        return body

    acc = lax.fori_loop(0, n_full, pass2(trip, 0),
                        tuple(_dot(va_ref[j, :, pl.ds(d0, MOBA_BLOCK)], jnp.exp2((s_own[j] - m_row[j]).astype(bf16)))
                              for j in range(MOBA_HEADS)))
    acc = lax.fori_loop(0, n_half, pass2(half, half0), acc)
    acc = lax.fori_loop(0, n_quarter, pass2(quarter, quarter0), acc)
    o_ref[0] = jnp.concatenate([(acc[j][:HEAD_DIM] / acc[j][HEAD_DIM:HEAD_DIM + 1]).T for j in range(MOBA_HEADS)],
                               axis=-1)


def _moba_prompt(qt, kt, vt):
    batch, heads, _, seq = qt.shape
    n_blk = seq // MOBA_BLOCK
    assert n_blk % MOBA_TRIP == 0 and 2 * n_blk == HEAD_DIM and MOBA_BLOCK == 1 << MOBA_SHIFT
    assert heads % MOBA_HEADS == 0
    slopes = jnp.asarray(np.repeat(np.asarray(ALIBI_SLOPES, np.float32)[:, None], LANES, axis=1))
    tile = pl.BlockSpec((1, MOBA_HEADS, HEAD_DIM, MOBA_BLOCK), lambda b, h, i: (b, h, 0, i))
    next_tile = pl.BlockSpec((1, MOBA_HEADS, HEAD_DIM, MOBA_BLOCK),
                             lambda b, h, i: (b, h, 0, jnp.minimum(i + 1, n_blk - 1)))
    whole = pl.BlockSpec((1, MOBA_HEADS, HEAD_DIM, seq), lambda b, h, i: (b, h, 0, 0),
                         pipeline_mode=pl.Buffered(1))
    return pl.pallas_call(
        functools.partial(_moba_prompt_body, seq=seq),
        grid=(batch, heads // MOBA_HEADS, n_blk),
        in_specs=[pl.BlockSpec((H_ATT, LANES), lambda b, h, i: (0, 0)), tile, next_tile, whole, whole],
        out_specs=pl.BlockSpec((1, MOBA_BLOCK, MOBA_HEADS * HEAD_DIM), lambda b, h, i: (b, i, h)),
        out_shape=jax.ShapeDtypeStruct((batch, seq, heads * HEAD_DIM), f32),
        scratch_shapes=[pltpu.VMEM((MOBA_HEADS, seq, 2 * HEAD_DIM), bf16),
                        pltpu.VMEM((MOBA_HEADS, 2 * HEAD_DIM, seq), bf16),
                        pltpu.VMEM((MOBA_HEADS, n_blk, HEAD_DIM), f32),
                        pltpu.VMEM((MOBA_HEADS, seq, MOBA_BLOCK), f32),
                        pltpu.VMEM((MOBA_HEADS, 2 * HEAD_DIM, MOBA_BLOCK), bf16)],
        compiler_params=pltpu.CompilerParams(dimension_semantics=("arbitrary",) * 3,
                                             vmem_limit_bytes=VMEM_LIMIT),
        name="moba_prompt",
    )(slopes, qt, qt, kt, vt)


SAMPLE_SEQS = 2


def _moba_sample_body(pt_ref, q_ref, k_ref, v_ref, *refs, past_len, t_new, spb):
    del pt_ref
    n_pages = past_len // PAGE_SIZE
    kp_refs, vp_refs = refs[:spb * n_pages], refs[spb * n_pages:2 * spb * n_pages]
    o_ref = refs[2 * spb * n_pages]
    own = past_len // MOBA_BLOCK
    n_sel = min(MOBA_TOPK, own)
    tq = _iota((t_new, past_len), 0)
    sk = _iota((t_new, past_len), 1)
    dist = (past_len + tq - sk).astype(f32)
    tq2 = _iota((t_new, t_new), 0)
    sk2 = _iota((t_new, t_new), 1)
    units = [(s, h) for s in range(spb) for h in range(H_ATT)]
    us = range(len(units))
    rows = [slice(s * t_new, (s + 1) * t_new) for s, _ in units]
    cols = [slice(h * HEAD_DIM, (h + 1) * HEAD_DIM) for _, h in units]
    slope = [ALIBI_SLOPES[h] for _, h in units]
    q = [q_ref[rows[u], cols[u]] for u in us]
    qb = [(q[u] * ATT_SCALE).astype(bf16) for u in us]
    kb = [jnp.concatenate([kp_refs[s * n_pages + p][0, h].astype(bf16) for p in range(n_pages)], axis=1)
          for s, h in units]
    s_raw = [_dot(qb[u], kb[u]) for u in us]
    gate = [jnp.concatenate([jnp.sum(s_raw[u][:, n * MOBA_BLOCK:(n + 1) * MOBA_BLOCK], axis=1, keepdims=True)
                             for n in range(own)], axis=1) for u in us]

    def penalty(g, n):
        gn = g[:, n:n + 1]
        rank = jnp.zeros((t_new, 1), f32)
        for m in range(own):
            if m != n:
                gm = g[:, m:m + 1]
                beats = (gm > gn) | ((gm == gn) & (m < n))
                rank = rank + jnp.where(beats, 1.0, 0.0)
        return jnp.broadcast_to(jnp.where(rank < n_sel, 0.0, NEG), (t_new, MOBA_BLOCK))

    pen = [jnp.concatenate([penalty(gate[u], n) for n in range(own)], axis=1) for u in us]
    s_past = [s_raw[u] + pen[u] - slope[u] * dist for u in us]
    s_own = [jnp.where(sk2 <= tq2,
                       _dot(qb[u], k_ref[rows[u], cols[u]].astype(bf16), NT) - slope[u] * (tq2 - sk2).astype(f32),
                       NEG) for u in us]
    m = [jnp.maximum(jnp.max(s_past[u], axis=1, keepdims=True), jnp.max(s_own[u], axis=1, keepdims=True))
         for u in us]
    p_past = [jnp.exp(s_past[u] - m[u]) for u in us]
    p_own = [jnp.exp(s_own[u] - m[u]) for u in us]
    denom = [jnp.sum(p_past[u], axis=1, keepdims=True) + jnp.sum(p_own[u], axis=1, keepdims=True) for u in us]
    vb = [jnp.concatenate([vp_refs[s * n_pages + p][0, h].astype(bf16) for p in range(n_pages)], axis=1)
          for s, h in units]
    o = [(_dot(p_own[u].astype(bf16), v_ref[rows[u], cols[u]].astype(bf16))
          + _dot(p_past[u].astype(bf16), vb[u], NT)) / denom[u] for u in us]
    for s in range(spb):
        o_ref[s * t_new:(s + 1) * t_new, :] = jnp.concatenate(o[s * H_ATT:(s + 1) * H_ATT], axis=-1)


def _moba_sample(q, k, v, cache_kt, cache_vt, page_table, *, t_new):
    n_seq, n_pages = page_table.shape
    past_len = n_pages * PAGE_SIZE
    assert past_len % MOBA_BLOCK == 0 and past_len // MOBA_BLOCK >= 1
    spb = SAMPLE_SEQS
    assert n_seq % spb == 0
    rows = pl.BlockSpec((spb * t_new, D_ATT), lambda b, pt: (b, 0))

    def page_spec(s, p):
        return pl.BlockSpec((1, H_ATT, HEAD_DIM, PAGE_SIZE),
                            lambda b, pt: (pt[(b * spb + s) * n_pages + p], 0, 0, 0))

    pages = [page_spec(s, p) for s in range(spb) for p in range(n_pages)]
    grid_spec = pltpu.PrefetchScalarGridSpec(
        num_scalar_prefetch=1,
        grid=(n_seq // spb,),
        in_specs=[rows, rows, rows] + pages * 2,
        out_specs=rows,
    )
    return pl.pallas_call(
        functools.partial(_moba_sample_body, past_len=past_len, t_new=t_new, spb=spb),
        grid_spec=grid_spec,
        out_shape=jax.ShapeDtypeStruct(q.shape, f32),
        compiler_params=pltpu.CompilerParams(dimension_semantics=("arbitrary",),
                                             vmem_limit_bytes=VMEM_LIMIT),
        name="moba_sample",
    )(page_table.reshape(-1), q, k, v, *([cache_kt] * (spb * n_pages)), *([cache_vt] * (spb * n_pages)))


def _conv_silu(cur, prev8, w):
    t = cur.shape[0]
    row8 = _iota((SUBLANES, cur.shape[1]), 0)
    y = None
    for i in range(CONV_W):
        d = CONV_W - 1 - i
        if d == 0:
            term = cur
        else:
            sh = pltpu.roll(cur, d, 0)
            top = jnp.where(row8 < d, pltpu.roll(prev8, d, 0), sh[0:SUBLANES])
            term = top if t == SUBLANES else jnp.concatenate([top, sh[SUBLANES:]], axis=0)
        term = term * w[i:i + 1, :]
        y = term if y is None else y + term
    return _silu(y)


def _conv_silu_stacked(cur, prev, w):
    n = cur.shape[0]
    pos = jnp.bitwise_and(_iota(cur.shape, 0), SUBLANES - 1)
    y = None
    for i in range(CONV_W):
        d = CONV_W - 1 - i
        if d == 0:
            term = cur
        else:
            term = jnp.where(pos < d, pltpu.roll(prev, (n - SUBLANES + d) % n, 0), pltpu.roll(cur, d, 0))
        term = term * w[i:i + 1, :]
        y = term if y is None else y + term
    return _silu(y)


def _gdn_body(uq_ref, uk_ref, uv_ref, pq_ref, pk_ref, pv_ref, cq_ref, ck_ref, cv_ref,
              wq_ref, wk_ref, wv_ref, ba_ref, gate_ref, alog_ref, dtb_ref, gg_ref, m0_ref,
              o_ref, m_ref, *, bb, tb, c, flat):
    t = pl.program_id(1)

    @pl.when(t == 0)
    def _():
        m_ref[...] = m0_ref[...]

    seqs = range(1) if flat else range(bb)
    stacked = bb * tb
    blk = (lambda ref, s: ref[...].reshape(stacked, ref.shape[-1])) if flat else (lambda ref, s: ref[s])
    tb = stacked if flat else tb
    n_chunks = tb // c
    lc = int(np.log2(c))
    assert 1 << lc == c
    ri = _iota((tb, tb), 0)
    ci = _iota((tb, tb), 1)
    same = jnp.right_shift(ri, lc) == jnp.right_shift(ci, lc)
    eye = ri == ci
    keep = same & (ri >= ci)
    cum = jnp.where(keep, 1.0, 0.0)
    tot = jnp.where(same, 1.0, 0.0)
    sb = min(tb, GDN_SOLVE)
    assert tb % sb == 0 and sb % c == 0
    rs_i = _iota((sb, sb), 0)
    cs_i = _iota((sb, sb), 1)
    eye_s = rs_i == cs_i
    keep_s = (jnp.right_shift(rs_i, lc) == jnp.right_shift(cs_i, lc)) & (rs_i >= cs_i)
    ident_s = jnp.where(eye_s, 1.0, 0.0)
    split = lambda z: [z[:, h * HEAD_DIM:(h + 1) * HEAD_DIM] for h in range(H_GDN)]
    if tb >= LANES:
        head_ones = jnp.where(_iota((D_GDN, D_GDN), 0) // HEAD_DIM == _iota((D_GDN, D_GDN), 1) // HEAD_DIM,
                              1.0, 0.0)
        l2_heads = lambda z: split(z * lax.rsqrt(_bdot(z * z, head_ones) + L2_EPS))
        rms_cat = lambda o: o * lax.rsqrt(_bdot(o * o, head_ones) * (1.0 / HEAD_DIM) + RMS_EPS)
        rms_heads = lambda parts: rms_cat(jnp.concatenate(parts, axis=-1))
    else:
        l2_heads = lambda z: [zh * lax.rsqrt(jnp.sum(zh * zh, axis=-1, keepdims=True) + L2_EPS) for zh in split(z)]
        rms_heads = lambda parts: jnp.concatenate([_rms(p) for p in parts], axis=-1)
    first = jnp.full((SUBLANES, D_GDN), t, jnp.int32) == 0

    for s in seqs:
        if flat:
            q = _conv_silu_stacked(blk(uq_ref, s), blk(cq_ref, s), wq_ref[...])
            k = _conv_silu_stacked(blk(uk_ref, s), blk(ck_ref, s), wk_ref[...])
            v = _conv_silu_stacked(blk(uv_ref, s), blk(cv_ref, s), wv_ref[...])
        else:
            q = _conv_silu(uq_ref[s], jnp.where(first, cq_ref[s], pq_ref[s]), wq_ref[...])
            k = _conv_silu(uk_ref[s], jnp.where(first, ck_ref[s], pk_ref[s]), wk_ref[...])
            v = _conv_silu(uv_ref[s], jnp.where(first, cv_ref[s], pv_ref[s]), wv_ref[...])
        ba = blk(ba_ref, s)
        beta_all = jax.nn.sigmoid(ba)
        xa = ba + dtb_ref[...]
        softplus = jnp.maximum(xa, 0.0) + jnp.log1p(jnp.exp(-jnp.abs(xa)))
        g_all = -jnp.exp(alog_ref[...]) * softplus
        gcum_all = _dot(cum, g_all, NN, HI)
        gtot_all = _dot(tot, g_all, NN, HI)
        gcum_t = gcum_all.T if tb % LANES == 0 else None
        hs = range(H_GDN)
        beta = [beta_all[:, h:h + 1] for h in hs]
        gcum = [gcum_all[:, H_GDN + h:H_GDN + h + 1] for h in hs]
        gtot = [gtot_all[:, H_GDN + h:H_GDN + h + 1] for h in hs]
        if gcum_t is None:
            grow = [jnp.sum(jnp.where(eye, gcum[h], 0.0), axis=0, keepdims=True) for h in hs]
        else:
            grow = [gcum_t[H_GDN + h:H_GDN + h + 1, :] for h in hs]
        qh = [z * (HEAD_DIM ** -0.5) for z in l2_heads(q)]
        kh = l2_heads(k)
        vh = [v[:, h * HEAD_DIM:(h + 1) * HEAD_DIM] for h in hs]
        eg = [jnp.exp(gcum[h]) for h in hs]
        rhs = [jnp.concatenate([beta[h] * vh[h], (beta[h] * eg[h]) * kh[h]], axis=-1) for h in hs]
        units = [(h, slice(r0, r0 + sb)) for h in hs for r0 in range(0, tb, sb)]
        decay = [jnp.exp(jnp.where(keep_s, gcum[h][rs] - grow[h][:, rs], NEG)) for h, rs in units]
        pw = [beta[h][rs] * _bdot(kh[h][rs], kh[h][rs], NT) * jnp.where(eye_s, 0.0, decay[u])
              for u, (h, rs) in enumerate(units)]
        x = [ident_s - a for a in pw]
        for _ in range(lc - 1):
            pw = [_bdot(a, a) for a in pw]
            x = [xu + _bdot(xu, a) for xu, a in zip(x, pw)]
        sol = [_bdot(x[u], rhs[h][rs]) for u, (h, rs) in enumerate(units)]
        aqk = [_bdot(qh[h][rs], kh[h][rs], NT) * decay[u] for u, (h, rs) in enumerate(units)]
        qg = [qh[h] * eg[h] for h in hs]
        kd = [kh[h] * jnp.exp(gtot[h] - gcum[h]) for h in hs]
        gdec = [jnp.exp(gtot[h]) for h in hs]
        m = None if flat else [m_ref[s, h] for h in hs]
        n_sub = tb // sb
        deltas = [[] for _ in hs]
        oqs = [[] for _ in hs]
        for ch in range(n_chunks):
            if flat:
                m = [m_ref[ch, h] for h in hs]
            sl = slice(ch * c, (ch + 1) * c)
            lsl = slice((ch * c) % sb, (ch * c) % sb + c)
            su = [sol[h * n_sub + (ch * c) // sb] for h in hs]
            both = [_bdot(jnp.concatenate([su[h][lsl, HEAD_DIM:], qg[h][sl]], axis=0), m[h]) for h in hs]
            for h in hs:
                deltas[h].append(su[h][lsl, :HEAD_DIM] - both[h][:c])
                oqs[h].append(both[h][c:])
            m = [gdec[h][ch * c:ch * c + 1, :] * m[h] + _dot(kd[h][sl], deltas[h][ch], TN) for h in hs]
            if flat:
                for h in hs:
                    m_ref[ch, h] = m[h]
        cps = sb // c
        cat = lambda parts: parts[0] if len(parts) == 1 else jnp.concatenate(parts, axis=0)
        outs = []
        for h in hs:
            if not flat:
                m_ref[s, h] = m[h]
            intra = cat([_bdot(aqk[h * n_sub + r], cat(deltas[h][r * cps:(r + 1) * cps])) for r in range(n_sub)])
            outs.append(cat(oqs[h]) + intra)
        o = rms_heads(outs) * gg_ref[...] * _silu(blk(gate_ref, s))
        if flat:
            o_ref[...] = o.reshape(o_ref.shape)
        else:
            o_ref[s] = o


def _gdn(u, conv_pad, conv_w, ba, gate, alog_pad, dtb_pad, gg8, m0, *, bb, tb):
    batch, seq, _ = u.shape
    c = min(GDN_CHUNK, seq)
    assert batch % bb == 0 and seq % tb == 0 and tb % c == 0 and tb % SUBLANES == 0
    flat = bb > 1
    assert not flat or (seq == tb == c == SUBLANES)
    cur = lambda g: pl.BlockSpec((bb, tb, D_GDN), lambda b, t: (b, t, g))
    prev = lambda g: pl.BlockSpec((bb, SUBLANES, D_GDN),
                                  lambda b, t: (b, jnp.maximum(t * (tb // SUBLANES) - 1, 0), g))
    cbuf = lambda g: pl.BlockSpec((bb, SUBLANES, D_GDN), lambda b, t: (b, 0, g))
    cw = lambda g: pl.BlockSpec((CONV_W, D_GDN), lambda b, t: (0, g))
    row = lambda width: pl.BlockSpec((1, width), lambda b, t: (0, 0))
    state = pl.BlockSpec((bb, H_GDN, HEAD_DIM, HEAD_DIM), lambda b, t: (b, 0, 0, 0))
    return pl.pallas_call(
        functools.partial(_gdn_body, bb=bb, tb=tb, c=c, flat=flat),
        grid=(batch // bb, seq // tb),
        in_specs=[cur(0), cur(1), cur(2), prev(0), prev(1), prev(2), cbuf(0), cbuf(1), cbuf(2),
                  cw(0), cw(1), cw(2),
                  pl.BlockSpec((bb, tb, LANES), lambda b, t: (b, t, 0)),
                  pl.BlockSpec((bb, tb, D_GDN), lambda b, t: (b, t, 0)),
                  row(LANES), row(LANES), row(D_GDN), state],
        out_specs=[pl.BlockSpec((bb, tb, D_GDN), lambda b, t: (b, t, 0)), state],
        out_shape=[jax.ShapeDtypeStruct((batch, seq, D_GDN), f32),
                   jax.ShapeDtypeStruct((batch, H_GDN, HEAD_DIM, HEAD_DIM), f32)],
        compiler_params=pltpu.CompilerParams(dimension_semantics=("arbitrary",) * 2,
                                             vmem_limit_bytes=VMEM_LIMIT),
        name="gdn",
    )(u, u, u, u, u, u, conv_pad, conv_pad, conv_pad, conv_w, conv_w, conv_w,
      ba, gate, alog_pad, dtb_pad, gg8, m0)


FF_CHUNK = 256


def _out_ffn_body(x_ref, oa_ref, og_ref, ag_ref, wo_ref, nf_ref, wg_ref, wu_ref, wd_ref, y_ref):
    oa = _rms(oa_ref[...]) * ag_ref[...]
    mix = jnp.concatenate([oa, og_ref[...]], axis=-1).astype(bf16)
    hid = x_ref[...] + jnp.dot(mix, wo_ref[...], preferred_element_type=f32)
    ub = (_rms(hid) * nf_ref[...]).astype(bf16)
    d_ff = wg_ref.shape[1]
    acc = hid
    for cidx in range(d_ff // FF_CHUNK):
        sl = slice(cidx * FF_CHUNK, (cidx + 1) * FF_CHUNK)
        a = jnp.dot(ub, wg_ref[:, sl], preferred_element_type=f32)
        b = jnp.dot(ub, wu_ref[:, sl], preferred_element_type=f32)
        acc = acc + jnp.dot((_silu(a) * b).astype(bf16), wd_ref[sl, :], preferred_element_type=f32)
    y_ref[...] = acc


def _out_ffn(x2d, o_att, o_gdn, ag, wo, nf, wg, wu, wd, *, tm):
    n = x2d.shape[0]
    assert n % tm == 0 and wg.shape[1] % FF_CHUNK == 0
    full = lambda shape: _resident(shape)
    rows = lambda width: pl.BlockSpec((tm, width), lambda i: (i, 0))
    return pl.pallas_call(
        _out_ffn_body,
        grid=(n // tm,),
        in_specs=[rows(D_MODEL), rows(D_ATT),
                  rows(D_GDN), full((1, D_ATT)), full(wo.shape), full((1, D_MODEL)),
                  full(wg.shape), full(wu.shape), full(wd.shape)],
        out_specs=rows(D_MODEL),
        out_shape=jax.ShapeDtypeStruct((n, D_MODEL), f32),
        compiler_params=pltpu.CompilerParams(dimension_semantics=("arbitrary",),
                                             vmem_limit_bytes=VMEM_LIMIT),
        name="out_ffn",
    )(x2d, o_att, o_gdn, ag, wo, nf, wg, wu, wd)


def kernel(x_prompt, x_sample, cache_k, cache_v, state_ssm, state_conv, page_table, norm_mix, w_in,
           q_norm_g, k_norm_g, attn_out_g, conv_w, a_log, dt_bias, gdn_out_g, w_out, norm_ffn,
           w_gate, w_up, w_down):
    depth = w_in.shape[0]
    assert depth == 1
    batch, seq, _ = x_prompt.shape
    n_seq, t_new, _ = x_sample.shape

    w = w_in[0]
    c0, c1, c2 = 3 * D_ATT, 3 * D_ATT + 3 * D_GDN, 3 * D_ATT + 4 * D_GDN
    wqkv = w[:, :c0].astype(bf16)
    wqkvt = w.T[:c0].astype(bf16)
    wg = w[:, c0:c1].astype(bf16)
    wgate = w[:, c1:c2].astype(bf16)
    wba = jnp.pad(w[:, c2:], ((0, 0), (0, LANES - 2 * H_GDN))).astype(bf16)
    nm = norm_mix[0].reshape(1, D_MODEL)
    qg = q_norm_g[0].reshape(1, HEAD_DIM)
    qgc = q_norm_g[0].reshape(HEAD_DIM, 1)
    kg = k_norm_g[0].reshape(1, HEAD_DIM)
    kgc = k_norm_g[0].reshape(HEAD_DIM, 1)
    ag = attn_out_g[0].reshape(1, D_ATT)
    nf = norm_ffn[0].reshape(1, D_MODEL)
    cw = conv_w[0]
    pad_lo = lambda vec: jnp.pad(vec.astype(f32), (H_GDN, LANES - 2 * H_GDN)).reshape(1, LANES)
    alog_pad = pad_lo(a_log[0])
    dtb_pad = pad_lo(dt_bias[0])
    gg8 = jnp.tile(gdn_out_g[0].astype(f32), H_GDN).reshape(1, D_GDN)
    wo = w_out[0].astype(bf16)
    wgt = w_gate[0].astype(bf16)
    wup = w_up[0].astype(bf16)
    wdn = w_down[0].astype(bf16)

    xp = x_prompt.reshape(batch * seq, D_MODEL)
    qtp, ktp, vtp, gp, gatep, bap = _in_proj_prompt(xp, nm, wqkvt, wg, wgate, wba, qgc, kgc,
                                                    batch=batch, seq=seq, tm=1024)
    oap = _moba_prompt(qtp, ktp, vtp)
    conv0 = jnp.zeros((batch, SUBLANES, 3 * D_GDN), f32)
    ssm0 = jnp.zeros((batch, H_GDN, HEAD_DIM, HEAD_DIM), f32)
    ogp, ssm_p = _gdn(gp.reshape(batch, seq, 3 * D_GDN), conv0, cw, bap.reshape(batch, seq, LANES),
                      gatep.reshape(batch, seq, D_GDN), alog_pad, dtb_pad, gg8, ssm0, bb=1, tb=256)
    yp = _out_ffn(xp, oap.reshape(batch * seq, D_ATT), ogp.reshape(batch * seq, D_GDN), ag, wo, nf, wgt, wup, wdn,
                  tm=512)

    ns = n_seq * t_new
    xs = x_sample.reshape(ns, D_MODEL)
    qs, ks, vs, gs, gates, bas = _in_proj_sample(xs, nm, wqkv, wg, wgate, wba, qg, kg, tm=256)
    cache_kt = jnp.swapaxes(cache_k[0], -1, -2)
    cache_vt = jnp.swapaxes(cache_v[0], -1, -2)
    oas = _moba_sample(qs, ks, vs, cache_kt, cache_vt, page_table, t_new=t_new)
    conv_s = jnp.pad(state_conv[0], ((0, 0), (SUBLANES - (CONV_W - 1), 0), (0, 0)))
    ogs, ssm_s = _gdn(gs.reshape(n_seq, t_new, 3 * D_GDN), conv_s, cw, bas.reshape(n_seq, t_new, LANES),
                      gates.reshape(n_seq, t_new, D_GDN), alog_pad, dtb_pad, gg8, state_ssm[0],
                      bb=16, tb=t_new)
    to_heads = lambda a: a.reshape(n_seq, t_new, H_ATT, HEAD_DIM).transpose(0, 2, 1, 3)
    ys = _out_ffn(xs, oas, ogs.reshape(ns, D_GDN), ag, wo, nf, wgt, wup, wdn, tm=512)

    gp3 = gp.reshape(batch, seq, 3 * D_GDN)
    gs3 = gs.reshape(n_seq, t_new, 3 * D_GDN)
    return (yp.reshape(batch, seq, D_MODEL), ys.reshape(n_seq, t_new, D_MODEL),
            jnp.swapaxes(ktp, -1, -2)[None], jnp.swapaxes(vtp, -1, -2)[None],
            to_heads(ks)[None], to_heads(vs)[None],
            ssm_p[None], ssm_s[None],
            gp3[:, seq - (CONV_W - 1):][None], gs3[:, t_new - (CONV_W - 1):][None])
```

```python
import functools

import numpy as np
import jax
import jax.numpy as jnp
from jax import lax
from jax.experimental import pallas as pl
from jax.experimental.pallas import tpu as pltpu

f32 = jnp.float32
bf16 = jnp.bfloat16

D_MODEL = 1024
HEAD_DIM = 64
H_ATT = 8
H_GDN = 8
D_ATT = H_ATT * HEAD_DIM
D_GDN = H_GDN * HEAD_DIM
MOBA_BLOCK = 256
MOBA_SHIFT = 8
MOBA_TOPK = 3
GDN_CHUNK = 64
GDN_SOLVE = 128
CONV_W = 4
PAGE_SIZE = 128
RMS_EPS = 1e-6
L2_EPS = 1e-6
ATT_SCALE = HEAD_DIM ** -0.5
LOG2E = float(np.log2(np.e))
NEG = -1e30
LANES = 128
SUBLANES = 8
VMEM_LIMIT = 56 * 1024 * 1024

HI = lax.Precision.HIGHEST
NN = (((1,), (0,)), ((), ()))
NT = (((1,), (1,)), ((), ()))
TN = (((0,), (0,)), ((), ()))

ALIBI_SLOPES = [2.0 ** (-8.0 * (i + 1) / H_ATT) for i in range(H_ATT)]


def _dot(a, b, dims=NN, precision=None):
    return lax.dot_general(a, b, dims, precision=precision, preferred_element_type=f32)


def _bdot(a, b, dims=NN):
    return lax.dot_general(a.astype(bf16), b.astype(bf16), dims, preferred_element_type=f32)


def _rms(x, axis=-1):
    return x * lax.rsqrt(jnp.mean(x * x, axis=axis, keepdims=True) + RMS_EPS)


def _silu(x):
    return x * jax.nn.sigmoid(x)


def _iota(shape, dim):
    return lax.broadcasted_iota(jnp.int32, shape, dim)


def _resident(shape):
    return pl.BlockSpec(shape, lambda i: (0,) * len(shape), pipeline_mode=pl.Buffered(1))


def _in_proj_prompt_body(x_ref, nm_ref, wqkvt_ref, wg_ref, wgate_ref, wba_ref, qgc_ref, kgc_ref,
                         qt_ref, kt_ref, vt_ref, g_ref, gate_ref, ba_ref):
    x = x_ref[...]
    xb = (_rms(x) * nm_ref[...]).astype(bf16)
    z = _dot(wqkvt_ref[...], xb, NT)
    for h in range(H_ATT):
        lo, hi = h * HEAD_DIM, (h + 1) * HEAD_DIM
        qt_ref[0, h] = _rms(z[lo:hi, :], axis=0) * qgc_ref[...]
        kt_ref[0, h] = _rms(z[D_ATT + lo:D_ATT + hi, :], axis=0) * kgc_ref[...]
        vt_ref[0, h] = z[2 * D_ATT + lo:2 * D_ATT + hi, :]
    g_ref[...] = jnp.dot(xb, wg_ref[...], preferred_element_type=f32)
    gate_ref[...] = jnp.dot(xb, wgate_ref[...], preferred_element_type=f32)
    ba_ref[...] = jnp.dot(xb, wba_ref[...], preferred_element_type=f32)


def _in_proj_prompt(x2d, nm, wqkvt, wg, wgate, wba, qgc, kgc, *, batch, seq, tm):
    n = x2d.shape[0]
    assert n == batch * seq and seq % tm == 0
    spb = seq // tm
    full = lambda shape: _resident(shape)
    rows = lambda width: pl.BlockSpec((tm, width), lambda i: (i, 0))
    t_spec = pl.BlockSpec((1, H_ATT, HEAD_DIM, tm), lambda i: (i // spb, 0, 0, i % spb))
    t_shape = jax.ShapeDtypeStruct((batch, H_ATT, HEAD_DIM, seq), f32)
    return pl.pallas_call(
        _in_proj_prompt_body,
        grid=(n // tm,),
        in_specs=[rows(D_MODEL), full((1, D_MODEL)), full(wqkvt.shape), full(wg.shape),
                  full(wgate.shape), full(wba.shape), full((HEAD_DIM, 1)), full((HEAD_DIM, 1))],
        out_specs=[t_spec, t_spec, t_spec, rows(3 * D_GDN), rows(D_GDN), rows(LANES)],
        out_shape=[t_shape, t_shape, t_shape,
                   jax.ShapeDtypeStruct((n, 3 * D_GDN), f32),
                   jax.ShapeDtypeStruct((n, D_GDN), f32),
                   jax.ShapeDtypeStruct((n, LANES), f32)],
        compiler_params=pltpu.CompilerParams(dimension_semantics=("arbitrary",),
                                             vmem_limit_bytes=VMEM_LIMIT),
        name="in_proj_prompt",
    )(x2d, nm, wqkvt, wg, wgate, wba, qgc, kgc)


def _in_proj_sample_body(x_ref, nm_ref, wqkv_ref, wg_ref, wgate_ref, wba_ref, qg_ref, kg_ref,
                         q_ref, k_ref, v_ref, g_ref, gate_ref, ba_ref):
    x = x_ref[...]
    xb = (_rms(x) * nm_ref[...]).astype(bf16)
    z = jnp.dot(xb, wqkv_ref[...], preferred_element_type=f32)
    qs, ks = [], []
    for h in range(H_ATT):
        lo, hi = h * HEAD_DIM, (h + 1) * HEAD_DIM
        qs.append(_rms(z[:, lo:hi]) * qg_ref[...])
        ks.append(_rms(z[:, D_ATT + lo:D_ATT + hi]) * kg_ref[...])
    q_ref[...] = jnp.concatenate(qs, axis=-1)
    k_ref[...] = jnp.concatenate(ks, axis=-1)
    v_ref[...] = z[:, 2 * D_ATT:]
    g_ref[...] = jnp.dot(xb, wg_ref[...], preferred_element_type=f32)
    gate_ref[...] = jnp.dot(xb, wgate_ref[...], preferred_element_type=f32)
    ba_ref[...] = jnp.dot(xb, wba_ref[...], preferred_element_type=f32)


def _in_proj_sample(x2d, nm, wqkv, wg, wgate, wba, qg, kg, *, tm):
    n = x2d.shape[0]
    assert n % tm == 0
    full = lambda shape: _resident(shape)
    rows = lambda width: pl.BlockSpec((tm, width), lambda i: (i, 0))
    tok = jax.ShapeDtypeStruct((n, D_ATT), f32)
    return pl.pallas_call(
        _in_proj_sample_body,
        grid=(n // tm,),
        in_specs=[rows(D_MODEL), full((1, D_MODEL)), full(wqkv.shape), full(wg.shape),
                  full(wgate.shape), full(wba.shape), full((1, HEAD_DIM)), full((1, HEAD_DIM))],
        out_specs=[rows(D_ATT), rows(D_ATT), rows(D_ATT), rows(3 * D_GDN), rows(D_GDN), rows(LANES)],
        out_shape=[tok, tok, tok,
                   jax.ShapeDtypeStruct((n, 3 * D_GDN), f32),
                   jax.ShapeDtypeStruct((n, D_GDN), f32),
                   jax.ShapeDtypeStruct((n, LANES), f32)],
        compiler_params=pltpu.CompilerParams(dimension_semantics=("arbitrary",),
                                             vmem_limit_bytes=VMEM_LIMIT),
        name="in_proj_sample",
    )(x2d, nm, wqkv, wg, wgate, wba, qg, kg)


def _fold_rows(s, op):
    parts = [s[r0:r0 + SUBLANES] for r0 in range(0, s.shape[0], SUBLANES)]
    while len(parts) > 1:
        parts = [op(parts[k], parts[k + 1]) for k in range(0, len(parts), 2)]
    return parts[0]


MOBA_HEADS = 2
MOBA_TRIP = 16


def _moba_prompt_body(slope_ref, qt_ref, qtn_ref, kt_ref, vt_ref, o_ref, ka_ref, va_ref, kmean_ref, s_ref,
                      qa_ref, *, seq):
    hg = pl.program_id(1)
    i = pl.program_id(2)
    n_blk = seq // MOBA_BLOCK
    trip = MOBA_TRIP * MOBA_BLOCK
    slopes = [slope_ref[pl.ds(hg * MOBA_HEADS + j, 1), 0:1] for j in range(MOBA_HEADS)]
    nrow = _iota((n_blk, MOBA_BLOCK), 0)
    nrow_f = nrow.astype(f32)

    def augmented_qt(qt, tile, j):
        tvec = jnp.full((n_blk, MOBA_BLOCK), tile, jnp.int32)
        gt = _dot(kmean_ref[j], qt, NN, HI)
        gt = jnp.where(nrow < tvec, gt, -jnp.inf)
        picked = jnp.zeros((n_blk, MOBA_BLOCK), f32)
        for _ in range(MOBA_TOPK):
            best = jnp.max(gt, axis=0, keepdims=True)
            first_best = jnp.min(jnp.where(gt == best, nrow_f, float(n_blk)), axis=0, keepdims=True)
            chosen = nrow_f == first_best
            picked = jnp.where(chosen, 1.0, picked)
            gt = jnp.where(chosen, -jnp.inf, gt)
        sel = (nrow < tvec) & (picked > 0.0)
        shift = -(slopes[j] * MOBA_BLOCK) * tile.astype(f32)
        c_t = jnp.where(nrow < 2, 1.0, jnp.where(nrow == 2, shift, 0.0))
        return jnp.concatenate([qt * ATT_SCALE, jnp.where(sel, 0.0, NEG), c_t], axis=0).astype(bf16)

    @pl.when(i == 0)
    def _():
        row = _iota((HEAD_DIM, seq), 0)
        pos = _iota((HEAD_DIM, seq), 1)
        blk = jnp.right_shift(pos, MOBA_SHIFT)
        jloc = jnp.bitwise_and(pos, MOBA_BLOCK - 1)
        onehot = jnp.where(row == blk, 1.0, 0.0)
        ones_row = jnp.where(row == 0, 1.0, 0.0).astype(bf16)
        eye_b = (_iota((MOBA_BLOCK, MOBA_BLOCK), 0) == _iota((MOBA_BLOCK, MOBA_BLOCK), 1)).astype(bf16)
        for j in range(MOBA_HEADS):
            kt = kt_ref[0, j]
            e = jnp.where(row == n_blk, slopes[j] * jloc.astype(f32), onehot)
            e = jnp.where(row == n_blk + 1, (slopes[j] * MOBA_BLOCK) * blk.astype(f32), e)
            e = jnp.where(row == n_blk + 2, 1.0, e)
            ka_t = jnp.concatenate([kt.astype(bf16), e.astype(bf16)], axis=0)
            for n in range(n_blk):
                blk_cols = slice(n * MOBA_BLOCK, (n + 1) * MOBA_BLOCK)
                ka_ref[j, blk_cols, :] = _dot(eye_b, ka_t[:, blk_cols], NT).astype(bf16)
            va_ref[j, 0:HEAD_DIM, :] = vt_ref[0, j].astype(bf16)
            va_ref[j, HEAD_DIM:, :] = ones_row
            sums = [jnp.sum(kt[:, n * MOBA_BLOCK:(n + 1) * MOBA_BLOCK], axis=1, keepdims=True)
                    for n in range(n_blk)]
            kmean_ref[j] = (jnp.concatenate(sums, axis=1) * (1.0 / MOBA_BLOCK)).T
        for j in range(MOBA_HEADS):
            qa_ref[j] = augmented_qt(qt_ref[0, j], i, j)

    arow = _iota((2 * HEAD_DIM, MOBA_BLOCK), 0)
    pen_rows = (arow >= HEAD_DIM) & (arow < HEAD_DIM + n_blk)
    causal = _iota((MOBA_BLOCK, MOBA_BLOCK), 0) <= _iota((MOBA_BLOCK, MOBA_BLOCK), 1)
    d0 = pl.multiple_of(i * MOBA_BLOCK, MOBA_BLOCK)
    qa = [qa_ref[j] for j in range(MOBA_HEADS)]
    s_own = []
    for j in range(MOBA_HEADS):
        qo = jnp.where(pen_rows, jnp.zeros_like(qa[j]), qa[j])
        s_own.append(jnp.where(causal, _dot(ka_ref[j, pl.ds(d0, MOBA_BLOCK), :], qo) * LOG2E, NEG))
    half, quarter = trip // 2, trip // 4
    qb = MOBA_TRIP // 4
    rem = i % MOBA_TRIP
    as_int = lambda cond: cond.astype(jnp.int32)
    n_full = i // MOBA_TRIP + as_int(rem > 3 * qb)
    n_half = as_int((rem > qb) & (rem <= 3 * qb))
    n_quarter = as_int(((rem > 0) & (rem <= qb)) | ((rem > 2 * qb) & (rem <= 3 * qb)))
    half0 = n_full * trip
    quarter0 = half0 + n_half * half

    def pass1(size, base):
        def body(t, mx):
            r0 = pl.multiple_of(base + t * size, quarter)
            out = []
            for j in range(MOBA_HEADS):
                s = _dot(ka_ref[j, pl.ds(r0, size), :], qa[j]) * LOG2E
                s_ref[j, pl.ds(r0, size), :] = s
                out.append(jnp.maximum(mx[j], _fold_rows(s, jnp.maximum)))
            return tuple(out)
        return body

    mx = lax.fori_loop(0, n_full, pass1(trip, 0), tuple(_fold_rows(s, jnp.maximum) for s in s_own))
    mx = lax.fori_loop(0, n_half, pass1(half, half0), mx)
    mx = lax.fori_loop(0, n_quarter, pass1(quarter, quarter0), mx)
    for j in range(MOBA_HEADS):
        qa_ref[j] = augmented_qt(qtn_ref[0, j], i + 1, j)
    m_row = [jnp.max(mx[j], axis=0, keepdims=True) for j in range(MOBA_HEADS)]

    def pass2(size, base):
        def body(t, acc):
            r0 = pl.multiple_of(base + t * size, quarter)
            return tuple(acc[j] + _dot(va_ref[j, :, pl.ds(r0, size)],
                                       jnp.exp2((s_ref[j, pl.ds(r0, size), :] - m_row[j]).astype(bf16)))
                         for j in range(MOBA_HEADS))
        return body

    acc = lax.fori_loop(0, n_full, pass2(trip, 0),
                        tuple(_dot(va_ref[j, :, pl.ds(d0, MOBA_BLOCK)], jnp.exp2((s_own[j] - m_row[j]).astype(bf16)))
                              for j in range(MOBA_HEADS)))
    acc = lax.fori_loop(0, n_half, pass2(half, half0), acc)
    acc = lax.fori_loop(0, n_quarter, pass2(quarter, quarter0), acc)
    o_t = jnp.concatenate([acc[j][:HEAD_DIM] / acc[j][HEAD_DIM:HEAD_DIM + 1] for j in range(MOBA_HEADS)], axis=0)
    o_ref[0] = o_t.T


def _moba_prompt(qt, kt, vt):
    batch, heads, _, seq = qt.shape
    n_blk = seq // MOBA_BLOCK
    assert n_blk % MOBA_TRIP == 0 and 2 * n_blk == HEAD_DIM and MOBA_BLOCK == 1 << MOBA_SHIFT
    assert heads % MOBA_HEADS == 0
    slopes = jnp.asarray(np.repeat(np.asarray(ALIBI_SLOPES, np.float32)[:, None], LANES, axis=1))
    tile = pl.BlockSpec((1, MOBA_HEADS, HEAD_DIM, MOBA_BLOCK), lambda b, h, i: (b, h, 0, i))
    next_tile = pl.BlockSpec((1, MOBA_HEADS, HEAD_DIM, MOBA_BLOCK),
                             lambda b, h, i: (b, h, 0, jnp.minimum(i + 1, n_blk - 1)))
    whole = pl.BlockSpec((1, MOBA_HEADS, HEAD_DIM, seq), lambda b, h, i: (b, h, 0, 0),
                         pipeline_mode=pl.Buffered(1))
    return pl.pallas_call(
        functools.partial(_moba_prompt_body, seq=seq),
        grid=(batch, heads // MOBA_HEADS, n_blk),
        in_specs=[pl.BlockSpec((H_ATT, LANES), lambda b, h, i: (0, 0)), tile, next_tile, whole, whole],
        out_specs=pl.BlockSpec((1, MOBA_BLOCK, MOBA_HEADS * HEAD_DIM), lambda b, h, i: (b, i, h)),
        out_shape=jax.ShapeDtypeStruct((batch, seq, heads * HEAD_DIM), f32),
        scratch_shapes=[pltpu.VMEM((MOBA_HEADS, seq, 2 * HEAD_DIM), bf16),
                        pltpu.VMEM((MOBA_HEADS, 2 * HEAD_DIM, seq), bf16),
                        pltpu.VMEM((MOBA_HEADS, n_blk, HEAD_DIM), f32),
                        pltpu.VMEM((MOBA_HEADS, seq, MOBA_BLOCK), f32),
                        pltpu.VMEM((MOBA_HEADS, 2 * HEAD_DIM, MOBA_BLOCK), bf16)],
        compiler_params=pltpu.CompilerParams(dimension_semantics=("arbitrary",) * 3,
                                             vmem_limit_bytes=VMEM_LIMIT),
        name="moba_prompt",
    )(slopes, qt, qt, kt, vt)


SAMPLE_SEQS = 2


def _moba_sample_body(pt_ref, q_ref, k_ref, v_ref, *refs, past_len, t_new, spb):
    del pt_ref
    n_pages = past_len // PAGE_SIZE
    kp_refs, vp_refs = refs[:spb * n_pages], refs[spb * n_pages:2 * spb * n_pages]
    o_ref = refs[2 * spb * n_pages]
    own = past_len // MOBA_BLOCK
    n_sel = min(MOBA_TOPK, own)
    tq = _iota((t_new, past_len), 0)
    sk = _iota((t_new, past_len), 1)
    dist = (past_len + tq - sk).astype(f32)
    tq2 = _iota((t_new, t_new), 0)
    sk2 = _iota((t_new, t_new), 1)
    units = [(s, h) for s in range(spb) for h in range(H_ATT)]
    us = range(len(units))
    rows = [slice(s * t_new, (s + 1) * t_new) for s, _ in units]
    cols = [slice(h * HEAD_DIM, (h + 1) * HEAD_DIM) for _, h in units]
    slope = [ALIBI_SLOPES[h] for _, h in units]
    q = [q_ref[rows[u], cols[u]] for u in us]
    qb = [(q[u] * ATT_SCALE).astype(bf16) for u in us]
    kb = [jnp.concatenate([kp_refs[s * n_pages + p][0, h].astype(bf16) for p in range(n_pages)], axis=1)
          for s, h in units]
    s_raw = [_dot(qb[u], kb[u]) for u in us]
    gate = [jnp.concatenate([jnp.sum(s_raw[u][:, n * MOBA_BLOCK:(n + 1) * MOBA_BLOCK], axis=1, keepdims=True)
                             for n in range(own)], axis=1) for u in us]

    def penalty(g, n):
        gn = g[:, n:n + 1]
        rank = jnp.zeros((t_new, 1), f32)
        for m in range(own):
            if m != n:
                gm = g[:, m:m + 1]
                beats = (gm > gn) | ((gm == gn) & (m < n))
                rank = rank + jnp.where(beats, 1.0, 0.0)
        return jnp.broadcast_to(jnp.where(rank < n_sel, 0.0, NEG), (t_new, MOBA_BLOCK))

    pen = [jnp.concatenate([penalty(gate[u], n) for n in range(own)], axis=1) for u in us]
    s_past = [s_raw[u] + pen[u] - slope[u] * dist for u in us]
    s_own = [jnp.where(sk2 <= tq2,
                       _dot(qb[u], k_ref[rows[u], cols[u]].astype(bf16), NT) - slope[u] * (tq2 - sk2).astype(f32),
                       NEG) for u in us]
    m = [jnp.maximum(jnp.max(s_past[u], axis=1, keepdims=True), jnp.max(s_own[u], axis=1, keepdims=True))
         for u in us]
    p_past = [jnp.exp(s_past[u] - m[u]) for u in us]
    p_own = [jnp.exp(s_own[u] - m[u]) for u in us]
    denom = [jnp.sum(p_past[u], axis=1, keepdims=True) + jnp.sum(p_own[u], axis=1, keepdims=True) for u in us]
    vb = [jnp.concatenate([vp_refs[s * n_pages + p][0, h].astype(bf16) for p in range(n_pages)], axis=1)
          for s, h in units]
    o = [(_dot(p_own[u].astype(bf16), v_ref[rows[u], cols[u]].astype(bf16))
          + _dot(p_past[u].astype(bf16), vb[u], NT)) / denom[u] for u in us]
    for s in range(spb):
        o_ref[s * t_new:(s + 1) * t_new, :] = jnp.concatenate(o[s * H_ATT:(s + 1) * H_ATT], axis=-1)


def _moba_sample(q, k, v, cache_kt, cache_vt, page_table, *, t_new):
    n_seq, n_pages = page_table.shape
    past_len = n_pages * PAGE_SIZE
    assert past_len % MOBA_BLOCK == 0 and past_len // MOBA_BLOCK >= 1
    spb = SAMPLE_SEQS
    assert n_seq % spb == 0
    rows = pl.BlockSpec((spb * t_new, D_ATT), lambda b, pt: (b, 0))

    def page_spec(s, p):
        return pl.BlockSpec((1, H_ATT, HEAD_DIM, PAGE_SIZE),
                            lambda b, pt: (pt[(b * spb + s) * n_pages + p], 0, 0, 0))

    pages = [page_spec(s, p) for s in range(spb) for p in range(n_pages)]
    grid_spec = pltpu.PrefetchScalarGridSpec(
        num_scalar_prefetch=1,
        grid=(n_seq // spb,),
        in_specs=[rows, rows, rows] + pages * 2,
        out_specs=rows,
    )
    return pl.pallas_call(
        functools.partial(_moba_sample_body, past_len=past_len, t_new=t_new, spb=spb),
        grid_spec=grid_spec,
        out_shape=jax.ShapeDtypeStruct(q.shape, f32),
        compiler_params=pltpu.CompilerParams(dimension_semantics=("arbitrary",),
                                             vmem_limit_bytes=VMEM_LIMIT),
        name="moba_sample",
    )(page_table.reshape(-1), q, k, v, *([cache_kt] * (spb * n_pages)), *([cache_vt] * (spb * n_pages)))


def _conv_silu(cur, prev8, w):
    t = cur.shape[0]
    row8 = _iota((SUBLANES, cur.shape[1]), 0)
    y = None
    for i in range(CONV_W):
        d = CONV_W - 1 - i
        if d == 0:
            term = cur
        else:
            sh = pltpu.roll(cur, d, 0)
            top = jnp.where(row8 < d, pltpu.roll(prev8, d, 0), sh[0:SUBLANES])
            term = top if t == SUBLANES else jnp.concatenate([top, sh[SUBLANES:]], axis=0)
        term = term * w[i:i + 1, :]
        y = term if y is None else y + term
    return _silu(y)


def _conv_silu_stacked(cur, prev, w):
    n = cur.shape[0]
    pos = jnp.bitwise_and(_iota(cur.shape, 0), SUBLANES - 1)
    y = None
    for i in range(CONV_W):
        d = CONV_W - 1 - i
        if d == 0:
            term = cur
        else:
            term = jnp.where(pos < d, pltpu.roll(prev, (n - SUBLANES + d) % n, 0), pltpu.roll(cur, d, 0))
        term = term * w[i:i + 1, :]
        y = term if y is None else y + term
    return _silu(y)


def _gdn_body(uq_ref, uk_ref, uv_ref, pq_ref, pk_ref, pv_ref, cq_ref, ck_ref, cv_ref,
              wq_ref, wk_ref, wv_ref, ba_ref, gate_ref, alog_ref, dtb_ref, gg_ref, m0_ref,
              o_ref, m_ref, *, bb, tb, c, flat):
    t = pl.program_id(1)

    @pl.when(t == 0)
    def _():
        m_ref[...] = m0_ref[...]

    seqs = range(1) if flat else range(bb)
    stacked = bb * tb
    blk = (lambda ref, s: ref[...].reshape(stacked, ref.shape[-1])) if flat else (lambda ref, s: ref[s])
    tb = stacked if flat else tb
    n_chunks = tb // c
    lc = int(np.log2(c))
    assert 1 << lc == c
    ri = _iota((tb, tb), 0)
    ci = _iota((tb, tb), 1)
    same = jnp.right_shift(ri, lc) == jnp.right_shift(ci, lc)
    eye = ri == ci
    keep = same & (ri >= ci)
    cum = jnp.where(keep, 1.0, 0.0)
    tot = jnp.where(same, 1.0, 0.0)
    sb = min(tb, GDN_SOLVE)
    assert tb % sb == 0 and sb % c == 0
    rs_i = _iota((sb, sb), 0)
    cs_i = _iota((sb, sb), 1)
    eye_s = rs_i == cs_i
    keep_s = (jnp.right_shift(rs_i, lc) == jnp.right_shift(cs_i, lc)) & (rs_i >= cs_i)
    ident_s = jnp.where(eye_s, 1.0, 0.0)
    split = lambda z: [z[:, h * HEAD_DIM:(h + 1) * HEAD_DIM] for h in range(H_GDN)]
    if tb >= LANES:
        head_ones = jnp.where(_iota((D_GDN, D_GDN), 0) // HEAD_DIM == _iota((D_GDN, D_GDN), 1) // HEAD_DIM,
                              1.0, 0.0)
        l2_heads = lambda z: split(z * lax.rsqrt(_bdot(z * z, head_ones) + L2_EPS))
        rms_cat = lambda o: o * lax.rsqrt(_bdot(o * o, head_ones) * (1.0 / HEAD_DIM) + RMS_EPS)
        rms_heads = lambda parts: rms_cat(jnp.concatenate(parts, axis=-1))
    else:
        l2_heads = lambda z: [zh * lax.rsqrt(jnp.sum(zh * zh, axis=-1, keepdims=True) + L2_EPS) for zh in split(z)]
        rms_heads = lambda parts: jnp.concatenate([_rms(p) for p in parts], axis=-1)
    first = jnp.full((SUBLANES, D_GDN), t, jnp.int32) == 0

    for s in seqs:
        if flat:
            q = _conv_silu_stacked(blk(uq_ref, s), blk(cq_ref, s), wq_ref[...])
            k = _conv_silu_stacked(blk(uk_ref, s), blk(ck_ref, s), wk_ref[...])
            v = _conv_silu_stacked(blk(uv_ref, s), blk(cv_ref, s), wv_ref[...])
        else:
            q = _conv_silu(uq_ref[s], jnp.where(first, cq_ref[s], pq_ref[s]), wq_ref[...])
            k = _conv_silu(uk_ref[s], jnp.where(first, ck_ref[s], pk_ref[s]), wk_ref[...])
            v = _conv_silu(uv_ref[s], jnp.where(first, cv_ref[s], pv_ref[s]), wv_ref[...])
        ba = blk(ba_ref, s)
        beta_all = jax.nn.sigmoid(ba)
        xa = ba + dtb_ref[...]
        softplus = jnp.maximum(xa, 0.0) + jnp.log1p(jnp.exp(-jnp.abs(xa)))
        g_all = -jnp.exp(alog_ref[...]) * softplus
        gcum_all = _dot(cum, g_all, NN, HI)
        gtot_all = _dot(tot, g_all, NN, HI)
        gcum_t = gcum_all.T if tb % LANES == 0 else None
        hs = range(H_GDN)
        beta = [beta_all[:, h:h + 1] for h in hs]
        gcum = [gcum_all[:, H_GDN + h:H_GDN + h + 1] for h in hs]
        gtot = [gtot_all[:, H_GDN + h:H_GDN + h + 1] for h in hs]
        if gcum_t is None:
            grow = [jnp.sum(jnp.where(eye, gcum[h], 0.0), axis=0, keepdims=True) for h in hs]
        else:
            grow = [gcum_t[H_GDN + h:H_GDN + h + 1, :] for h in hs]
        qh = [z * (HEAD_DIM ** -0.5) for z in l2_heads(q)]
        kh = l2_heads(k)
        vh = [v[:, h * HEAD_DIM:(h + 1) * HEAD_DIM] for h in hs]
        eg = [jnp.exp(gcum[h]) for h in hs]
        rhs = [jnp.concatenate([beta[h] * vh[h], (beta[h] * eg[h]) * kh[h]], axis=-1) for h in hs]
        units = [(h, slice(r0, r0 + sb)) for h in hs for r0 in range(0, tb, sb)]
        decay = [jnp.exp(jnp.where(keep_s, gcum[h][rs] - grow[h][:, rs], NEG)) for h, rs in units]
        pw = [beta[h][rs] * _bdot(kh[h][rs], kh[h][rs], NT) * jnp.where(eye_s, 0.0, decay[u])
              for u, (h, rs) in enumerate(units)]
        x = [ident_s - a for a in pw]
        for _ in range(lc - 1):
            pw = [_bdot(a, a) for a in pw]
            x = [xu + _bdot(xu, a) for xu, a in zip(x, pw)]
        sol = [_bdot(x[u], rhs[h][rs]) for u, (h, rs) in enumerate(units)]
        aqk = [_bdot(qh[h][rs], kh[h][rs], NT) * decay[u] for u, (h, rs) in enumerate(units)]
        qg = [qh[h] * eg[h] for h in hs]
        kd = [kh[h] * jnp.exp(gtot[h] - gcum[h]) for h in hs]
        gdec = [jnp.exp(gtot[h]) for h in hs]
        m = None if flat else [m_ref[s, h] for h in hs]
        n_sub = tb // sb
        deltas = [[] for _ in hs]
        oqs = [[] for _ in hs]
        for ch in range(n_chunks):
            if flat:
                m = [m_ref[ch, h] for h in hs]
            sl = slice(ch * c, (ch + 1) * c)
            lsl = slice((ch * c) % sb, (ch * c) % sb + c)
            su = [sol[h * n_sub + (ch * c) // sb] for h in hs]
            both = [_bdot(jnp.concatenate([su[h][lsl, HEAD_DIM:], qg[h][sl]], axis=0), m[h]) for h in hs]
            for h in hs:
                deltas[h].append(su[h][lsl, :HEAD_DIM] - both[h][:c])
                oqs[h].append(both[h][c:])
            m = [gdec[h][ch * c:ch * c + 1, :] * m[h] + _dot(kd[h][sl], deltas[h][ch], TN) for h in hs]
            if flat:
                for h in hs:
                    m_ref[ch, h] = m[h]
        cps = sb // c
        cat = lambda parts: parts[0] if len(parts) == 1 else jnp.concatenate(parts, axis=0)
        outs = []
        for h in hs:
            if not flat:
                m_ref[s, h] = m[h]
            intra = cat([_bdot(aqk[h * n_sub + r], cat(deltas[h][r * cps:(r + 1) * cps])) for r in range(n_sub)])
            outs.append(cat(oqs[h]) + intra)
        o = rms_heads(outs) * gg_ref[...] * _silu(blk(gate_ref, s))
        if flat:
            o_ref[...] = o.reshape(o_ref.shape)
        else:
            o_ref[s] = o


def _gdn(u, conv_pad, conv_w, ba, gate, alog_pad, dtb_pad, gg8, m0, *, bb, tb):
    batch, seq, _ = u.shape
    c = min(GDN_CHUNK, seq)
    assert batch % bb == 0 and seq % tb == 0 and tb % c == 0 and tb % SUBLANES == 0
    flat = bb > 1
    assert not flat or (seq == tb == c == SUBLANES)
    cur = lambda g: pl.BlockSpec((bb, tb, D_GDN), lambda b, t: (b, t, g))
    prev = lambda g: pl.BlockSpec((bb, SUBLANES, D_GDN),
                                  lambda b, t: (b, jnp.maximum(t * (tb // SUBLANES) - 1, 0), g))
    cbuf = lambda g: pl.BlockSpec((bb, SUBLANES, D_GDN), lambda b, t: (b, 0, g))
    cw = lambda g: pl.BlockSpec((CONV_W, D_GDN), lambda b, t: (0, g))
    row = lambda width: pl.BlockSpec((1, width), lambda b, t: (0, 0))
    state = pl.BlockSpec((bb, H_GDN, HEAD_DIM, HEAD_DIM), lambda b, t: (b, 0, 0, 0))
    return pl.pallas_call(
        functools.partial(_gdn_body, bb=bb, tb=tb, c=c, flat=flat),
        grid=(batch // bb, seq // tb),
        in_specs=[cur(0), cur(1), cur(2), prev(0), prev(1), prev(2), cbuf(0), cbuf(1), cbuf(2),
                  cw(0), cw(1), cw(2),
                  pl.BlockSpec((bb, tb, LANES), lambda b, t: (b, t, 0)),
                  pl.BlockSpec((bb, tb, D_GDN), lambda b, t: (b, t, 0)),
                  row(LANES), row(LANES), row(D_GDN), state],
        out_specs=[pl.BlockSpec((bb, tb, D_GDN), lambda b, t: (b, t, 0)), state],
        out_shape=[jax.ShapeDtypeStruct((batch, seq, D_GDN), f32),
                   jax.ShapeDtypeStruct((batch, H_GDN, HEAD_DIM, HEAD_DIM), f32)],
        compiler_params=pltpu.CompilerParams(dimension_semantics=("arbitrary",) * 2,
                                             vmem_limit_bytes=VMEM_LIMIT),
        name="gdn",
    )(u, u, u, u, u, u, conv_pad, conv_pad, conv_pad, conv_w, conv_w, conv_w,
      ba, gate, alog_pad, dtb_pad, gg8, m0)


FF_CHUNK = 256


def _out_ffn_body(x_ref, oa_ref, og_ref, ag_ref, wo_ref, nf_ref, wg_ref, wu_ref, wd_ref, y_ref):
    oa = _rms(oa_ref[...]) * ag_ref[...]
    mix = jnp.concatenate([oa, og_ref[...]], axis=-1).astype(bf16)
    hid = x_ref[...] + jnp.dot(mix, wo_ref[...], preferred_element_type=f32)
    ub = (_rms(hid) * nf_ref[...]).astype(bf16)
    d_ff = wg_ref.shape[1]
    acc = hid
    for cidx in range(d_ff // FF_CHUNK):
        sl = slice(cidx * FF_CHUNK, (cidx + 1) * FF_CHUNK)
        a = jnp.dot(ub, wg_ref[:, sl], preferred_element_type=f32)
        b = jnp.dot(ub, wu_ref[:, sl], preferred_element_type=f32)
        acc = acc + jnp.dot((_silu(a) * b).astype(bf16), wd_ref[sl, :], preferred_element_type=f32)
    y_ref[...] = acc


def _out_ffn(x2d, o_att, o_gdn, ag, wo, nf, wg, wu, wd, *, tm):
    n = x2d.shape[0]
    assert n % tm == 0 and wg.shape[1] % FF_CHUNK == 0
    full = lambda shape: _resident(shape)
    rows = lambda width: pl.BlockSpec((tm, width), lambda i: (i, 0))
    return pl.pallas_call(
        _out_ffn_body,
        grid=(n // tm,),
        in_specs=[rows(D_MODEL), rows(D_ATT),
                  rows(D_GDN), full((1, D_ATT)), full(wo.shape), full((1, D_MODEL)),
                  full(wg.shape), full(wu.shape), full(wd.shape)],
        out_specs=rows(D_MODEL),
        out_shape=jax.ShapeDtypeStruct((n, D_MODEL), f32),
        compiler_params=pltpu.CompilerParams(dimension_semantics=("arbitrary",),
                                             vmem_limit_bytes=VMEM_LIMIT),
        name="out_ffn",
    )(x2d, o_att, o_gdn, ag, wo, nf, wg, wu, wd)


def kernel(x_prompt, x_sample, cache_k, cache_v, state_ssm, state_conv, page_table, norm_mix, w_in,
           q_norm_g, k_norm_g, attn_out_g, conv_w, a_log, dt_bias, gdn_out_g, w_out, norm_ffn,
           w_gate, w_up, w_down):
    depth = w_in.shape[0]
    assert depth == 1
    batch, seq, _ = x_prompt.shape
    n_seq, t_new, _ = x_sample.shape

    w = w_in[0]
    c0, c1, c2 = 3 * D_ATT, 3 * D_ATT + 3 * D_GDN, 3 * D_ATT + 4 * D_GDN
    wqkv = w[:, :c0].astype(bf16)
    wqkvt = w.T[:c0].astype(bf16)
    wg = w[:, c0:c1].astype(bf16)
    wgate = w[:, c1:c2].astype(bf16)
    wba = jnp.pad(w[:, c2:], ((0, 0), (0, LANES - 2 * H_GDN))).astype(bf16)
    nm = norm_mix[0].reshape(1, D_MODEL)
    qg = q_norm_g[0].reshape(1, HEAD_DIM)
    qgc = q_norm_g[0].reshape(HEAD_DIM, 1)
    kg = k_norm_g[0].reshape(1, HEAD_DIM)
    kgc = k_norm_g[0].reshape(HEAD_DIM, 1)
    ag = attn_out_g[0].reshape(1, D_ATT)
    nf = norm_ffn[0].reshape(1, D_MODEL)
    cw = conv_w[0]
    pad_lo = lambda vec: jnp.pad(vec.astype(f32), (H_GDN, LANES - 2 * H_GDN)).reshape(1, LANES)
    alog_pad = pad_lo(a_log[0])
    dtb_pad = pad_lo(dt_bias[0])
    gg8 = jnp.tile(gdn_out_g[0].astype(f32), H_GDN).reshape(1, D_GDN)
    wo = w_out[0].astype(bf16)
    wgt = w_gate[0].astype(bf16)
    wup = w_up[0].astype(bf16)
    wdn = w_down[0].astype(bf16)

    xp = x_prompt.reshape(batch * seq, D_MODEL)
    qtp, ktp, vtp, gp, gatep, bap = _in_proj_prompt(xp, nm, wqkvt, wg, wgate, wba, qgc, kgc,
                                                    batch=batch, seq=seq, tm=1024)
    oap = _moba_prompt(qtp, ktp, vtp)
    conv0 = jnp.zeros((batch, SUBLANES, 3 * D_GDN), f32)
    ssm0 = jnp.zeros((batch, H_GDN, HEAD_DIM, HEAD_DIM), f32)
    ogp, ssm_p = _gdn(gp.reshape(batch, seq, 3 * D_GDN), conv0, cw, bap.reshape(batch, seq, LANES),
                      gatep.reshape(batch, seq, D_GDN), alog_pad, dtb_pad, gg8, ssm0, bb=1, tb=256)
    yp = _out_ffn(xp, oap.reshape(batch * seq, D_ATT), ogp.reshape(batch * seq, D_GDN), ag, wo, nf, wgt, wup, wdn,
                  tm=512)

    ns = n_seq * t_new
    xs = x_sample.reshape(ns, D_MODEL)
    qs, ks, vs, gs, gates, bas = _in_proj_sample(xs, nm, wqkv, wg, wgate, wba, qg, kg, tm=256)
    cache_kt = jnp.swapaxes(cache_k[0], -1, -2)
    cache_vt = jnp.swapaxes(cache_v[0], -1, -2)
    oas = _moba_sample(qs, ks, vs, cache_kt, cache_vt, page_table, t_new=t_new)
    conv_s = jnp.pad(state_conv[0], ((0, 0), (SUBLANES - (CONV_W - 1), 0), (0, 0)))
    ogs, ssm_s = _gdn(gs.reshape(n_seq, t_new, 3 * D_GDN), conv_s, cw, bas.reshape(n_seq, t_new, LANES),
                      gates.reshape(n_seq, t_new, D_GDN), alog_pad, dtb_pad, gg8, state_ssm[0],
                      bb=16, tb=t_new)
    to_heads = lambda a: a.reshape(n_seq, t_new, H_ATT, HEAD_DIM).transpose(0, 2, 1, 3)
    ys = _out_ffn(xs, oas, ogs.reshape(ns, D_GDN), ag, wo, nf, wgt, wup, wdn, tm=512)

    gp3 = gp.reshape(batch, seq, 3 * D_GDN)
    gs3 = gs.reshape(n_seq, t_new, 3 * D_GDN)
    return (yp.reshape(batch, seq, D_MODEL), ys.reshape(n_seq, t_new, D_MODEL),
            jnp.swapaxes(ktp, -1, -2)[None], jnp.swapaxes(vtp, -1, -2)[None],
            to_heads(ks)[None], to_heads(vs)[None],
            ssm_p[None], ssm_s[None],
            gp3[:, seq - (CONV_W - 1):][None], gs3[:, t_new - (CONV_W - 1):][None])
```
